```python
import jax, jax.numpy as jnp
from jax import lax
import numpy as np

D_MODEL = 4096
BATCH = 2
SEQ = 4096
DEPTH = 2
DEC_BATCH = 16
DEC_SEQ = 32
PAST_LEN = 4096

CHUNK = 64
N_MIXERS = 2
N_RWKV = (DEPTH + 1) // 2
N_POOL = DEPTH // 2
HEAD_SIZE = 64
N_HEADS = D_MODEL // HEAD_SIZE
DECAY_LORA = 128
AAA_LORA = 128
GATE_LORA = 480
GN_EPS = 64e-5
N_SHIFT_MIX = 6
POOL_WINDOWS = (2, 4, 8, 16)
N_POOL_GROUPS = 4
POOL_GROUP = D_MODEL // N_POOL_GROUPS
POOL_HIST = 15
N_GROUPS = 4
EXPERTS_PER_GROUP = 8
N_EXPERTS = N_GROUPS * EXPERTS_PER_GROUP
TOP_K = 2
D_EXPERT = 1024
MOE_BLOCK = 128
PLE_DIM = 256
NORM_EPS = 1e-6

kernel_name = "rwkv7_pool_hier_moe_stream_step"


def rmsnorm(x, g):
    xf = x.astype(jnp.float32)
    y = xf * lax.rsqrt(jnp.mean(xf * xf, axis=-1, keepdims=True) + NORM_EPS)
    return (y * g.astype(jnp.float32)).astype(x.dtype)


def rwkv7_mix(h, shift, wkv, W, j):
    B, T, D = h.shape
    f32 = jnp.float32
    mu = W["rwkv_mu"][j]
    x_prev = jnp.concatenate([shift[:, None, :].astype(h.dtype), h[:, :-1]], axis=1)
    xx = x_prev - h
    xr, xw, xk, xv, xa, xg = [h + xx * mu[n] for n in range(N_SHIFT_MIX)]
    w_rkv = W["rwkv_w_rkv"][j]
    r = xr @ w_rkv[0]
    k = xk @ w_rkv[1]
    v = xv @ w_rkv[2]
    w_log = -jax.nn.softplus(-(W["rwkv_w0"][j] + jnp.tanh(xw @ W["rwkv_w1"][j]) @ W["rwkv_w2"][j]).astype(f32)) - 0.5
    decay = jnp.exp(-jnp.exp(w_log))
    a = jax.nn.sigmoid((W["rwkv_a0"][j] + (xa @ W["rwkv_a1"][j]) @ W["rwkv_a2"][j]).astype(f32))
    g = jax.nn.sigmoid(xg @ W["rwkv_g1"][j]) @ W["rwkv_g2"][j]
    hs = lambda t: t.reshape(B, T, N_HEADS, HEAD_SIZE)
    kk = hs(k.astype(f32) * W["rwkv_k_k"][j].astype(f32))
    kk = kk * lax.rsqrt(jnp.maximum(jnp.sum(kk * kk, axis=-1, keepdims=True), 1e-24))
    kf = k.astype(f32) * (1.0 + (a - 1.0) * W["rwkv_k_a"][j].astype(f32))
    rh, kh, vh, dh, ah = hs(r.astype(f32)), hs(kf), hs(v.astype(f32)), hs(decay), hs(a)
    seq = tuple(jnp.swapaxes(t, 0, 1) for t in (rh, dh, kh, vh, -kk, kk * ah))

    def step(S, inp):
        r_t, d_t, k_t, v_t, a_t, b_t = inp
        Sa = jnp.einsum('bhvk,bhk->bhv', S, a_t)
        S = S * d_t[:, :, None, :] + Sa[..., None] * b_t[:, :, None, :] + v_t[..., None] * k_t[:, :, None, :]
        return S, jnp.einsum('bhvk,bhk->bhv', S, r_t)

    S_T, o = lax.scan(step, wkv.astype(f32), seq)
    o = jnp.swapaxes(o, 0, 1)
    o_mean = jnp.mean(o, axis=-1, keepdims=True)
    o_c = o - o_mean
    o = o_c * lax.rsqrt(jnp.mean(o_c * o_c, axis=-1, keepdims=True) + GN_EPS)
    o = o * W["rwkv_lnx_w"][j].astype(f32).reshape(N_HEADS, HEAD_SIZE) + W["rwkv_lnx_b"][j].astype(f32).reshape(N_HEADS, HEAD_SIZE)
    o = o + jnp.sum(rh * kh * W["rwkv_r_k"][j].astype(f32), axis=-1, keepdims=True) * vh
    o = o.reshape(B, T, D).astype(h.dtype)
    y = (o * g) @ W["rwkv_w_o"][j]
    return y, h[:, -1], S_T.astype(wkv.dtype)


def pool_mix(h, hist, pos0, W, j):
    B, T, D = h.shape
    f32 = jnp.float32
    ext = jnp.concatenate([hist.astype(f32), h.astype(f32)], axis=1)
    cs = jnp.concatenate([jnp.zeros((B, 1, D), f32), jnp.cumsum(ext, axis=1)], axis=1)
    end = cs[:, POOL_HIST + 1:]
    pos = pos0 + jnp.arange(T)
    means = []
    for gi, w in enumerate(POOL_WINDOWS):
        sl = slice(gi * POOL_GROUP, (gi + 1) * POOL_GROUP)
        start = cs[:, POOL_HIST + 1 - w:POOL_HIST + 1 - w + T, sl]
        cnt = jnp.minimum(pos + 1, w).astype(f32)[None, :, None]
        means.append((end[:, :, sl] - start) / cnt)
    d = (jnp.concatenate(means, axis=-1) - h.astype(f32)).astype(h.dtype)
    d = d.reshape(B, T, N_POOL_GROUPS, POOL_GROUP)
    y = jnp.einsum('btgc,gce->btge', d, W["pool_w"][j]).reshape(B, T, D)
    return y * W["pool_scale"][j], ext[:, -POOL_HIST:].astype(hist.dtype)


def hier_moe(h, W, i):
    B, T, D = h.shape
    f32 = jnp.float32
    xf = h.reshape(-1, D)
    N = xf.shape[0]
    grp_prob = jax.nn.softmax((xf @ W["moe_w_grp"][i]).astype(f32) + W["moe_b_grp"][i].astype(f32), axis=-1)
    g_idx = jnp.argmax(grp_prob, axis=-1)
    g_gate = jnp.take_along_axis(grp_prob, g_idx[:, None], axis=-1)
    e_logits = ((xf @ W["moe_w_exp"][i]).astype(f32) + W["moe_b_exp"][i].astype(f32)).reshape(N, N_GROUPS, EXPERTS_PER_GROUP)
    e_logits = jnp.take_along_axis(e_logits, g_idx[:, None, None], axis=1)[:, 0]
    top_p, top_i = lax.top_k(jax.nn.softmax(e_logits, axis=-1), TOP_K)
    wts = g_gate * top_p / jnp.sum(top_p, axis=-1, keepdims=True)
    eid = (g_idx[:, None] * EXPERTS_PER_GROUP + top_i).astype(jnp.int32)
    S = N * TOP_K
    NB = -(-S // MOE_BLOCK) + N_EXPERTS
    e_flat = eid.reshape(-1)
    order = jnp.argsort(e_flat)
    e_sorted = e_flat[order]
    tok_sorted = (order // TOP_K).astype(jnp.int32)
    counts = jnp.bincount(e_flat, length=N_EXPERTS)
    padded = (counts + MOE_BLOCK - 1) // MOE_BLOCK * MOE_BLOCK
    pad_end = jnp.cumsum(padded)
    start = jnp.cumsum(counts) - counts
    dest = (pad_end - padded)[e_sorted] + jnp.arange(S) - start[e_sorted]
    buf_tok = jnp.zeros((NB * MOE_BLOCK,), jnp.int32).at[dest].set(tok_sorted)
    blk_e = jnp.minimum(jnp.searchsorted(pad_end, jnp.arange(NB) * MOE_BLOCK, side='right'), N_EXPERTS - 1)
    xb = xf[buf_tok].reshape(NB, MOE_BLOCK, D)
    w_gate, w_up, w_down = W["moe_w_gate"][i], W["moe_w_up"][i], W["moe_w_down"][i]

    def run(args):
        xblk, e = args
        hdn = jax.nn.silu(xblk @ w_gate[e]) * (xblk @ w_up[e])
        return hdn @ w_down[e]

    yb = lax.map(run, (xb, blk_e)).reshape(NB * MOE_BLOCK, D)
    contrib = yb[dest] * wts.reshape(-1)[order][:, None].astype(yb.dtype)
    out = jnp.zeros_like(xf).at[tok_sorted].add(contrib)
    return out.reshape(B, T, D)


def trunk(x, p, shift_st, wkv_st, pool_st, pos0, W):
    new_shift, new_wkv, new_pool = [], [], []
    for i in range(DEPTH):
        j = i // N_MIXERS
        h = rmsnorm(x, W["norm_mix"][i])
        if i % N_MIXERS == 0:
            y, s_new, w_new = rwkv7_mix(h, shift_st[j], wkv_st[j], W, j)
            new_shift.append(s_new)
            new_wkv.append(w_new)
        else:
            y, hist_new = pool_mix(h, pool_st[j], pos0, W, j)
            new_pool.append(hist_new)
        x = x + y
        x = x + hier_moe(rmsnorm(x, W["norm_ffn"][i]), W, i)
        gate = jax.nn.sigmoid((rmsnorm(x, W["norm_ple"][i]) @ W["ple_gate_down"][i]) @ W["ple_gate_up"][i])
        x = x + (p[i] @ W["ple_w_in"][i]) * gate
    return rmsnorm(x, W["norm_final"]), jnp.stack(new_shift), jnp.stack(new_wkv), jnp.stack(new_pool)


def setup_inputs(seed: int = 0) -> dict:
    key = jax.random.key(seed)
    keys = jax.random.split(key, 48)
    counter = [0]

    def nk():
        k = keys[counter[0]]
        counter[0] += 1
        return k

    f32 = jnp.float32
    nrm = lambda shape, s: jax.random.normal(nk(), shape, f32) * s
    D = D_MODEL
    inp = {}
    inp["x_prompt"] = nrm((BATCH, SEQ, D), 1.0)
    inp["x_sample"] = nrm((DEC_BATCH, DEC_SEQ, D), 1.0)
    inp["state_rwkv_shift"] = nrm((N_RWKV, DEC_BATCH, D), 1.0)
    inp["state_rwkv_wkv"] = nrm((N_RWKV, DEC_BATCH, N_HEADS, HEAD_SIZE, HEAD_SIZE), 0.3)
    inp["state_pool"] = nrm((N_POOL, DEC_BATCH, POOL_HIST, D), 1.0)
    inp["p_prompt"] = nrm((DEPTH, BATCH, SEQ, PLE_DIM), 1.0)
    inp["p_sample"] = nrm((DEPTH, DEC_BATCH, DEC_SEQ, PLE_DIM), 1.0)
    inp["rwkv_mu"] = jax.random.uniform(nk(), (N_RWKV, N_SHIFT_MIX, D), f32)
    inp["rwkv_w_rkv"] = nrm((N_RWKV, 3, D, D), D ** -0.5)
    inp["rwkv_w_o"] = nrm((N_RWKV, D, D), D ** -0.5)
    inp["rwkv_w0"] = -1.0 + nrm((N_RWKV, D), 0.5)
    inp["rwkv_w1"] = nrm((N_RWKV, D, DECAY_LORA), D ** -0.5)
    inp["rwkv_w2"] = nrm((N_RWKV, DECAY_LORA, D), 0.3 * DECAY_LORA ** -0.5)
    inp["rwkv_a0"] = nrm((N_RWKV, D), 0.5)
    inp["rwkv_a1"] = nrm((N_RWKV, D, AAA_LORA), D ** -0.5)
    inp["rwkv_a2"] = nrm((N_RWKV, AAA_LORA, D), 0.5 * AAA_LORA ** -0.5)
    inp["rwkv_g1"] = nrm((N_RWKV, D, GATE_LORA), D ** -0.5)
    inp["rwkv_g2"] = nrm((N_RWKV, GATE_LORA, D), GATE_LORA ** -0.5)
    inp["rwkv_k_k"] = 0.85 + nrm((N_RWKV, D), 0.1)
    inp["rwkv_k_a"] = 1.0 + nrm((N_RWKV, D), 0.1)
    inp["rwkv_r_k"] = nrm((N_RWKV, N_HEADS, HEAD_SIZE), 0.1)
    inp["rwkv_lnx_w"] = 1.0 + nrm((N_RWKV, D), 0.1)
    inp["rwkv_lnx_b"] = nrm((N_RWKV, D), 0.02)
    inp["pool_w"] = nrm((N_POOL, N_POOL_GROUPS, POOL_GROUP, POOL_GROUP), POOL_GROUP ** -0.5)
    inp["pool_scale"] = 1.0 + nrm((N_POOL, D), 0.1)
    inp["norm_mix"] = 1.0 + nrm((DEPTH, D), 0.1)
    inp["norm_ffn"] = 1.0 + nrm((DEPTH, D), 0.1)
    inp["norm_ple"] = 1.0 + nrm((DEPTH, D), 0.1)
    inp["norm_final"] = 1.0 + nrm((D,), 0.1)
    inp["moe_w_grp"] = nrm((DEPTH, D, N_GROUPS), D ** -0.5)
    inp["moe_b_grp"] = nrm((DEPTH, N_GROUPS), 0.01)
    inp["moe_w_exp"] = nrm((DEPTH, D, N_EXPERTS), D ** -0.5)
    inp["moe_b_exp"] = nrm((DEPTH, N_EXPERTS), 0.01)
    inp["moe_w_gate"] = nrm((DEPTH, N_EXPERTS, D, D_EXPERT), D ** -0.5)
    inp["moe_w_up"] = nrm((DEPTH, N_EXPERTS, D, D_EXPERT), D ** -0.5)
    inp["moe_w_down"] = nrm((DEPTH, N_EXPERTS, D_EXPERT, D), D_EXPERT ** -0.5)
    inp["ple_w_in"] = nrm((DEPTH, PLE_DIM, D), PLE_DIM ** -0.5)
    inp["ple_gate_down"] = nrm((DEPTH, D, PLE_DIM), D ** -0.5)
    inp["ple_gate_up"] = nrm((DEPTH, PLE_DIM, D), PLE_DIM ** -0.5)
    return inp


def reference(x_prompt, x_sample, state_rwkv_shift, state_rwkv_wkv, state_pool, p_prompt, p_sample,
              rwkv_mu, rwkv_w_rkv, rwkv_w_o, rwkv_w0, rwkv_w1, rwkv_w2, rwkv_a0, rwkv_a1, rwkv_a2,
              rwkv_g1, rwkv_g2, rwkv_k_k, rwkv_k_a, rwkv_r_k, rwkv_lnx_w, rwkv_lnx_b,
              pool_w, pool_scale, norm_mix, norm_ffn, norm_ple, norm_final,
              moe_w_grp, moe_b_grp, moe_w_exp, moe_b_exp, moe_w_gate, moe_w_up, moe_w_down,
              ple_w_in, ple_gate_down, ple_gate_up):
    W = dict(rwkv_mu=rwkv_mu, rwkv_w_rkv=rwkv_w_rkv, rwkv_w_o=rwkv_w_o, rwkv_w0=rwkv_w0, rwkv_w1=rwkv_w1,
             rwkv_w2=rwkv_w2, rwkv_a0=rwkv_a0, rwkv_a1=rwkv_a1, rwkv_a2=rwkv_a2, rwkv_g1=rwkv_g1,
             rwkv_g2=rwkv_g2, rwkv_k_k=rwkv_k_k, rwkv_k_a=rwkv_k_a, rwkv_r_k=rwkv_r_k,
             rwkv_lnx_w=rwkv_lnx_w, rwkv_lnx_b=rwkv_lnx_b, pool_w=pool_w, pool_scale=pool_scale,
             norm_mix=norm_mix, norm_ffn=norm_ffn, norm_ple=norm_ple, norm_final=norm_final,
             moe_w_grp=moe_w_grp, moe_b_grp=moe_b_grp, moe_w_exp=moe_w_exp, moe_b_exp=moe_b_exp,
             moe_w_gate=moe_w_gate, moe_w_up=moe_w_up, moe_w_down=moe_w_down,
             ple_w_in=ple_w_in, ple_gate_down=ple_gate_down, ple_gate_up=ple_gate_up)
    Bp = x_prompt.shape[0]
    dt = x_prompt.dtype
    zero_shift = jnp.zeros((N_RWKV, Bp, D_MODEL), dt)
    zero_wkv = jnp.zeros((N_RWKV, Bp, N_HEADS, HEAD_SIZE, HEAD_SIZE), dt)
    zero_pool = jnp.zeros((N_POOL, Bp, POOL_HIST, D_MODEL), dt)
    y_prompt, shift_p, wkv_p, pool_p = trunk(x_prompt, p_prompt, zero_shift, zero_wkv, zero_pool, 0, W)
    y_sample, shift_s, wkv_s, pool_s = trunk(x_sample, p_sample, state_rwkv_shift, state_rwkv_wkv, state_pool, PAST_LEN, W)
    return (y_prompt, y_sample, shift_p, wkv_p, pool_p, shift_s, wkv_s, pool_s)
```

```python
import functools

import jax
import jax.numpy as jnp
from jax import lax
from jax.experimental import pallas as pl
from jax.experimental.pallas import tpu as pltpu

F32 = jnp.float32
BF16 = jnp.bfloat16

NORM_EPS = 1e-6
GN_EPS = 64e-5
PAST_LEN = 4096
POOL_WINDOWS = (2, 4, 8, 16)
TOP_K = 2
SEQ_TILE = 32
CHUNK = 64
QUAD = 4
MOE_ROWS = 256
VMEM_LIMIT = 56 * 1024 * 1024


def _cparams(sem):
    return pltpu.CompilerParams(dimension_semantics=sem, vmem_limit_bytes=VMEM_LIMIT)


def _pick(n, cands):
    for c in cands:
        if n % c == 0:
            return c
    return n


def _rms(x, g):
    return x * lax.rsqrt(jnp.mean(x * x, axis=-1, keepdims=True) + NORM_EPS) * g


def _dot(a, b, dims=(((1,), (0,)), ((), ())), prec=None):
    return lax.dot_general(a, b, dims, precision=prec, preferred_element_type=F32)


_NT = (((1,), (1,)), ((), ()))
_TN = (((0,), (0,)), ((), ()))
_HI = lax.Precision.HIGHEST


def _mm_kernel(*refs, has_res, has_scale):
    a_ref, w_ref = refs[0], refs[1]
    k = 2
    acc = _dot(a_ref[...], w_ref[...])
    if has_scale:
        acc = acc * refs[k][...]
        k += 1
    if has_res:
        acc = refs[k][...] + acc
        k += 1
    o_ref = refs[k]
    o_ref[...] = acc.astype(o_ref.dtype)


def _mm(a, w, *, res=None, scale=None, out_dtype=F32, name="mm"):
    M = a.shape[0]
    G, Kg, Ng = w.shape
    tm = _pick(M, (512, 256, 128, 64, 32, 16, 8))
    tn = _pick(Ng, (1024, 512, 256, 128))
    nn = Ng // tn
    in_specs = [
        pl.BlockSpec((tm, Kg), lambda g, n, m: (m, g)),
        pl.BlockSpec((None, Kg, tn), lambda g, n, m: (g, 0, n)),
    ]
    args = [a, w]
    if scale is not None:
        in_specs.append(pl.BlockSpec((1, tn), lambda g, n, m: (0, g * nn + n)))
        args.append(scale)
    if res is not None:
        in_specs.append(pl.BlockSpec((tm, tn), lambda g, n, m: (m, g * nn + n)))
        args.append(res)
    return pl.pallas_call(
        functools.partial(_mm_kernel, has_res=res is not None, has_scale=scale is not None),
        grid=(G, nn, M // tm),
        in_specs=in_specs,
        out_specs=pl.BlockSpec((tm, tn), lambda g, n, m: (m, g * nn + n)),
        out_shape=jax.ShapeDtypeStruct((M, G * Ng), out_dtype),
        compiler_params=_cparams(("arbitrary", "arbitrary", "arbitrary")),
        name=name,
    )(*args)


def _lora_kernel(a_ref, w1_ref, w2_ref, b_ref, o_ref, *, mode):
    t = _dot(a_ref[...], w1_ref[...])
    if mode == "decay":
        t = jnp.tanh(t)
    elif mode == "gate":
        t = jax.nn.sigmoid(t)
    z = _dot(t.astype(BF16), w2_ref[...]) + b_ref[...]
    if mode == "decay":
        z = -jnp.exp(-jax.nn.softplus(-z) - 0.5)
    elif mode == "aaa":
        z = jax.nn.sigmoid(z)
    o_ref[...] = z


def _lora(a, w1, w2, bias, mode):
    M, K = a.shape
    L = w1.shape[1]
    D = w2.shape[1]
    tm = _pick(M, (256, 128, 64, 32, 16, 8))
    return pl.pallas_call(
        functools.partial(_lora_kernel, mode=mode),
        grid=(M // tm,),
        in_specs=[
            pl.BlockSpec((tm, K), lambda m: (m, 0)),
            pl.BlockSpec((K, L), lambda m: (0, 0)),
            pl.BlockSpec((L, D), lambda m: (0, 0)),
            pl.BlockSpec((1, D), lambda m: (0, 0)),
        ],
        out_specs=pl.BlockSpec((tm, D), lambda m: (m, 0)),
        out_shape=jax.ShapeDtypeStruct((M, D), F32),
        compiler_params=_cparams(("arbitrary",)),
        name="lora_" + mode,
    )(a, w1, w2, bias)


class _SeqLayout:
    def __init__(self, bp, tp, bs, ts):
        assert tp % SEQ_TILE == 0 and ts % SEQ_TILE == 0
        self.bp, self.tp, self.bs, self.ts = bp, tp, bs, ts
        self.tiles_p = tp // SEQ_TILE
        self.tiles_s = ts // SEQ_TILE
        self.n_tiles_p = bp * self.tiles_p
        self.n_tiles = self.n_tiles_p + bs * self.tiles_s
        self.n_seq = bp + bs
        self.rows = bp * tp + bs * ts

    def seq_of_tile(self, i):
        return jnp.where(i < self.n_tiles_p, i // self.tiles_p,
                         self.bp + (i - self.n_tiles_p) // self.tiles_s)

    def tile_in_seq(self, i):
        return jnp.where(i < self.n_tiles_p, i % self.tiles_p, (i - self.n_tiles_p) % self.tiles_s)


def _norm_mix_kernel(x_ref, xp_ref, sh_ref, g_ref, mu_ref, *outs, lay):
    i = pl.program_id(0)
    g = g_ref[...]
    h = _rms(x_ref[...], g)
    h_before = _rms(xp_ref[7:8, :], g)
    first = lay.tile_in_seq(i) == 0
    prev_row = jnp.where(first, sh_ref[...], h_before)
    rows = lax.broadcasted_iota(jnp.int32, h.shape, 0)
    h_prev = jnp.where(rows == 0, prev_row, pltpu.roll(h, 1, axis=0))
    xx = h_prev - h
    for n in range(6):
        outs[n][...] = (h + xx * mu_ref[n:n + 1, :]).astype(BF16)
    outs[6][...] = h


def _norm_mix(x, shift_all, g, mu, lay):
    R, D = x.shape
    tpb = SEQ_TILE // 8
    row_spec = pl.BlockSpec((SEQ_TILE, D), lambda i: (i, 0))
    outs = pl.pallas_call(
        functools.partial(_norm_mix_kernel, lay=lay),
        grid=(lay.n_tiles,),
        in_specs=[
            row_spec,
            pl.BlockSpec((8, D), lambda i: (jnp.maximum(i * tpb - 1, 0), 0)),
            pl.BlockSpec((None, 1, D), lambda i: (lay.seq_of_tile(i), 0, 0)),
            pl.BlockSpec((1, D), lambda i: (0, 0)),
            pl.BlockSpec((6, D), lambda i: (0, 0)),
        ],
        out_specs=[row_spec] * 6 + [pl.BlockSpec((None, SEQ_TILE, D), lambda i: (lay.seq_of_tile(i), 0, 0))],
        out_shape=[jax.ShapeDtypeStruct((R, D), BF16)] * 6
        + [jax.ShapeDtypeStruct((lay.n_seq, SEQ_TILE, D), F32)],
        compiler_params=_cparams(("arbitrary",)),
        name="norm_mix",
    )(x, x, shift_all, g, mu)
    return outs[:6], outs[6]


def _scan_kernel(r_ref, k_ref, v_ref, lw_ref, a_ref, g_ref, kk_ref, ka_ref, rk_ref, lw_w_ref, lw_b_ref,
                 s0_ref, o_ref, s_ref, *, t_blk, hs):
    C = CHUNK
    W = QUAD * hs
    c = pl.program_id(2)

    @pl.when(c == 0)
    def _():
        s_ref[...] = s0_ref[...]

    def load(ref):
        x = ref[...]
        if t_blk < C:
            x = jnp.concatenate([x, jnp.zeros((C - t_blk, W), F32)], axis=0)
        return x

    r, k, v, lw, a = load(r_ref), load(k_ref), load(v_ref), load(lw_ref), load(a_ref)

    ri = lax.broadcasted_iota(jnp.int32, (W, W), 0)
    ci = lax.broadcasted_iota(jnp.int32, (W, W), 1)
    head_blk = (ri // hs) == (ci // hs)
    ones_blk = jnp.where(head_blk, 1.0, 0.0).astype(F32)
    ti = lax.broadcasted_iota(jnp.int32, (C, C), 0)
    si = lax.broadcasted_iota(jnp.int32, (C, C), 1)
    tri_incl = jnp.where(ti >= si, 1.0, 0.0).astype(F32)

    def head_sum(x):
        return _dot(x, ones_blk, prec=_HI)

    kk = k * kk_ref[...]
    kk = kk * lax.rsqrt(jnp.maximum(head_sum(kk * kk), 1e-24))
    kf = k * (1.0 + (a - 1.0) * ka_ref[...])
    av = -kk
    bv = kk * a

    cum = _dot(tri_incl, lw, prec=_HI)
    cum_c = cum[C - 1:C, :]
    e_neg = jnp.exp(-cum)
    rt = r * jnp.exp(cum)
    at = av * jnp.exp(cum - lw)
    kt = kf * e_neg
    bt = bv * e_neg
    p_c = jnp.exp(cum_c)
    k_end = kt * p_c
    b_end = bt * p_c

    lane_head = lax.broadcasted_iota(jnp.int32, (C, W), 1) // hs

    def stack(x):
        return jnp.concatenate([jnp.where(lane_head == h, x, 0.0) for h in range(QUAD)], axis=0)

    a_st, r_st, b_st, k_st, v_st = stack(at), stack(rt), stack(bt), stack(kt), stack(v)

    rj = lax.broadcasted_iota(jnp.int32, (QUAD * C, QUAD * C), 0)
    cj = lax.broadcasted_iota(jnp.int32, (QUAD * C, QUAD * C), 1)
    same = (rj // C) == (cj // C)
    strict = same & ((rj % C) > (cj % C))
    incl = same & ((rj % C) >= (cj % C))
    a_ab = jnp.where(strict, _dot(a_st, b_st, _NT, _HI), 0.0)
    a_ak = jnp.where(strict, _dot(a_st, k_st, _NT, _HI), 0.0)
    a_rb = jnp.where(incl, _dot(r_st, b_st, _NT, _HI), 0.0)
    a_rk = jnp.where(incl, _dot(r_st, k_st, _NT, _HI), 0.0)

    t_inv = jnp.where(rj == cj, 1.0, 0.0).astype(F32) + a_ab
    n_pow = a_ab
    p = 1
    while 2 * p < C:
        n_pow = _dot(n_pow, n_pow, prec=_HI)
        t_inv = t_inv + _dot(t_inv, n_pow, prec=_HI)
        p *= 2

    s = s_ref[...]
    w_st = _dot(t_inv, a_st, prec=_HI)
    u_pre = _dot(t_inv, _dot(a_ak, v_st, prec=_HI), prec=_HI)
    u_st = _dot(w_st, s, _NT, _HI) + u_pre
    o_st = _dot(r_st, s, _NT, _HI) + _dot(a_rb, u_st, prec=_HI) + _dot(a_rk, v_st, prec=_HI)

    def fold(x):
        y = x[0:C]
        for h in range(1, QUAD):
            y = y + x[h * C:(h + 1) * C]
        return y

    o = fold(o_st)
    u = fold(u_st)
    upd = _dot(jnp.concatenate([u, v], axis=0), jnp.concatenate([b_end, k_end], axis=0), _TN, _HI)
    s_ref[...] = s * p_c + jnp.where(head_blk, upd, 0.0)

    inv_n = 1.0 / hs
    o_c = o - head_sum(o) * inv_n
    o_n = o_c * lax.rsqrt(head_sum(o_c * o_c) * inv_n + GN_EPS)
    o_n = o_n * lw_w_ref[...] + lw_b_ref[...]
    o_n = o_n + head_sum(r * kf * rk_ref[...]) * v
    out = o_n[:t_blk] * g_ref[...]
    o_ref[...] = out.astype(o_ref.dtype)


def _scan(r, k, v, lw, a, g, prm, s0, *, row0, n_seq, t_len, hs):
    D = r.shape[1]
    W = QUAD * hs
    nq = D // W
    t_blk = min(CHUNK, t_len)
    n_chunks = t_len // t_blk
    blk0 = row0 // t_blk
    row_spec = pl.BlockSpec((t_blk, W), lambda b, q, c: (blk0 + b * n_chunks + c, q))
    prm_spec = pl.BlockSpec((1, W), lambda b, q, c: (0, q))
    st_spec = pl.BlockSpec((None, None, W, W), lambda b, q, c: (b, q, 0, 0))
    return pl.pallas_call(
        functools.partial(_scan_kernel, t_blk=t_blk, hs=hs),
        grid=(n_seq, nq, n_chunks),
        in_specs=[row_spec] * 6 + [prm_spec] * 5 + [st_spec],
        out_specs=[pl.BlockSpec((t_blk, W), lambda b, q, c: (b * n_chunks + c, q)), st_spec],
        out_shape=[jax.ShapeDtypeStruct((n_seq * t_len, D), BF16),
                   jax.ShapeDtypeStruct((n_seq, nq, W, W), F32)],
        compiler_params=_cparams(("arbitrary", "arbitrary", "arbitrary")),
        name="rwkv_scan",
    )(r, k, v, lw, a, g, *prm, s0)


def _pack_state(s, hs):
    n, H = s.shape[0], s.shape[1]
    s = s.reshape(n, H // QUAD, QUAD, hs, hs)
    eye = jnp.eye(QUAD, dtype=s.dtype)
    bd = s[:, :, :, :, None, :] * eye[None, None, :, None, :, None]
    return bd.reshape(n, H // QUAD, QUAD * hs, QUAD * hs)


def _unpack_state(bd, hs):
    n, nq = bd.shape[0], bd.shape[1]
    x = bd.reshape(n, nq, QUAD, hs, QUAD, hs)
    x = jnp.stack([x[:, :, j, :, j, :] for j in range(QUAD)], axis=2)
    return x.reshape(n, nq * QUAD, hs, hs)


def _pool_kernel(x_ref, xp_ref, hist_ref, g_ref, d_ref, hl_ref, *, lay):
    i = pl.program_id(0)
    g = g_ref[...]
    h = _rms(x_ref[...], g)
    tis = lay.tile_in_seq(i)
    prev = jnp.where(tis == 0, hist_ref[...], _rms(xp_ref[...], g))
    ext = jnp.concatenate([prev, h], axis=0)
    D = h.shape[1]
    gw = D // len(POOL_WINDOWS)
    pos0 = jnp.where(i < lay.n_tiles_p, 0, PAST_LEN) + tis * SEQ_TILE
    pos = pos0 + lax.broadcasted_iota(jnp.int32, (SEQ_TILE, 1), 0)
    for gi, w in enumerate(POOL_WINDOWS):
        e = ext[:, gi * gw:(gi + 1) * gw]
        s = e
        step = 1
        while step < w:
            s = s + pltpu.roll(s, step, axis=0)
            step *= 2
        cnt = jnp.minimum(pos + 1, w).astype(F32)
        mean = s[16:] / cnt
        d_ref[:, gi * gw:(gi + 1) * gw] = (mean - h[:, gi * gw:(gi + 1) * gw]).astype(BF16)
    hl_ref[...] = h


def _pool(x, hist16, g, lay):
    R, D = x.shape
    tp16 = SEQ_TILE // 16
    return pl.pallas_call(
        functools.partial(_pool_kernel, lay=lay),
        grid=(lay.n_tiles,),
        in_specs=[
            pl.BlockSpec((SEQ_TILE, D), lambda i: (i, 0)),
            pl.BlockSpec((16, D), lambda i: (jnp.maximum(i * tp16 - 1, 0), 0)),
            pl.BlockSpec((None, 16, D), lambda i: (lay.seq_of_tile(i), 0, 0)),
            pl.BlockSpec((1, D), lambda i: (0, 0)),
        ],
        out_specs=[pl.BlockSpec((SEQ_TILE, D), lambda i: (i, 0)),
                   pl.BlockSpec((None, SEQ_TILE, D), lambda i: (lay.seq_of_tile(i), 0, 0))],
        out_shape=[jax.ShapeDtypeStruct((R, D), BF16),
                   jax.ShapeDtypeStruct((lay.n_seq, SEQ_TILE, D), F32)],
        compiler_params=_cparams(("arbitrary",)),
        name="pool_mix",
    )(x, x, hist16, g)


def _router_kernel(x_ref, g_ref, w_ref, b_ref, hn_ref, wts_ref, eid_ref, *, n_grp, per_grp):
    hn = _rms(x_ref[...], g_ref[...])
    hn_ref[...] = hn
    logits = _dot(hn, w_ref[...], prec=_HI) + b_ref[...]
    lane = lax.broadcasted_iota(jnp.int32, logits.shape, 1).astype(F32)
    neg = -jnp.inf
    big = float(logits.shape[1])
    gl = jnp.where(lane < n_grp, logits, neg)
    gmax = jnp.max(gl, axis=-1, keepdims=True)
    gsum = jnp.sum(jnp.exp(gl - gmax), axis=-1, keepdims=True)
    g_idx = jnp.min(jnp.where(gl == gmax, lane, big), axis=-1, keepdims=True)
    g_gate = 1.0 / gsum
    lo = n_grp + g_idx * per_grp
    el = jnp.where((lane >= lo) & (lane < lo + per_grp), logits, neg)
    e1 = jnp.max(el, axis=-1, keepdims=True)
    i1 = jnp.min(jnp.where(el == e1, lane, big), axis=-1, keepdims=True)
    el2 = jnp.where(lane == i1, neg, el)
    e2 = jnp.max(el2, axis=-1, keepdims=True)
    i2 = jnp.min(jnp.where(el2 == e2, lane, big), axis=-1, keepdims=True)
    esum = jnp.sum(jnp.exp(el - e1), axis=-1, keepdims=True)
    p1 = 1.0 / esum
    p2 = jnp.exp(e2 - e1) / esum
    w1 = g_gate * p1 / (p1 + p2)
    w2 = g_gate * p2 / (p1 + p2)
    wts_ref[...] = jnp.where(lane == 0, w1, jnp.where(lane == 1, w2, 0.0))
    eid_ref[...] = jnp.where(lane == 0, i1 - n_grp, jnp.where(lane == 1, i2 - n_grp, 0.0)).astype(jnp.int32)


def _router(x, g, w_cat, b_cat, n_grp, per_grp):
    R, D = x.shape
    tm = _pick(R, (256, 128, 64, 32, 16, 8))
    L = w_cat.shape[1]
    return pl.pallas_call(
        functools.partial(_router_kernel, n_grp=n_grp, per_grp=per_grp),
        grid=(R // tm,),
        in_specs=[
            pl.BlockSpec((tm, D), lambda m: (m, 0)),
            pl.BlockSpec((1, D), lambda m: (0, 0)),
            pl.BlockSpec((D, L), lambda m: (0, 0)),
            pl.BlockSpec((1, L), lambda m: (0, 0)),
        ],
        out_specs=[pl.BlockSpec((tm, D), lambda m: (m, 0)),
                   pl.BlockSpec((tm, L), lambda m: (m, 0)),
                   pl.BlockSpec((tm, L), lambda m: (m, 0))],
        out_shape=[jax.ShapeDtypeStruct((R, D), F32),
                   jax.ShapeDtypeStruct((R, L), F32),
                   jax.ShapeDtypeStruct((R, L), jnp.int32)],
        compiler_params=_cparams(("arbitrary",)),
        name="moe_router",
    )(x, g, w_cat, b_cat)


_GATHER_ROWS = 128


def _gather_kernel(idx_ref, src_ref, o_ref, sem):
    base = pl.program_id(0) * _GATHER_ROWS

    def row_copy(j):
        return pltpu.make_async_copy(src_ref.at[pl.ds(idx_ref[base + j], 1)],
                                     o_ref.at[pl.ds(base + j, 1)], sem)

    def issue(j, carry):
        row_copy(j).start()
        return carry

    def drain(j, carry):
        row_copy(j).wait()
        return carry

    lax.fori_loop(0, _GATHER_ROWS, issue, 0)
    lax.fori_loop(0, _GATHER_ROWS, drain, 0)


def _gather_rows(src, idx):
    n = idx.shape[0]
    assert n % _GATHER_ROWS == 0
    return pl.pallas_call(
        _gather_kernel,
        grid_spec=pltpu.PrefetchScalarGridSpec(
            num_scalar_prefetch=1,
            grid=(n // _GATHER_ROWS,),
            in_specs=[pl.BlockSpec(memory_space=pl.ANY)],
            out_specs=pl.BlockSpec(memory_space=pl.ANY),
            scratch_shapes=[pltpu.SemaphoreType.DMA(())],
        ),
        out_shape=jax.ShapeDtypeStruct((n, src.shape[1]), src.dtype),
        compiler_params=_cparams(("arbitrary",)),
        name="row_gather",
    )(idx, src)


def _new_expert(blk_e_ref, b):
    prev = blk_e_ref[jnp.maximum(b - 1, 0)]
    return (b == 0) | (blk_e_ref[b] != prev)


def _moe_up_kernel(blk_e_ref, nblk_ref, x_ref, wg_ref, wu_ref, h_ref, wg_bf, wu_bf):
    b = pl.program_id(1)

    @pl.when(b < nblk_ref[0])
    def _():
        @pl.when(_new_expert(blk_e_ref, b))
        def _():
            wg_bf[...] = wg_ref[...].astype(BF16)
            wu_bf[...] = wu_ref[...].astype(BF16)

        x = x_ref[...].astype(BF16)
        gate = _dot(x, wg_bf[...])
        up = _dot(x, wu_bf[...])
        h_ref[...] = (jax.nn.silu(gate) * up).astype(BF16)

    @pl.when(b >= nblk_ref[0])
    def _():
        h_ref[...] = jnp.zeros(h_ref.shape, BF16)


def _moe_up(xg, w_gate, w_up, layer, blk_e, nblk):
    P, D = xg.shape
    De = w_gate.shape[3]
    tj = _pick(De, (512, 256, 128))
    nb = P // MOE_ROWS
    w_spec = pl.BlockSpec((None, None, D, tj), lambda j, b, be, nbk: (layer, be[b], 0, j))
    return pl.pallas_call(
        _moe_up_kernel,
        grid_spec=pltpu.PrefetchScalarGridSpec(
            num_scalar_prefetch=2,
            grid=(De // tj, nb),
            in_specs=[pl.BlockSpec((MOE_ROWS, D), lambda j, b, be, nbk: (b, 0)), w_spec, w_spec],
            out_specs=pl.BlockSpec((MOE_ROWS, tj), lambda j, b, be, nbk: (b, j)),
            scratch_shapes=[pltpu.VMEM((D, tj), BF16), pltpu.VMEM((D, tj), BF16)],
        ),
        out_shape=jax.ShapeDtypeStruct((P, De), BF16),
        compiler_params=_cparams(("arbitrary", "arbitrary")),
        name="moe_up",
    )(blk_e, nblk, xg, w_gate, w_up)


def _moe_down_kernel(blk_e_ref, nblk_ref, h_ref, wd_ref, y_ref, wd_bf):
    b = pl.program_id(1)

    @pl.when(b < nblk_ref[0])
    def _():
        @pl.when(_new_expert(blk_e_ref, b))
        def _():
            wd_bf[...] = wd_ref[...].astype(BF16)

        y_ref[...] = _dot(h_ref[...], wd_bf[...])

    @pl.when(b >= nblk_ref[0])
    def _():
        y_ref[...] = jnp.zeros(y_ref.shape, F32)


def _moe_down(h, w_down, layer, blk_e, nblk):
    P, De = h.shape
    D = w_down.shape[3]
    tn = _pick(D, (2048, 1024, 512, 256, 128))
    nb = P // MOE_ROWS
    return pl.pallas_call(
        _moe_down_kernel,
        grid_spec=pltpu.PrefetchScalarGridSpec(
            num_scalar_prefetch=2,
            grid=(D // tn, nb),
            in_specs=[pl.BlockSpec((MOE_ROWS, De), lambda n, b, be, nbk: (b, 0)),
                      pl.BlockSpec((None, None, De, tn), lambda n, b, be, nbk: (layer, be[b], 0, n))],
            out_specs=pl.BlockSpec((MOE_ROWS, tn), lambda n, b, be, nbk: (b, n)),
            scratch_shapes=[pltpu.VMEM((De, tn), BF16)],
        ),
        out_shape=jax.ShapeDtypeStruct((P, D), F32),
        compiler_params=_cparams(("arbitrary", "arbitrary")),
        name="moe_down",
    )(blk_e, nblk, h, w_down)


def _dispatch_plan(eid, n_experts):
    R = eid.shape[0]
    S = R * TOP_K
    e_flat = eid.reshape(-1)
    onehot = (e_flat[:, None] == jnp.arange(n_experts, dtype=jnp.int32)[None, :]).astype(jnp.int32)
    csum = jnp.cumsum(onehot, axis=0)
    rank = jnp.sum(onehot * csum, axis=1) - 1
    counts = csum[-1]
    padded = (counts + MOE_ROWS - 1) // MOE_ROWS * MOE_ROWS
    pad_end = jnp.cumsum(padded)
    dest = ((pad_end - padded)[e_flat] + rank).astype(jnp.int32)
    nb = -(-S // MOE_ROWS) + n_experts
    buf_tok = jnp.zeros((nb * MOE_ROWS,), jnp.int32).at[dest].set(jnp.arange(S, dtype=jnp.int32) // TOP_K)
    blk_e = jnp.minimum(jnp.searchsorted(pad_end, jnp.arange(nb, dtype=jnp.int32) * MOE_ROWS, side='right'),
                        n_experts - 1).astype(jnp.int32)
    nblk = (pad_end[-1:] // MOE_ROWS).astype(jnp.int32)
    return dest, buf_tok, blk_e, nblk


def _ple_kernel(x_ref, y0_ref, y1_ref, wts_ref, p_ref, g_ref, gd_ref, gu_ref, win_ref, gf_ref, o_ref, *, final):
    wts = wts_ref[...]
    x = x_ref[...] + y0_ref[...] * wts[:, 0:1] + y1_ref[...] * wts[:, 1:2]
    hn = _rms(x, g_ref[...]).astype(BF16)
    t = _dot(hn, gd_ref[...]).astype(BF16)
    gate = jax.nn.sigmoid(_dot(t, gu_ref[...]))
    pe = _dot(p_ref[...], win_ref[...])
    x = x + pe * gate
    if final:
        x = _rms(x, gf_ref[...])
    o_ref[...] = x


def _ple(x, yg, wts, p, g, gd, gu, win, gf, final):
    R, D = x.shape
    Pd = p.shape[1]
    L = wts.shape[1]
    tm = _pick(R, (128, 64, 32, 16, 8))
    nt = R // tm
    row = pl.BlockSpec((tm, D), lambda m: (m, 0))
    return pl.pallas_call(
        functools.partial(_ple_kernel, final=final),
        grid=(nt,),
        in_specs=[
            row,
            pl.BlockSpec((tm, D), lambda m: (m, 0)),
            pl.BlockSpec((tm, D), lambda m: (nt + m, 0)),
            pl.BlockSpec((tm, L), lambda m: (m, 0)),
            pl.BlockSpec((tm, Pd), lambda m: (m, 0)),
            pl.BlockSpec((1, D), lambda m: (0, 0)),
            pl.BlockSpec((D, Pd), lambda m: (0, 0)),
            pl.BlockSpec((Pd, D), lambda m: (0, 0)),
            pl.BlockSpec((Pd, D), lambda m: (0, 0)),
            pl.BlockSpec((1, D), lambda m: (0, 0)),
        ],
        out_specs=row,
        out_shape=jax.ShapeDtypeStruct((R, D), F32),
        compiler_params=_cparams(("arbitrary",)),
        name="moe_combine_ple",
    )(x, yg, yg, wts, p, g, gd, gu, win, gf)


def _moe_ple(x, p, layer, prm, final):
    R, D = x.shape
    n_grp = prm["moe_w_grp"].shape[2]
    n_exp = prm["moe_w_exp"].shape[2]
    L = 128
    w_cat = jnp.concatenate([prm["moe_w_grp"][layer], prm["moe_w_exp"][layer],
                             jnp.zeros((D, L - n_grp - n_exp), F32)], axis=1)
    b_cat = jnp.concatenate([prm["moe_b_grp"][layer], prm["moe_b_exp"][layer],
                             jnp.zeros((L - n_grp - n_exp,), F32)])[None, :]
    hn, wts, eid = _router(x, prm["norm_ffn"][layer][None, :], w_cat, b_cat, n_grp, n_exp // n_grp)
    dest, buf_tok, blk_e, nblk = _dispatch_plan(eid[:, :TOP_K], n_exp)
    xg = _gather_rows(hn, buf_tok)
    hmid = _moe_up(xg, prm["moe_w_gate"], prm["moe_w_up"], layer, blk_e, nblk)
    y = _moe_down(hmid, prm["moe_w_down"], layer, blk_e, nblk)
    yg = _gather_rows(y, dest.reshape(R, TOP_K).T.reshape(-1))
    return _ple(x, yg, wts, p, prm["norm_ple"][layer][None, :],
                prm["ple_gate_down"][layer].astype(BF16), prm["ple_gate_up"][layer].astype(BF16),
                prm["ple_w_in"][layer].astype(BF16), prm["norm_final"][None, :], final)


def kernel(x_prompt, x_sample, state_rwkv_shift, state_rwkv_wkv, state_pool, p_prompt, p_sample,
           rwkv_mu, rwkv_w_rkv, rwkv_w_o, rwkv_w0, rwkv_w1, rwkv_w2, rwkv_a0, rwkv_a1, rwkv_a2,
           rwkv_g1, rwkv_g2, rwkv_k_k, rwkv_k_a, rwkv_r_k, rwkv_lnx_w, rwkv_lnx_b,
           pool_w, pool_scale, norm_mix, norm_ffn, norm_ple, norm_final,
           moe_w_grp, moe_b_grp, moe_w_exp, moe_b_exp, moe_w_gate, moe_w_up, moe_w_down,
           ple_w_in, ple_gate_down, ple_gate_up):
    Bp, Tp, D = x_prompt.shape
    Bs, Ts, _ = x_sample.shape
    depth = norm_mix.shape[0]
    H, hs = rwkv_r_k.shape[1], rwkv_r_k.shape[2]
    assert D % (QUAD * hs) == 0 and Ts <= CHUNK and Tp % CHUNK == 0
    lay = _SeqLayout(Bp, Tp, Bs, Ts)
    Rp = Bp * Tp
    prm = dict(moe_w_grp=moe_w_grp, moe_b_grp=moe_b_grp, moe_w_exp=moe_w_exp, moe_b_exp=moe_b_exp,
               moe_w_gate=moe_w_gate, moe_w_up=moe_w_up, moe_w_down=moe_w_down, norm_ffn=norm_ffn,
               norm_ple=norm_ple, norm_final=norm_final, ple_w_in=ple_w_in, ple_gate_down=ple_gate_down,
               ple_gate_up=ple_gate_up)

    x = jnp.concatenate([x_prompt.reshape(Rp, D), x_sample.reshape(Bs * Ts, D)], axis=0)
    Pd = p_prompt.shape[-1]
    p_all = jnp.concatenate([p_prompt.reshape(depth, Rp, Pd), p_sample.reshape(depth, Bs * Ts, Pd)],
                            axis=1).astype(BF16)

    shift_p, wkv_p, pool_p, shift_s, wkv_s, pool_s = [], [], [], [], [], []
    for i in range(depth):
        j = i // 2
        g_mix = norm_mix[i][None, :]
        if i % 2 == 0:
            shift_all = jnp.concatenate([jnp.zeros((Bp, D), F32), state_rwkv_shift[j]], axis=0)[:, None, :]
            mixed, h_last = _norm_mix(x, shift_all, g_mix, rwkv_mu[j], lay)
            xr, xw, xk, xv, xa, xg = mixed
            w_rkv = rwkv_w_rkv[j].astype(BF16)
            r = _mm(xr, w_rkv[0:1], name="proj_r")
            k = _mm(xk, w_rkv[1:2], name="proj_k")
            v = _mm(xv, w_rkv[2:3], name="proj_v")
            lw = _lora(xw, rwkv_w1[j].astype(BF16), rwkv_w2[j].astype(BF16), rwkv_w0[j][None, :], "decay")
            a = _lora(xa, rwkv_a1[j].astype(BF16), rwkv_a2[j].astype(BF16), rwkv_a0[j][None, :], "aaa")
            gl = rwkv_g1.shape[2]
            glp = -(-gl // 128) * 128
            g1 = jnp.pad(rwkv_g1[j], ((0, 0), (0, glp - gl))).astype(BF16)
            g2 = jnp.pad(rwkv_g2[j], ((0, glp - gl), (0, 0))).astype(BF16)
            g = _lora(xg, g1, g2, jnp.zeros((1, D), F32), "gate")
            sprm = (rwkv_k_k[j][None, :], rwkv_k_a[j][None, :], rwkv_r_k[j].reshape(1, D),
                    rwkv_lnx_w[j][None, :], rwkv_lnx_b[j][None, :])
            s0_p = jnp.zeros((Bp, D // (QUAD * hs), QUAD * hs, QUAD * hs), F32)
            o_p, sp = _scan(r, k, v, lw, a, g, sprm, s0_p, row0=0, n_seq=Bp, t_len=Tp, hs=hs)
            o_s, ss = _scan(r, k, v, lw, a, g, sprm, _pack_state(state_rwkv_wkv[j], hs),
                            row0=Rp, n_seq=Bs, t_len=Ts, hs=hs)
            o = jnp.concatenate([o_p, o_s], axis=0)
            x = _mm(o, rwkv_w_o[j].astype(BF16)[None], res=x, name="proj_o")
            shift_p.append(h_last[:Bp, -1])
            shift_s.append(h_last[Bp:, -1])
            wkv_p.append(_unpack_state(sp, hs))
            wkv_s.append(_unpack_state(ss, hs))
        else:
            hist = jnp.concatenate([jnp.zeros((Bp, 16, D), F32),
                                    jnp.pad(state_pool[j], ((0, 0), (1, 0), (0, 0)))], axis=0)
            d, h_last = _pool(x, hist, g_mix, lay)
            x = _mm(d, pool_w[j].astype(BF16), scale=pool_scale[j][None, :], res=x, name="pool_proj")
            nh = state_pool.shape[2]
            pool_p.append(h_last[:Bp, SEQ_TILE - nh:])
            pool_s.append(h_last[Bp:, SEQ_TILE - nh:])
        x = _moe_ple(x, p_all[i], i, prm, final=(i == depth - 1))

    y_prompt = x[:Rp].reshape(Bp, Tp, D)
    y_sample = x[Rp:].reshape(Bs, Ts, D)
    return (y_prompt, y_sample, jnp.stack(shift_p), jnp.stack(wkv_p), jnp.stack(pool_p),
            jnp.stack(shift_s), jnp.stack(wkv_s), jnp.stack(pool_s))
```

```python
import functools

import jax
import jax.numpy as jnp
from jax import lax
from jax.experimental import pallas as pl
from jax.experimental.pallas import tpu as pltpu

F32 = jnp.float32
BF16 = jnp.bfloat16

NORM_EPS = 1e-6
GN_EPS = 64e-5
PAST_LEN = 4096
POOL_WINDOWS = (2, 4, 8, 16)
TOP_K = 2
LANES = 128
SEQ_TILE = 32
CHUNK = 64
QUAD = 4
SCAN_GROUP = 4
MOE_ROWS = 256
GATHER_PITCH = 36
VMEM_LIMIT = 56 * 1024 * 1024


def _cparams(sem):
    return pltpu.CompilerParams(dimension_semantics=sem, vmem_limit_bytes=VMEM_LIMIT)


def _pick(n, cands):
    for c in cands:
        if n % c == 0:
            return c
    return n


def _rms(x, g):
    return x * lax.rsqrt(jnp.mean(x * x, axis=-1, keepdims=True) + NORM_EPS) * g


_NN = (((1,), (0,)), ((), ()))
_NT = (((1,), (1,)), ((), ()))
_TN = (((0,), (0,)), ((), ()))


def _dot(a, b, dims=_NN, prec=None):
    return lax.dot_general(a, b, dims, precision=prec, preferred_element_type=F32)


def _mm_kernel(*refs, has_res, has_scale, a_slabs, o_slabs):
    a_ref, w_ref = refs[0], refs[1]
    k = 2
    if a_slabs:
        a = jnp.concatenate([a_ref[q] for q in range(a_slabs)], axis=1)
    else:
        a = a_ref[...]
    acc = _dot(a, w_ref[...])
    if has_scale:
        acc = acc * refs[k][...]
        k += 1
    if has_res:
        acc = refs[k][...] + acc
        k += 1
    o_ref = refs[k]
    if o_slabs:
        sw = o_ref.shape[2]
        for q in range(o_slabs):
            o_ref[q] = acc[:, q * sw:(q + 1) * sw].astype(o_ref.dtype)
    else:
        o_ref[...] = acc.astype(o_ref.dtype)


def _mm(a, w, *, res=None, scale=None, out_dtype=F32, slab_in=False, slab_out=0, name="mm"):
    M = a.shape[1] if slab_in else a.shape[0]
    G, Kg, Ng = w.shape
    tm = _pick(M, (512, 256, 128, 64, 32, 16, 8))
    tn = _pick(Ng, (1024, 512, 256, 128))
    nn = Ng // tn
    if slab_in:
        assert G == 1
        a_spec = pl.BlockSpec((a.shape[0], tm, a.shape[2]), lambda g, n, m: (0, m, 0))
    else:
        a_spec = pl.BlockSpec((tm, Kg), lambda g, n, m: (m, g))
    in_specs = [a_spec, pl.BlockSpec((None, Kg, tn), lambda g, n, m: (g, 0, n))]
    args = [a, w]
    if scale is not None:
        in_specs.append(pl.BlockSpec((1, tn), lambda g, n, m: (0, g * nn + n)))
        args.append(scale)
    if res is not None:
        in_specs.append(pl.BlockSpec((tm, tn), lambda g, n, m: (m, g * nn + n)))
        args.append(res)
    if slab_out:
        assert tn % slab_out == 0
        per = tn // slab_out
        out_spec = pl.BlockSpec((per, tm, slab_out), lambda g, n, m: (g * nn + n, m, 0))
        out_shape = jax.ShapeDtypeStruct((G * Ng // slab_out, M, slab_out), out_dtype)
    else:
        per = 0
        out_spec = pl.BlockSpec((tm, tn), lambda g, n, m: (m, g * nn + n))
        out_shape = jax.ShapeDtypeStruct((M, G * Ng), out_dtype)
    return pl.pallas_call(
        functools.partial(_mm_kernel, has_res=res is not None, has_scale=scale is not None,
                          a_slabs=a.shape[0] if slab_in else 0, o_slabs=per),
        grid=(G, nn, M // tm),
        in_specs=in_specs,
        out_specs=out_spec,
        out_shape=out_shape,
        compiler_params=_cparams(("arbitrary", "arbitrary", "arbitrary")),
        name=name,
    )(*args)


def _lora_kernel(a_ref, w1_ref, w2_ref, b_ref, o_ref, *, mode):
    t = _dot(a_ref[...], w1_ref[...])
    if mode == "decay":
        t = jnp.tanh(t)
    elif mode == "gate":
        t = jax.nn.sigmoid(t)
    z = _dot(t.astype(BF16), w2_ref[...]) + b_ref[...]
    if mode == "decay":
        z = -jnp.exp(-jax.nn.softplus(-z) - 0.5)
    elif mode == "aaa":
        z = jax.nn.sigmoid(z)
    sw = o_ref.shape[2]
    for q in range(o_ref.shape[0]):
        o_ref[q] = z[:, q * sw:(q + 1) * sw]


def _lora(a, w1, w2, bias, mode, sw):
    M, K = a.shape
    L = w1.shape[1]
    D = w2.shape[1]
    tm = _pick(M, (256, 128, 64, 32, 16, 8))
    return pl.pallas_call(
        functools.partial(_lora_kernel, mode=mode),
        grid=(M // tm,),
        in_specs=[
            pl.BlockSpec((tm, K), lambda m: (m, 0)),
            pl.BlockSpec((K, L), lambda m: (0, 0)),
            pl.BlockSpec((L, D), lambda m: (0, 0)),
            pl.BlockSpec((1, D), lambda m: (0, 0)),
        ],
        out_specs=pl.BlockSpec((D // sw, tm, sw), lambda m: (0, m, 0)),
        out_shape=jax.ShapeDtypeStruct((D // sw, M, sw), F32),
        compiler_params=_cparams(("arbitrary",)),
        name="lora_" + mode,
    )(a, w1, w2, bias)


class _SeqLayout:
    def __init__(self, bp, tp, bs, ts):
        assert tp % SEQ_TILE == 0 and ts % SEQ_TILE == 0
        self.bp, self.tp, self.bs, self.ts = bp, tp, bs, ts
        self.tiles_p = tp // SEQ_TILE
        self.tiles_s = ts // SEQ_TILE
        self.n_tiles_p = bp * self.tiles_p
        self.n_tiles = self.n_tiles_p + bs * self.tiles_s
        self.n_seq = bp + bs
        self.rows = bp * tp + bs * ts

    def seq_of_tile(self, i):
        return jnp.where(i < self.n_tiles_p, i // self.tiles_p,
                         self.bp + (i - self.n_tiles_p) // self.tiles_s)

    def tile_in_seq(self, i):
        return jnp.where(i < self.n_tiles_p, i % self.tiles_p, (i - self.n_tiles_p) % self.tiles_s)


def _norm_mix_kernel(x_ref, xp_ref, sh_ref, g_ref, mu_ref, *outs, lay):
    i = pl.program_id(0)
    g = g_ref[...]
    h = _rms(x_ref[...], g)
    h_before = _rms(xp_ref[7:8, :], g)
    first = lay.tile_in_seq(i) == 0
    prev_row = jnp.where(first, sh_ref[...], h_before)
    rows = lax.broadcasted_iota(jnp.int32, h.shape, 0)
    h_prev = jnp.where(rows == 0, prev_row, pltpu.roll(h, 1, axis=0))
    xx = h_prev - h
    for n in range(6):
        outs[n][...] = (h + xx * mu_ref[n:n + 1, :]).astype(BF16)
    outs[6][...] = h


def _norm_mix(x, shift_all, g, mu, lay):
    R, D = x.shape
    tpb = SEQ_TILE // 8
    row_spec = pl.BlockSpec((SEQ_TILE, D), lambda i: (i, 0))
    outs = pl.pallas_call(
        functools.partial(_norm_mix_kernel, lay=lay),
        grid=(lay.n_tiles,),
        in_specs=[
            row_spec,
            pl.BlockSpec((8, D), lambda i: (jnp.maximum(i * tpb - 1, 0), 0)),
            pl.BlockSpec((None, 1, D), lambda i: (lay.seq_of_tile(i), 0, 0)),
            pl.BlockSpec((1, D), lambda i: (0, 0)),
            pl.BlockSpec((6, D), lambda i: (0, 0)),
        ],
        out_specs=[row_spec] * 6 + [pl.BlockSpec((None, SEQ_TILE, D), lambda i: (lay.seq_of_tile(i), 0, 0))],
        out_shape=[jax.ShapeDtypeStruct((R, D), BF16)] * 6
        + [jax.ShapeDtypeStruct((lay.n_seq, SEQ_TILE, D), F32)],
        compiler_params=_cparams(("arbitrary",)),
        name="norm_mix",
    )(x, x, shift_all, g, mu)
    return outs[:6], outs[6]


def _scan_kernel(r_ref, k_ref, v_ref, lw_ref, a_ref, g_ref, kk_ref, ka_ref, rk_ref, lnw_ref, lnb_ref,
                 s0_ref, o_ref, s_ref, *, t_blk, hs, group, n_chunks):
    C = CHUNK
    nq, _, W = r_ref.shape
    c = pl.program_id(1)

    @pl.when(c == 0)
    def _():
        s_ref[...] = s0_ref[...]

    ri = lax.broadcasted_iota(jnp.int32, (W, W), 0)
    ci = lax.broadcasted_iota(jnp.int32, (W, W), 1)
    head_blk = (ri // hs) == (ci // hs)
    ones_blk = jnp.where(head_blk, 1.0, 0.0).astype(F32)
    ti = lax.broadcasted_iota(jnp.int32, (C, C), 0)
    si = lax.broadcasted_iota(jnp.int32, (C, C), 1)
    tri_incl = jnp.where(ti >= si, 1.0, 0.0).astype(F32)
    lane_head = lax.broadcasted_iota(jnp.int32, (C, W), 1) // hs
    rj = lax.broadcasted_iota(jnp.int32, (QUAD * C, QUAD * C), 0)
    cj = lax.broadcasted_iota(jnp.int32, (QUAD * C, QUAD * C), 1)
    t_in = rj % C
    s_in = cj % C
    strict = jnp.where(t_in > s_in, 1.0, 0.0).astype(F32)
    incl = jnp.where(t_in >= s_in, 1.0, 0.0).astype(F32)
    eye = jnp.where(rj == cj, 1.0, 0.0).astype(F32)

    ones_bf = ones_blk.astype(BF16)
    head_masks = [jnp.where(lane_head == h, 1.0, 0.0).astype(BF16) for h in range(QUAD)]

    def fold(x):
        y = x[0:C]
        for h in range(1, QUAD):
            y = y + x[h * C:(h + 1) * C]
        return y

    def group_body(it, carry, *, passes, group):
        qs = [it * group + j for j in range(group)]
        each = lambda f, *xs: [f(*t) for t in zip(*xs)]

        def parts(x):
            hi = x.astype(BF16)
            if passes == 1:
                return (hi,)
            return (hi, (x - hi.astype(F32)).astype(BF16))

        def bdot(x, y, dims=_NN):
            out = _dot(x[0], y[0], dims)
            if passes > 1:
                out = out + (_dot(x[0], y[1], dims) + _dot(x[1], y[0], dims))
            return out

        def head_sum(x):
            out = _dot(parts(x)[0], ones_bf)
            if passes > 1:
                out = out + _dot(parts(x)[1], ones_bf)
            return out

        def stack(x):
            return tuple(jnp.concatenate([xb * m for m in head_masks], axis=0) for xb in parts(x))

        def load(ref):
            def one(q):
                x = ref[q]
                if t_blk < C:
                    x = jnp.concatenate([x, jnp.zeros((C - t_blk, W), F32)], axis=0)
                return x
            return [one(q) for q in qs]

        r, k, v, lw, a = load(r_ref), load(k_ref), load(v_ref), load(lw_ref), load(a_ref)
        kk = each(lambda k_, q: k_ * kk_ref[q], k, qs)
        kk = each(lambda x: x * lax.rsqrt(jnp.maximum(head_sum(x * x), 1e-24)), kk)
        kf = each(lambda k_, a_, q: k_ * (1.0 + (a_ - 1.0) * ka_ref[q]), k, a, qs)
        bv = each(lambda x, a_: x * a_, kk, a)

        cum = each(lambda x: _dot(tri_incl, x, _NN, lax.Precision.HIGHEST), lw)
        e_neg = each(lambda x: jnp.exp(-x), cum)
        rt = each(lambda r_, x: r_ * jnp.exp(x), r, cum)
        at = each(lambda x, c_, l_: -x * jnp.exp(c_ - l_), kk, cum, lw)
        kt = each(lambda x, e: x * e, kf, e_neg)
        bt = each(lambda x, e: x * e, bv, e_neg)
        p_c = each(lambda x: jnp.exp(x[C - 1:C, :]), cum)

        a_st, r_st, b_st, k_st, v_st = (each(stack, x) for x in (at, rt, bt, kt, v))
        a_ab = each(lambda x, y: bdot(x, y, _NT) * strict, a_st, b_st)
        a_ak = each(lambda x, y: parts(bdot(x, y, _NT) * strict), a_st, k_st)
        a_rb = each(lambda x, y: parts(bdot(x, y, _NT) * incl), r_st, b_st)
        a_rk = each(lambda x, y: parts(bdot(x, y, _NT) * incl), r_st, k_st)

        t_inv = each(lambda x: eye + x, a_ab)
        n_pow = each(parts, a_ab)
        p = 1
        while 2 * p < C:
            n_pow = each(lambda x: parts(bdot(x, x)), n_pow)
            t_inv = each(lambda t, n: t + bdot(parts(t), n), t_inv, n_pow)
            p *= 2
        t_bf = each(parts, t_inv)

        s = [s_ref[q] for q in qs]
        s_bf = each(parts, s)
        u_in = each(lambda a_, s_, ak, v_: parts(bdot(a_, s_, _NT) + bdot(ak, v_)), a_st, s_bf, a_ak, v_st)
        u_st = each(bdot, t_bf, u_in)
        o_st = each(lambda r_, s_, rb, u_, rk, v_: bdot(r_, s_, _NT) + bdot(rb, parts(u_)) + bdot(rk, v_),
                    r_st, s_bf, a_rb, u_st, a_rk, v_st)
        o = each(fold, o_st)
        u = each(fold, u_st)
        upd = each(lambda u_, v_, b_, k_, pc: bdot(parts(jnp.concatenate([u_, v_], axis=0)),
                                                   parts(jnp.concatenate([b_ * pc, k_ * pc], axis=0)), _TN),
                   u, v, bt, kt, p_c)
        for q, s_, pc, up in zip(qs, s, p_c, upd):
            s_ref[q] = s_ * pc + jnp.where(head_blk, up, 0.0)

        inv_n = 1.0 / hs
        o_c = each(lambda x: x - head_sum(x) * inv_n, o)
        o_n = each(lambda x: x * lax.rsqrt(head_sum(x * x) * inv_n + GN_EPS), o_c)
        bonus = each(lambda r_, k_, q: head_sum(r_ * k_ * rk_ref[q]), r, kf, qs)
        for q, x, bo, v_ in zip(qs, o_n, bonus, v):
            y = x * lnw_ref[q] + lnb_ref[q] + bo * v_
            o_ref[q] = (y[:t_blk] * g_ref[q]).astype(o_ref.dtype)
        return carry

    def run(passes, grp):
        lax.fori_loop(0, nq // grp, functools.partial(group_body, passes=passes, group=grp), 0)

    last = pl.num_programs(1) - 1
    if n_chunks > 1:
        @pl.when(c < last)
        def _():
            run(1, group)

    @pl.when(c == last)
    def _():
        run(3, _pick(nq, (2, 1)))


def _scan(r, k, v, lw, a, g, prm, s0, *, row0, n_seq, t_len, hs):
    nq, _, W = r.shape
    t_blk = min(CHUNK, t_len)
    n_chunks = t_len // t_blk
    blk0 = row0 // t_blk
    row_spec = pl.BlockSpec((nq, t_blk, W), lambda b, c: (0, blk0 + b * n_chunks + c, 0))
    prm_spec = pl.BlockSpec((nq, 1, W), lambda b, c: (0, 0, 0))
    st_spec = pl.BlockSpec((None, nq, W, W), lambda b, c: (b, 0, 0, 0))
    return pl.pallas_call(
        functools.partial(_scan_kernel, t_blk=t_blk, hs=hs, group=_pick(nq, (SCAN_GROUP, 2, 1)),
                          n_chunks=n_chunks),
        grid=(n_seq, n_chunks),
        in_specs=[row_spec] * 6 + [prm_spec] * 5 + [st_spec],
        out_specs=[pl.BlockSpec((nq, t_blk, W), lambda b, c: (0, b * n_chunks + c, 0)), st_spec],
        out_shape=[jax.ShapeDtypeStruct((nq, n_seq * t_len, W), BF16),
                   jax.ShapeDtypeStruct((n_seq, nq, W, W), F32)],
        compiler_params=_cparams(("arbitrary", "arbitrary")),
        name="rwkv_scan",
    )(r, k, v, lw, a, g, *prm, s0)


def _pack_state(s, hs):
    n, H = s.shape[0], s.shape[1]
    s = s.reshape(n, H // QUAD, QUAD, hs, hs)
    eye = jnp.eye(QUAD, dtype=s.dtype)
    bd = s[:, :, :, :, None, :] * eye[None, None, :, None, :, None]
    return bd.reshape(n, H // QUAD, QUAD * hs, QUAD * hs)


def _unpack_state(bd, hs):
    n, nq = bd.shape[0], bd.shape[1]
    x = bd.reshape(n, nq, QUAD, hs, QUAD, hs)
    x = jnp.stack([x[:, :, j, :, j, :] for j in range(QUAD)], axis=2)
    return x.reshape(n, nq * QUAD, hs, hs)


def _pool_kernel(x_ref, xp_ref, hist_ref, g_ref, d_ref, hl_ref, *, lay):
    i = pl.program_id(0)
    g = g_ref[...]
    h = _rms(x_ref[...], g)
    tis = lay.tile_in_seq(i)
    prev = jnp.where(tis == 0, hist_ref[...], _rms(xp_ref[...], g))
    ext = jnp.concatenate([prev, h], axis=0)
    D = h.shape[1]
    gw = D // len(POOL_WINDOWS)
    pos0 = jnp.where(i < lay.n_tiles_p, 0, PAST_LEN) + tis * SEQ_TILE
    pos = pos0 + lax.broadcasted_iota(jnp.int32, (SEQ_TILE, 1), 0)
    for gi, w in enumerate(POOL_WINDOWS):
        e = ext[:, gi * gw:(gi + 1) * gw]
        s = e
        step = 1
        while step < w:
            s = s + pltpu.roll(s, step, axis=0)
            step *= 2
        cnt = jnp.minimum(pos + 1, w).astype(F32)
        mean = s[16:] / cnt
        d_ref[:, gi * gw:(gi + 1) * gw] = (mean - h[:, gi * gw:(gi + 1) * gw]).astype(BF16)
    hl_ref[...] = h


def _pool(x, hist16, g, lay):
    R, D = x.shape
    tp16 = SEQ_TILE // 16
    return pl.pallas_call(
        functools.partial(_pool_kernel, lay=lay),
        grid=(lay.n_tiles,),
        in_specs=[
            pl.BlockSpec((SEQ_TILE, D), lambda i: (i, 0)),
            pl.BlockSpec((16, D), lambda i: (jnp.maximum(i * tp16 - 1, 0), 0)),
            pl.BlockSpec((None, 16, D), lambda i: (lay.seq_of_tile(i), 0, 0)),
            pl.BlockSpec((1, D), lambda i: (0, 0)),
        ],
        out_specs=[pl.BlockSpec((SEQ_TILE, D), lambda i: (i, 0)),
                   pl.BlockSpec((None, SEQ_TILE, D), lambda i: (lay.seq_of_tile(i), 0, 0))],
        out_shape=[jax.ShapeDtypeStruct((R, D), BF16),
                   jax.ShapeDtypeStruct((lay.n_seq, SEQ_TILE, D), F32)],
        compiler_params=_cparams(("arbitrary",)),
        name="pool_mix",
    )(x, x, hist16, g)


def _router_kernel(x_ref, g_ref, w_ref, b_ref, hn_ref, wts_ref, eid_ref, *, n_grp, per_grp):
    hn = _rms(x_ref[...], g_ref[...])
    hn_ref[...] = hn
    logits = _dot(hn.astype(BF16), w_ref[...]) + b_ref[...]
    lane = lax.broadcasted_iota(jnp.int32, logits.shape, 1).astype(F32)
    neg = -jnp.inf
    big = float(logits.shape[1])
    gl = jnp.where(lane < n_grp, logits, neg)
    gmax = jnp.max(gl, axis=-1, keepdims=True)
    gsum = jnp.sum(jnp.exp(gl - gmax), axis=-1, keepdims=True)
    g_idx = jnp.min(jnp.where(gl == gmax, lane, big), axis=-1, keepdims=True)
    g_gate = 1.0 / gsum
    lo = n_grp + g_idx * per_grp
    el = jnp.where((lane >= lo) & (lane < lo + per_grp), logits, neg)
    e1 = jnp.max(el, axis=-1, keepdims=True)
    i1 = jnp.min(jnp.where(el == e1, lane, big), axis=-1, keepdims=True)
    el2 = jnp.where(lane == i1, neg, el)
    e2 = jnp.max(el2, axis=-1, keepdims=True)
    i2 = jnp.min(jnp.where(el2 == e2, lane, big), axis=-1, keepdims=True)
    esum = jnp.sum(jnp.exp(el - e1), axis=-1, keepdims=True)
    p1 = 1.0 / esum
    p2 = jnp.exp(e2 - e1) / esum
    w1 = g_gate * p1 / (p1 + p2)
    w2 = g_gate * p2 / (p1 + p2)
    wts_ref[...] = jnp.where(lane == 0, w1, jnp.where(lane == 1, w2, 0.0))
    eid_ref[...] = jnp.where(lane == 0, i1 - n_grp, jnp.where(lane == 1, i2 - n_grp, 0.0)).astype(jnp.int32)


def _router(x, g, w_cat, b_cat, n_grp, per_grp):
    R, D = x.shape
    tm = _pick(R, (256, 128, 64, 32, 16, 8))
    L = w_cat.shape[1]
    return pl.pallas_call(
        functools.partial(_router_kernel, n_grp=n_grp, per_grp=per_grp),
        grid=(R // tm,),
        in_specs=[
            pl.BlockSpec((tm, D), lambda m: (m, 0)),
            pl.BlockSpec((1, D), lambda m: (0, 0)),
            pl.BlockSpec((D, L), lambda m: (0, 0)),
            pl.BlockSpec((1, L), lambda m: (0, 0)),
        ],
        out_specs=[pl.BlockSpec((tm, D), lambda m: (m, 0)),
                   pl.BlockSpec((tm, L), lambda m: (m, 0)),
                   pl.BlockSpec((tm, L), lambda m: (m, 0))],
        out_shape=[jax.ShapeDtypeStruct((R, D), F32),
                   jax.ShapeDtypeStruct((R, L), F32),
                   jax.ShapeDtypeStruct((R, L), jnp.int32)],
        compiler_params=_cparams(("arbitrary",)),
        name="moe_router",
    )(x, g, w_cat, b_cat)


def _gather_kernel(idx_ref, nblk_ref, src_ref, o_ref, buf, sem):
    G = o_ref.shape[0]
    nc = src_ref.shape[1]
    b = pl.program_id(0)
    base = b * G

    def row_copy(j):
        return pltpu.make_async_copy(src_ref.at[idx_ref[base + j]],
                                     buf.at[pl.ds(j * GATHER_PITCH, nc)], sem)

    @pl.when(b < nblk_ref[0])
    def _():
        def issue(j, carry):
            row_copy(j).start()
            return carry

        def drain(j, carry):
            row_copy(j).wait()
            return carry

        lax.fori_loop(0, G, issue, 0)
        lax.fori_loop(0, G, drain, 0)
        for cc in range(nc):
            o_ref[:, cc * LANES:(cc + 1) * LANES] = buf[pl.ds(cc, G, stride=GATHER_PITCH), :].astype(o_ref.dtype)

    @pl.when(b >= nblk_ref[0])
    def _():
        o_ref[...] = jnp.zeros(o_ref.shape, o_ref.dtype)


def _gather_rows(src3, idx, nblk, out_dtype):
    n = idx.shape[0]
    _, nc, lanes = src3.shape
    G = MOE_ROWS
    assert n % G == 0 and lanes == LANES
    return pl.pallas_call(
        _gather_kernel,
        grid_spec=pltpu.PrefetchScalarGridSpec(
            num_scalar_prefetch=2,
            grid=(n // G,),
            in_specs=[pl.BlockSpec(memory_space=pl.ANY)],
            out_specs=pl.BlockSpec((G, nc * LANES), lambda b, idx_r, nb_r: (b, 0)),
            scratch_shapes=[pltpu.VMEM((G * GATHER_PITCH, LANES), src3.dtype),
                            pltpu.SemaphoreType.DMA(())],
        ),
        out_shape=jax.ShapeDtypeStruct((n, nc * LANES), out_dtype),
        compiler_params=_cparams(("arbitrary",)),
        name="row_gather",
    )(idx, nblk, src3)


def _new_expert(blk_e_ref, b):
    prev = blk_e_ref[jnp.maximum(b - 1, 0)]
    return (b == 0) | (blk_e_ref[b] != prev)


def _moe_up_kernel(blk_e_ref, nblk_ref, x_ref, wg_ref, wu_ref, h_ref, wg_bf, wu_bf):
    b = pl.program_id(1)

    @pl.when(b < nblk_ref[0])
    def _():
        @pl.when(_new_expert(blk_e_ref, b))
        def _():
            wg_bf[...] = wg_ref[...].astype(BF16)
            wu_bf[...] = wu_ref[...].astype(BF16)

        x = x_ref[...]
        gate = _dot(x, wg_bf[...])
        up = _dot(x, wu_bf[...])
        h_ref[...] = (jax.nn.silu(gate) * up).astype(BF16)

    @pl.when(b >= nblk_ref[0])
    def _():
        h_ref[...] = jnp.zeros(h_ref.shape, BF16)


def _moe_up(xg, w_gate, w_up, layer, blk_e, nblk):
    P, D = xg.shape
    De = w_gate.shape[3]
    tj = _pick(De, (512, 256, 128))
    nb = P // MOE_ROWS
    w_spec = pl.BlockSpec((None, None, D, tj), lambda j, b, be, nbk: (layer, be[b], 0, j))
    return pl.pallas_call(
        _moe_up_kernel,
        grid_spec=pltpu.PrefetchScalarGridSpec(
            num_scalar_prefetch=2,
            grid=(De // tj, nb),
            in_specs=[pl.BlockSpec((MOE_ROWS, D), lambda j, b, be, nbk: (b, 0)), w_spec, w_spec],
            out_specs=pl.BlockSpec((MOE_ROWS, tj), lambda j, b, be, nbk: (b, j)),
            scratch_shapes=[pltpu.VMEM((D, tj), BF16), pltpu.VMEM((D, tj), BF16)],
        ),
        out_shape=jax.ShapeDtypeStruct((P, De), BF16),
        compiler_params=_cparams(("arbitrary", "arbitrary")),
        name="moe_up",
    )(blk_e, nblk, xg, w_gate, w_up)


def _moe_down_kernel(blk_e_ref, nblk_ref, h_ref, wd_ref, y_ref, wd_bf):
    b = pl.program_id(1)

    @pl.when(b < nblk_ref[0])
    def _():
        @pl.when(_new_expert(blk_e_ref, b))
        def _():
            wd_bf[...] = wd_ref[...].astype(BF16)

        y_ref[...] = _dot(h_ref[...], wd_bf[...])

    @pl.when(b >= nblk_ref[0])
    def _():
        y_ref[...] = jnp.zeros(y_ref.shape, F32)


def _moe_down(h, w_down, layer, blk_e, nblk):
    P, De = h.shape
    D = w_down.shape[3]
    tn = _pick(D, (2048, 1024, 512, 256, 128))
    nb = P // MOE_ROWS
    return pl.pallas_call(
        _moe_down_kernel,
        grid_spec=pltpu.PrefetchScalarGridSpec(
            num_scalar_prefetch=2,
            grid=(D // tn, nb),
            in_specs=[pl.BlockSpec((MOE_ROWS, De), lambda n, b, be, nbk: (b, 0)),
                      pl.BlockSpec((None, None, De, tn), lambda n, b, be, nbk: (layer, be[b], 0, n))],
            out_specs=pl.BlockSpec((MOE_ROWS, tn), lambda n, b, be, nbk: (b, n)),
            scratch_shapes=[pltpu.VMEM((De, tn), BF16)],
        ),
        out_shape=jax.ShapeDtypeStruct((P, D), F32),
        compiler_params=_cparams(("arbitrary", "arbitrary")),
        name="moe_down",
    )(blk_e, nblk, h, w_down)


def _dispatch_plan(eid, n_experts):
    R = eid.shape[0]
    S = R * TOP_K
    e_flat = eid.reshape(-1)
    onehot = (e_flat[:, None] == jnp.arange(n_experts, dtype=jnp.int32)[None, :]).astype(jnp.int32)
    csum = jnp.cumsum(onehot, axis=0)
    rank = jnp.sum(onehot * csum, axis=1) - 1
    counts = csum[-1]
    padded = (counts + MOE_ROWS - 1) // MOE_ROWS * MOE_ROWS
    pad_end = jnp.cumsum(padded)
    dest = ((pad_end - padded)[e_flat] + rank).astype(jnp.int32)
    nb = -(-S // MOE_ROWS) + n_experts
    buf_tok = jnp.zeros((nb * MOE_ROWS,), jnp.int32).at[dest].set(jnp.arange(S, dtype=jnp.int32) // TOP_K)
    blk_e = jnp.minimum(jnp.searchsorted(pad_end, jnp.arange(nb, dtype=jnp.int32) * MOE_ROWS, side='right'),
                        n_experts - 1).astype(jnp.int32)
    nblk = (pad_end[-1:] // MOE_ROWS).astype(jnp.int32)
    return dest, buf_tok, blk_e, nblk


def _ple_kernel(x_ref, y0_ref, y1_ref, wts_ref, p_ref, g_ref, gd_ref, gu_ref, win_ref, gf_ref, o_ref, *, final):
    wts = wts_ref[...]
    x = x_ref[...] + y0_ref[...] * wts[:, 0:1] + y1_ref[...] * wts[:, 1:2]
    hn = _rms(x, g_ref[...]).astype(BF16)
    t = _dot(hn, gd_ref[...]).astype(BF16)
    gate = jax.nn.sigmoid(_dot(t, gu_ref[...]))
    pe = _dot(p_ref[...], win_ref[...])
    x = x + pe * gate
    if final:
        x = _rms(x, gf_ref[...])
    o_ref[...] = x


def _ple(x, yg, wts, p, g, gd, gu, win, gf, final):
    R, D = x.shape
    Pd = p.shape[1]
    L = wts.shape[1]
    tm = _pick(R, (128, 64, 32, 16, 8))
    nt = R // tm
    row = pl.BlockSpec((tm, D), lambda m: (m, 0))
    return pl.pallas_call(
        functools.partial(_ple_kernel, final=final),
        grid=(nt,),
        in_specs=[
            row,
            pl.BlockSpec((tm, D), lambda m: (m, 0)),
            pl.BlockSpec((tm, D), lambda m: (nt + m, 0)),
            pl.BlockSpec((tm, L), lambda m: (m, 0)),
            pl.BlockSpec((tm, Pd), lambda m: (m, 0)),
            pl.BlockSpec((1, D), lambda m: (0, 0)),
            pl.BlockSpec((D, Pd), lambda m: (0, 0)),
            pl.BlockSpec((Pd, D), lambda m: (0, 0)),
            pl.BlockSpec((Pd, D), lambda m: (0, 0)),
            pl.BlockSpec((1, D), lambda m: (0, 0)),
        ],
        out_specs=row,
        out_shape=jax.ShapeDtypeStruct((R, D), F32),
        compiler_params=_cparams(("arbitrary",)),
        name="moe_combine_ple",
    )(x, yg, yg, wts, p, g, gd, gu, win, gf)


def _moe_ple(x, p, layer, prm, final):
    R, D = x.shape
    assert (R * TOP_K) % MOE_ROWS == 0 and D % LANES == 0
    n_grp = prm["moe_w_grp"].shape[2]
    n_exp = prm["moe_w_exp"].shape[2]
    L = LANES
    w_cat = jnp.concatenate([prm["moe_w_grp"][layer], prm["moe_w_exp"][layer],
                             jnp.zeros((D, L - n_grp - n_exp), F32)], axis=1).astype(BF16)
    b_cat = jnp.concatenate([prm["moe_b_grp"][layer], prm["moe_b_exp"][layer],
                             jnp.zeros((L - n_grp - n_exp,), F32)])[None, :]
    hn, wts, eid = _router(x, prm["norm_ffn"][layer][None, :], w_cat, b_cat, n_grp, n_exp // n_grp)
    dest, buf_tok, blk_e, nblk = _dispatch_plan(eid[:, :TOP_K], n_exp)
    xg = _gather_rows(hn.reshape(R, D // LANES, LANES), buf_tok, nblk, BF16)
    hmid = _moe_up(xg, prm["moe_w_gate"], prm["moe_w_up"], layer, blk_e, nblk)
    y = _moe_down(hmid, prm["moe_w_down"], layer, blk_e, nblk)
    slot_rows = dest.reshape(R, TOP_K).T.reshape(-1)
    all_blocks = jnp.full((1,), R * TOP_K // MOE_ROWS, jnp.int32)
    yg = _gather_rows(y.reshape(y.shape[0], D // LANES, LANES), slot_rows, all_blocks, F32)
    return _ple(x, yg, wts, p, prm["norm_ple"][layer][None, :],
                prm["ple_gate_down"][layer].astype(BF16), prm["ple_gate_up"][layer].astype(BF16),
                prm["ple_w_in"][layer].astype(BF16), prm["norm_final"][None, :], final)


def kernel(x_prompt, x_sample, state_rwkv_shift, state_rwkv_wkv, state_pool, p_prompt, p_sample,
           rwkv_mu, rwkv_w_rkv, rwkv_w_o, rwkv_w0, rwkv_w1, rwkv_w2, rwkv_a0, rwkv_a1, rwkv_a2,
           rwkv_g1, rwkv_g2, rwkv_k_k, rwkv_k_a, rwkv_r_k, rwkv_lnx_w, rwkv_lnx_b,
           pool_w, pool_scale, norm_mix, norm_ffn, norm_ple, norm_final,
           moe_w_grp, moe_b_grp, moe_w_exp, moe_b_exp, moe_w_gate, moe_w_up, moe_w_down,
           ple_w_in, ple_gate_down, ple_gate_up):
    Bp, Tp, D = x_prompt.shape
    Bs, Ts, _ = x_sample.shape
    depth = norm_mix.shape[0]
    H, hs = rwkv_r_k.shape[1], rwkv_r_k.shape[2]
    W = QUAD * hs
    assert D % W == 0 and Ts <= CHUNK and Tp % CHUNK == 0
    lay = _SeqLayout(Bp, Tp, Bs, Ts)
    Rp = Bp * Tp
    prm = dict(moe_w_grp=moe_w_grp, moe_b_grp=moe_b_grp, moe_w_exp=moe_w_exp, moe_b_exp=moe_b_exp,
               moe_w_gate=moe_w_gate, moe_w_up=moe_w_up, moe_w_down=moe_w_down, norm_ffn=norm_ffn,
               norm_ple=norm_ple, norm_final=norm_final, ple_w_in=ple_w_in, ple_gate_down=ple_gate_down,
               ple_gate_up=ple_gate_up)

    x = jnp.concatenate([x_prompt.reshape(Rp, D), x_sample.reshape(Bs * Ts, D)], axis=0)
    Pd = p_prompt.shape[-1]
    p_all = jnp.concatenate([p_prompt.reshape(depth, Rp, Pd), p_sample.reshape(depth, Bs * Ts, Pd)],
                            axis=1).astype(BF16)

    def slabs(vec):
        return vec.reshape(D // W, 1, W)

    shift_p, wkv_p, pool_p, shift_s, wkv_s, pool_s = [], [], [], [], [], []
    for i in range(depth):
        j = i // 2
        g_mix = norm_mix[i][None, :]
        if i % 2 == 0:
            shift_all = jnp.concatenate([jnp.zeros((Bp, D), F32), state_rwkv_shift[j]], axis=0)[:, None, :]
            mixed, h_last = _norm_mix(x, shift_all, g_mix, rwkv_mu[j], lay)
            xr, xw, xk, xv, xa, xg = mixed
            w_rkv = rwkv_w_rkv[j].astype(BF16)
            r = _mm(xr, w_rkv[0:1], slab_out=W, name="proj_r")
            k = _mm(xk, w_rkv[1:2], slab_out=W, name="proj_k")
            v = _mm(xv, w_rkv[2:3], slab_out=W, name="proj_v")
            lw = _lora(xw, rwkv_w1[j].astype(BF16), rwkv_w2[j].astype(BF16), rwkv_w0[j][None, :], "decay", W)
            a = _lora(xa, rwkv_a1[j].astype(BF16), rwkv_a2[j].astype(BF16), rwkv_a0[j][None, :], "aaa", W)
            gl = rwkv_g1.shape[2]
            glp = -(-gl // LANES) * LANES
            g1 = jnp.pad(rwkv_g1[j], ((0, 0), (0, glp - gl))).astype(BF16)
            g2 = jnp.pad(rwkv_g2[j], ((0, glp - gl), (0, 0))).astype(BF16)
            g = _lora(xg, g1, g2, jnp.zeros((1, D), F32), "gate", W)
            sprm = (slabs(rwkv_k_k[j]), slabs(rwkv_k_a[j]), slabs(rwkv_r_k[j].reshape(D)),
                    slabs(rwkv_lnx_w[j]), slabs(rwkv_lnx_b[j]))
            s0_p = jnp.zeros((Bp, D // W, W, W), F32)
            o_p, sp = _scan(r, k, v, lw, a, g, sprm, s0_p, row0=0, n_seq=Bp, t_len=Tp, hs=hs)
            o_s, ss = _scan(r, k, v, lw, a, g, sprm, _pack_state(state_rwkv_wkv[j], hs),
                            row0=Rp, n_seq=Bs, t_len=Ts, hs=hs)
            o = jnp.concatenate([o_p, o_s], axis=1)
            x = _mm(o, rwkv_w_o[j].astype(BF16)[None], res=x, slab_in=True, name="proj_o")
            shift_p.append(h_last[:Bp, -1])
            shift_s.append(h_last[Bp:, -1])
            wkv_p.append(_unpack_state(sp, hs))
            wkv_s.append(_unpack_state(ss, hs))
        else:
            hist = jnp.concatenate([jnp.zeros((Bp, 16, D), F32),
                                    jnp.pad(state_pool[j], ((0, 0), (1, 0), (0, 0)))], axis=0)
            d, h_last = _pool(x, hist, g_mix, lay)
            x = _mm(d, pool_w[j].astype(BF16), scale=pool_scale[j][None, :], res=x, name="pool_proj")
            nh = state_pool.shape[2]
            pool_p.append(h_last[:Bp, SEQ_TILE - nh:])
            pool_s.append(h_last[Bp:, SEQ_TILE - nh:])
        x = _moe_ple(x, p_all[i], i, prm, final=(i == depth - 1))

    y_prompt = x[:Rp].reshape(Bp, Tp, D)
    y_sample = x[Rp:].reshape(Bs, Ts, D)
    return (y_prompt, y_sample, jnp.stack(shift_p), jnp.stack(wkv_p), jnp.stack(pool_p),
            jnp.stack(shift_s), jnp.stack(wkv_s), jnp.stack(pool_s))
```

```python
import functools

import jax
import jax.numpy as jnp
from jax import lax
from jax.experimental import pallas as pl
from jax.experimental.pallas import tpu as pltpu

F32 = jnp.float32
BF16 = jnp.bfloat16

NORM_EPS = 1e-6
GN_EPS = 64e-5
PAST_LEN = 4096
POOL_WINDOWS = (2, 4, 8, 16)
TOP_K = 2
LANES = 128
SEQ_TILE = 32
CHUNK = 64
QUAD = 4
SCAN_GROUP = 4
MOE_ROWS = 256
GATHER_PITCH = 36
VMEM_LIMIT = 56 * 1024 * 1024


def _cparams(sem):
    return pltpu.CompilerParams(dimension_semantics=sem, vmem_limit_bytes=VMEM_LIMIT)


def _pick(n, cands):
    for c in cands:
        if n % c == 0:
            return c
    return n


def _rms(x, g):
    return x * lax.rsqrt(jnp.mean(x * x, axis=-1, keepdims=True) + NORM_EPS) * g


_NN = (((1,), (0,)), ((), ()))
_NT = (((1,), (1,)), ((), ()))
_TN = (((0,), (0,)), ((), ()))


def _dot(a, b, dims=_NN, prec=None):
    return lax.dot_general(a, b, dims, precision=prec, preferred_element_type=F32)


def _mm_kernel(*refs, has_res, has_scale, a_slabs, o_slabs):
    a_ref, w_ref = refs[0], refs[1]
    k = 2
    if a_slabs:
        a = jnp.concatenate([a_ref[q] for q in range(a_slabs)], axis=1)
    else:
        a = a_ref[...]
    acc = _dot(a, w_ref[...])
    if has_scale:
        acc = acc * refs[k][...]
        k += 1
    if has_res:
        acc = refs[k][...] + acc
        k += 1
    o_ref = refs[k]
    if o_slabs:
        sw = o_ref.shape[2]
        for q in range(o_slabs):
            o_ref[q] = acc[:, q * sw:(q + 1) * sw].astype(o_ref.dtype)
    else:
        o_ref[...] = acc.astype(o_ref.dtype)


def _mm(a, w, *, res=None, scale=None, out_dtype=F32, slab_in=False, slab_out=0, name="mm"):
    M = a.shape[1] if slab_in else a.shape[0]
    G, Kg, Ng = w.shape
    tm = _pick(M, (512, 256, 128, 64, 32, 16, 8))
    tn = _pick(Ng, (1024, 512, 256, 128))
    nn = Ng // tn
    if slab_in:
        assert G == 1
        a_spec = pl.BlockSpec((a.shape[0], tm, a.shape[2]), lambda g, n, m: (0, m, 0))
    else:
        a_spec = pl.BlockSpec((tm, Kg), lambda g, n, m: (m, g))
    in_specs = [a_spec, pl.BlockSpec((None, Kg, tn), lambda g, n, m: (g, 0, n))]
    args = [a, w]
    if scale is not None:
        in_specs.append(pl.BlockSpec((1, tn), lambda g, n, m: (0, g * nn + n)))
        args.append(scale)
    if res is not None:
        in_specs.append(pl.BlockSpec((tm, tn), lambda g, n, m: (m, g * nn + n)))
        args.append(res)
    if slab_out:
        assert tn % slab_out == 0
        per = tn // slab_out
        out_spec = pl.BlockSpec((per, tm, slab_out), lambda g, n, m: (g * nn + n, m, 0))
        out_shape = jax.ShapeDtypeStruct((G * Ng // slab_out, M, slab_out), out_dtype)
    else:
        per = 0
        out_spec = pl.BlockSpec((tm, tn), lambda g, n, m: (m, g * nn + n))
        out_shape = jax.ShapeDtypeStruct((M, G * Ng), out_dtype)
    return pl.pallas_call(
        functools.partial(_mm_kernel, has_res=res is not None, has_scale=scale is not None,
                          a_slabs=a.shape[0] if slab_in else 0, o_slabs=per),
        grid=(G, nn, M // tm),
        in_specs=in_specs,
        out_specs=out_spec,
        out_shape=out_shape,
        compiler_params=_cparams(("arbitrary", "arbitrary", "arbitrary")),
        name=name,
    )(*args)


def _lora_kernel(a_ref, w1_ref, w2_ref, b_ref, o_ref, *, mode):
    t = _dot(a_ref[...], w1_ref[...])
    if mode == "decay":
        t = jnp.tanh(t)
    elif mode == "gate":
        t = jax.nn.sigmoid(t)
    z = _dot(t.astype(BF16), w2_ref[...]) + b_ref[...]
    if mode == "decay":
        z = -jnp.exp(-jax.nn.softplus(-z) - 0.5)
    elif mode == "aaa":
        z = jax.nn.sigmoid(z)
    sw = o_ref.shape[2]
    for q in range(o_ref.shape[0]):
        o_ref[q] = z[:, q * sw:(q + 1) * sw]


def _lora(a, w1, w2, bias, mode, sw):
    M, K = a.shape
    L = w1.shape[1]
    D = w2.shape[1]
    tm = _pick(M, (256, 128, 64, 32, 16, 8))
    return pl.pallas_call(
        functools.partial(_lora_kernel, mode=mode),
        grid=(M // tm,),
        in_specs=[
            pl.BlockSpec((tm, K), lambda m: (m, 0)),
            pl.BlockSpec((K, L), lambda m: (0, 0)),
            pl.BlockSpec((L, D), lambda m: (0, 0)),
            pl.BlockSpec((1, D), lambda m: (0, 0)),
        ],
        out_specs=pl.BlockSpec((D // sw, tm, sw), lambda m: (0, m, 0)),
        out_shape=jax.ShapeDtypeStruct((D // sw, M, sw), F32),
        compiler_params=_cparams(("arbitrary",)),
        name="lora_" + mode,
    )(a, w1, w2, bias)


class _SeqLayout:
    def __init__(self, bp, tp, bs, ts):
        assert tp % SEQ_TILE == 0 and ts % SEQ_TILE == 0
        self.bp, self.tp, self.bs, self.ts = bp, tp, bs, ts
        self.tiles_p = tp // SEQ_TILE
        self.tiles_s = ts // SEQ_TILE
        self.n_tiles_p = bp * self.tiles_p
        self.n_tiles = self.n_tiles_p + bs * self.tiles_s
        self.n_seq = bp + bs
        self.rows = bp * tp + bs * ts

    def seq_of_tile(self, i):
        return jnp.where(i < self.n_tiles_p, i // self.tiles_p,
                         self.bp + (i - self.n_tiles_p) // self.tiles_s)

    def tile_in_seq(self, i):
        return jnp.where(i < self.n_tiles_p, i % self.tiles_p, (i - self.n_tiles_p) % self.tiles_s)


def _norm_mix_kernel(x_ref, xp_ref, sh_ref, g_ref, mu_ref, *outs, lay):
    i = pl.program_id(0)
    g = g_ref[...]
    h = _rms(x_ref[...], g)
    h_before = _rms(xp_ref[7:8, :], g)
    first = lay.tile_in_seq(i) == 0
    prev_row = jnp.where(first, sh_ref[...], h_before)
    rows = lax.broadcasted_iota(jnp.int32, h.shape, 0)
    h_prev = jnp.where(rows == 0, prev_row, pltpu.roll(h, 1, axis=0))
    xx = h_prev - h
    for n in range(6):
        outs[n][...] = (h + xx * mu_ref[n:n + 1, :]).astype(BF16)
    outs[6][...] = h


def _norm_mix(x, shift_all, g, mu, lay):
    R, D = x.shape
    tpb = SEQ_TILE // 8
    row_spec = pl.BlockSpec((SEQ_TILE, D), lambda i: (i, 0))
    outs = pl.pallas_call(
        functools.partial(_norm_mix_kernel, lay=lay),
        grid=(lay.n_tiles,),
        in_specs=[
            row_spec,
            pl.BlockSpec((8, D), lambda i: (jnp.maximum(i * tpb - 1, 0), 0)),
            pl.BlockSpec((None, 1, D), lambda i: (lay.seq_of_tile(i), 0, 0)),
            pl.BlockSpec((1, D), lambda i: (0, 0)),
            pl.BlockSpec((6, D), lambda i: (0, 0)),
        ],
        out_specs=[row_spec] * 6 + [pl.BlockSpec((None, SEQ_TILE, D), lambda i: (lay.seq_of_tile(i), 0, 0))],
        out_shape=[jax.ShapeDtypeStruct((R, D), BF16)] * 6
        + [jax.ShapeDtypeStruct((lay.n_seq, SEQ_TILE, D), F32)],
        compiler_params=_cparams(("arbitrary",)),
        name="norm_mix",
    )(x, x, shift_all, g, mu)
    return outs[:6], outs[6]


def _scan_kernel(r_ref, k_ref, v_ref, lw_ref, a_ref, g_ref, kk_ref, ka_ref, rk_ref, lnw_ref, lnb_ref,
                 *rest, t_blk, hs, group, n_chunks):
    s0_ref = rest[0] if len(rest) == 3 else None
    o_ref, s_ref = rest[-2:]
    C = CHUNK
    nq, _, W = r_ref.shape
    c = pl.program_id(1)

    @pl.when(c == 0)
    def _():
        s_ref[...] = jnp.zeros(s_ref.shape, F32) if s0_ref is None else s0_ref[...]

    ri = lax.broadcasted_iota(jnp.int32, (W, W), 0)
    ci = lax.broadcasted_iota(jnp.int32, (W, W), 1)
    head_blk = (ri // hs) == (ci // hs)
    ones_blk = jnp.where(head_blk, 1.0, 0.0).astype(F32)
    ti = lax.broadcasted_iota(jnp.int32, (C, C), 0)
    si = lax.broadcasted_iota(jnp.int32, (C, C), 1)
    tri_incl = jnp.where(ti >= si, 1.0, 0.0).astype(F32)
    lane_head = lax.broadcasted_iota(jnp.int32, (C, W), 1) // hs
    rj = lax.broadcasted_iota(jnp.int32, (QUAD * C, QUAD * C), 0)
    cj = lax.broadcasted_iota(jnp.int32, (QUAD * C, QUAD * C), 1)
    t_in = rj % C
    s_in = cj % C
    strict = jnp.where(t_in > s_in, 1.0, 0.0).astype(F32)
    incl = jnp.where(t_in >= s_in, 1.0, 0.0).astype(F32)
    eye = jnp.where(rj == cj, 1.0, 0.0).astype(F32)

    ones_bf = ones_blk.astype(BF16)
    head_masks = [jnp.where(lane_head == h, 1.0, 0.0).astype(BF16) for h in range(QUAD)]

    def fold(x):
        y = x[0:C]
        for h in range(1, QUAD):
            y = y + x[h * C:(h + 1) * C]
        return y

    def group_body(it, carry, *, passes, group):
        qs = [it * group + j for j in range(group)]
        each = lambda f, *xs: [f(*t) for t in zip(*xs)]

        def parts(x):
            hi = x.astype(BF16)
            if passes == 1:
                return (hi,)
            return (hi, (x - hi.astype(F32)).astype(BF16))

        def bdot(x, y, dims=_NN):
            out = _dot(x[0], y[0], dims)
            if passes > 1:
                out = out + (_dot(x[0], y[1], dims) + _dot(x[1], y[0], dims))
            return out

        def head_sum(x):
            out = _dot(parts(x)[0], ones_bf)
            if passes > 1:
                out = out + _dot(parts(x)[1], ones_bf)
            return out

        def stack(x):
            return tuple(jnp.concatenate([xb * m for m in head_masks], axis=0) for xb in parts(x))

        def load(ref):
            def one(q):
                x = ref[q]
                if t_blk < C:
                    x = jnp.concatenate([x, jnp.zeros((C - t_blk, W), F32)], axis=0)
                return x
            return [one(q) for q in qs]

        r, k, v, lw, a = load(r_ref), load(k_ref), load(v_ref), load(lw_ref), load(a_ref)
        kk = each(lambda k_, q: k_ * kk_ref[q], k, qs)
        kk = each(lambda x: x * lax.rsqrt(jnp.maximum(head_sum(x * x), 1e-24)), kk)
        kf = each(lambda k_, a_, q: k_ * (1.0 + (a_ - 1.0) * ka_ref[q]), k, a, qs)
        bv = each(lambda x, a_: x * a_, kk, a)

        cum = each(lambda x: _dot(tri_incl, x, _NN, lax.Precision.HIGHEST), lw)
        e_neg = each(lambda x: jnp.exp(-x), cum)
        rt = each(lambda r_, x: r_ * jnp.exp(x), r, cum)
        at = each(lambda x, c_, l_: -x * jnp.exp(c_ - l_), kk, cum, lw)
        kt = each(lambda x, e: x * e, kf, e_neg)
        bt = each(lambda x, e: x * e, bv, e_neg)
        p_c = each(lambda x: jnp.exp(x[C - 1:C, :]), cum)

        a_st, r_st, b_st, k_st, v_st = (each(stack, x) for x in (at, rt, bt, kt, v))
        a_ab = each(lambda x, y: bdot(x, y, _NT) * strict, a_st, b_st)
        a_ak = each(lambda x, y: parts(bdot(x, y, _NT) * strict), a_st, k_st)
        a_rb = each(lambda x, y: parts(bdot(x, y, _NT) * incl), r_st, b_st)
        a_rk = each(lambda x, y: parts(bdot(x, y, _NT) * incl), r_st, k_st)

        t_inv = each(lambda x: eye + x, a_ab)
        n_pow = each(parts, a_ab)
        p = 1
        while 2 * p < C:
            n_pow = each(lambda x: parts(bdot(x, x)), n_pow)
            t_inv = each(lambda t, n: t + bdot(parts(t), n), t_inv, n_pow)
            p *= 2
        t_bf = each(parts, t_inv)

        s = [s_ref[q] for q in qs]
        s_bf = each(parts, s)
        u_in = each(lambda a_, s_, ak, v_: parts(bdot(a_, s_, _NT) + bdot(ak, v_)), a_st, s_bf, a_ak, v_st)
        u_st = each(bdot, t_bf, u_in)
        o_st = each(lambda r_, s_, rb, u_, rk, v_: bdot(r_, s_, _NT) + bdot(rb, parts(u_)) + bdot(rk, v_),
                    r_st, s_bf, a_rb, u_st, a_rk, v_st)
        o = each(fold, o_st)
        u = each(fold, u_st)
        upd = each(lambda u_, v_, b_, k_, pc: bdot(parts(jnp.concatenate([u_, v_], axis=0)),
                                                   parts(jnp.concatenate([b_ * pc, k_ * pc], axis=0)), _TN),
                   u, v, bt, kt, p_c)
        for q, s_, pc, up in zip(qs, s, p_c, upd):
            s_ref[q] = s_ * pc + jnp.where(head_blk, up, 0.0)

        inv_n = 1.0 / hs
        o_c = each(lambda x: x - head_sum(x) * inv_n, o)
        o_n = each(lambda x: x * lax.rsqrt(head_sum(x * x) * inv_n + GN_EPS), o_c)
        bonus = each(lambda r_, k_, q: head_sum(r_ * k_ * rk_ref[q]), r, kf, qs)
        for q, x, bo, v_ in zip(qs, o_n, bonus, v):
            y = x * lnw_ref[q] + lnb_ref[q] + bo * v_
            o_ref[q] = (y[:t_blk] * g_ref[q]).astype(o_ref.dtype)
        return carry

    def run(passes, grp):
        lax.fori_loop(0, nq // grp, functools.partial(group_body, passes=passes, group=grp), 0)

    if n_chunks > 1:
        last = n_chunks - 1

        @pl.when(c < last)
        def _():
            run(1, group)

        @pl.when(c == last)
        def _():
            run(3, _pick(nq, (2, 1)))
    else:
        run(1, group)


def _scan(r, k, v, lw, a, g, prm, s0, *, row0, n_seq, t_len, hs):
    nq, _, W = r.shape
    t_blk = min(CHUNK, t_len)
    n_chunks = t_len // t_blk
    blk0 = row0 // t_blk
    row_spec = pl.BlockSpec((nq, t_blk, W), lambda b, c: (0, blk0 + b * n_chunks + c, 0))
    prm_spec = pl.BlockSpec((nq, 1, W), lambda b, c: (0, 0, 0))
    st_spec = pl.BlockSpec((None, nq, W, W), lambda b, c: (b, 0, 0, 0))
    return pl.pallas_call(
        functools.partial(_scan_kernel, t_blk=t_blk, hs=hs, group=_pick(nq, (SCAN_GROUP, 2, 1)),
                          n_chunks=n_chunks),
        grid=(n_seq, n_chunks),
        in_specs=[row_spec] * 6 + [prm_spec] * 5 + ([] if s0 is None else [st_spec]),
        out_specs=[pl.BlockSpec((nq, t_blk, W), lambda b, c: (0, b * n_chunks + c, 0)), st_spec],
        out_shape=[jax.ShapeDtypeStruct((nq, n_seq * t_len, W), BF16),
                   jax.ShapeDtypeStruct((n_seq, nq, W, W), F32)],
        compiler_params=_cparams(("arbitrary", "arbitrary")),
        name="rwkv_scan",
    )(r, k, v, lw, a, g, *prm, *(() if s0 is None else (s0,)))


def _pack_state(s, hs):
    n, H = s.shape[0], s.shape[1]
    s = s.reshape(n, H // QUAD, QUAD, hs, hs)
    eye = jnp.eye(QUAD, dtype=s.dtype)
    bd = s[:, :, :, :, None, :] * eye[None, None, :, None, :, None]
    return bd.reshape(n, H // QUAD, QUAD * hs, QUAD * hs)


def _unpack_state(bd, hs):
    n, nq = bd.shape[0], bd.shape[1]
    x = bd.reshape(n, nq, QUAD, hs, QUAD, hs)
    x = jnp.stack([x[:, :, j, :, j, :] for j in range(QUAD)], axis=2)
    return x.reshape(n, nq * QUAD, hs, hs)


def _pool_kernel(x_ref, xp_ref, hist_ref, g_ref, d_ref, hl_ref, *, lay):
    i = pl.program_id(0)
    g = g_ref[...]
    h = _rms(x_ref[...], g)
    tis = lay.tile_in_seq(i)
    prev = jnp.where(tis == 0, hist_ref[...], _rms(xp_ref[...], g))
    ext = jnp.concatenate([prev, h], axis=0)
    D = h.shape[1]
    gw = D // len(POOL_WINDOWS)
    pos0 = jnp.where(i < lay.n_tiles_p, 0, PAST_LEN) + tis * SEQ_TILE
    pos = pos0 + lax.broadcasted_iota(jnp.int32, (SEQ_TILE, 1), 0)
    for gi, w in enumerate(POOL_WINDOWS):
        e = ext[:, gi * gw:(gi + 1) * gw]
        s = e
        step = 1
        while step < w:
            s = s + pltpu.roll(s, step, axis=0)
            step *= 2
        cnt = jnp.minimum(pos + 1, w).astype(F32)
        mean = s[16:] / cnt
        d_ref[:, gi * gw:(gi + 1) * gw] = (mean - h[:, gi * gw:(gi + 1) * gw]).astype(BF16)
    hl_ref[...] = h


def _pool(x, hist16, g, lay):
    R, D = x.shape
    tp16 = SEQ_TILE // 16
    return pl.pallas_call(
        functools.partial(_pool_kernel, lay=lay),
        grid=(lay.n_tiles,),
        in_specs=[
            pl.BlockSpec((SEQ_TILE, D), lambda i: (i, 0)),
            pl.BlockSpec((16, D), lambda i: (jnp.maximum(i * tp16 - 1, 0), 0)),
            pl.BlockSpec((None, 16, D), lambda i: (lay.seq_of_tile(i), 0, 0)),
            pl.BlockSpec((1, D), lambda i: (0, 0)),
        ],
        out_specs=[pl.BlockSpec((SEQ_TILE, D), lambda i: (i, 0)),
                   pl.BlockSpec((None, SEQ_TILE, D), lambda i: (lay.seq_of_tile(i), 0, 0))],
        out_shape=[jax.ShapeDtypeStruct((R, D), BF16),
                   jax.ShapeDtypeStruct((lay.n_seq, SEQ_TILE, D), F32)],
        compiler_params=_cparams(("arbitrary",)),
        name="pool_mix",
    )(x, x, hist16, g)


def _router_kernel(x_ref, g_ref, w_ref, b_ref, hn_ref, wts_ref, eid_ref, cnt_ref, *, n_grp, per_grp):
    hn = _rms(x_ref[...], g_ref[...])
    hn_ref[...] = hn
    logits = _dot(hn.astype(BF16), w_ref[...]) + b_ref[...]
    lane = lax.broadcasted_iota(jnp.int32, logits.shape, 1).astype(F32)
    neg = -jnp.inf
    big = float(logits.shape[1])
    gl = jnp.where(lane < n_grp, logits, neg)
    gmax = jnp.max(gl, axis=-1, keepdims=True)
    gsum = jnp.sum(jnp.exp(gl - gmax), axis=-1, keepdims=True)
    g_idx = jnp.min(jnp.where(gl == gmax, lane, big), axis=-1, keepdims=True)
    g_gate = 1.0 / gsum
    lo = n_grp + g_idx * per_grp
    el = jnp.where((lane >= lo) & (lane < lo + per_grp), logits, neg)
    e1 = jnp.max(el, axis=-1, keepdims=True)
    i1 = jnp.min(jnp.where(el == e1, lane, big), axis=-1, keepdims=True)
    el2 = jnp.where(lane == i1, neg, el)
    e2 = jnp.max(el2, axis=-1, keepdims=True)
    i2 = jnp.min(jnp.where(el2 == e2, lane, big), axis=-1, keepdims=True)
    esum = jnp.sum(jnp.exp(el - e1), axis=-1, keepdims=True)
    p1 = 1.0 / esum
    p2 = jnp.exp(e2 - e1) / esum
    w1 = g_gate * p1 / (p1 + p2)
    w2 = g_gate * p2 / (p1 + p2)
    wts_ref[...] = jnp.where(lane == 0, w1, jnp.where(lane == 1, w2, 0.0))

    @pl.when(pl.program_id(0) == 0)
    def _():
        cnt_ref[...] = jnp.zeros(cnt_ref.shape, F32)

    e1 = i1 - n_grp
    e2 = i2 - n_grp
    oh1 = jnp.where(lane == e1, 1.0, 0.0)
    oh2 = jnp.where(lane == e2, 1.0, 0.0)
    both = oh1 + oh2
    tm = both.shape[0]
    before = (lax.broadcasted_iota(jnp.int32, (tm, tm), 1) < lax.broadcasted_iota(jnp.int32, (tm, tm), 0))
    seen = _dot(jnp.where(before, 1.0, 0.0).astype(BF16), both.astype(BF16)) + cnt_ref[...]
    rank1 = jnp.sum(oh1 * seen, axis=-1, keepdims=True)
    rank2 = jnp.sum(oh2 * seen, axis=-1, keepdims=True)
    cnt_ref[...] += jnp.sum(both, axis=0, keepdims=True)
    eid_ref[...] = jnp.where(lane == 0, e1, jnp.where(lane == 1, e2, jnp.where(
        lane == 2, rank1, jnp.where(lane == 3, rank2, 0.0)))).astype(jnp.int32)


def _router(x, g, w_cat, b_cat, n_grp, per_grp):
    R, D = x.shape
    tm = _pick(R, (256, 128, 64, 32, 16, 8))
    L = w_cat.shape[1]
    return pl.pallas_call(
        functools.partial(_router_kernel, n_grp=n_grp, per_grp=per_grp),
        grid=(R // tm,),
        in_specs=[
            pl.BlockSpec((tm, D), lambda m: (m, 0)),
            pl.BlockSpec((1, D), lambda m: (0, 0)),
            pl.BlockSpec((D, L), lambda m: (0, 0)),
            pl.BlockSpec((1, L), lambda m: (0, 0)),
        ],
        out_specs=[pl.BlockSpec((tm, D), lambda m: (m, 0)),
                   pl.BlockSpec((tm, L), lambda m: (m, 0)),
                   pl.BlockSpec((tm, L), lambda m: (m, 0)),
                   pl.BlockSpec((1, L), lambda m: (0, 0))],
        out_shape=[jax.ShapeDtypeStruct((R, D), F32),
                   jax.ShapeDtypeStruct((R, L), F32),
                   jax.ShapeDtypeStruct((R, L), jnp.int32),
                   jax.ShapeDtypeStruct((1, L), F32)],
        compiler_params=_cparams(("arbitrary",)),
        name="moe_router",
    )(x, g, w_cat, b_cat)


def _gather_kernel(idx_ref, nblk_ref, src_ref, o_ref, buf0, buf1, sem):
    G = o_ref.shape[0]
    nc = src_ref.shape[1]
    b = pl.program_id(0)
    nblk = nblk_ref[0]
    bufs = (buf0, buf1)

    def row_copy(blk, slot, j):
        return pltpu.make_async_copy(src_ref.at[idx_ref[blk * G + j]],
                                     bufs[slot].at[pl.ds(j * GATHER_PITCH, nc)], sem.at[slot])

    def fetch(blk, slot):
        def issue(j, carry):
            row_copy(blk, slot, j).start()
            return carry
        lax.fori_loop(0, G, issue, 0)

    @pl.when((b == 0) & (nblk > 0))
    def _():
        fetch(0, 0)

    for slot in range(2):
        @pl.when((b < nblk) & (b % 2 == slot))
        def _(slot=slot):
            @pl.when(b + 1 < nblk)
            def _():
                fetch(b + 1, 1 - slot)

            def drain(j, carry):
                row_copy(b, slot, j).wait()
                return carry

            lax.fori_loop(0, G, drain, 0)
            for cc in range(nc):
                o_ref[:, cc * LANES:(cc + 1) * LANES] = (
                    bufs[slot][pl.ds(cc, G, stride=GATHER_PITCH), :].astype(o_ref.dtype))

    @pl.when(b >= nblk_ref[0])
    def _():
        o_ref[...] = jnp.zeros(o_ref.shape, o_ref.dtype)


def _gather_rows(src3, idx, nblk, out_dtype):
    n = idx.shape[0]
    _, nc, lanes = src3.shape
    G = MOE_ROWS
    assert n % G == 0 and lanes == LANES
    return pl.pallas_call(
        _gather_kernel,
        grid_spec=pltpu.PrefetchScalarGridSpec(
            num_scalar_prefetch=2,
            grid=(n // G,),
            in_specs=[pl.BlockSpec(memory_space=pl.ANY)],
            out_specs=pl.BlockSpec((G, nc * LANES), lambda b, idx_r, nb_r: (b, 0)),
            scratch_shapes=[pltpu.VMEM((G * GATHER_PITCH, LANES), src3.dtype),
                            pltpu.VMEM((G * GATHER_PITCH, LANES), src3.dtype),
                            pltpu.SemaphoreType.DMA((2,))],
        ),
        out_shape=jax.ShapeDtypeStruct((n, nc * LANES), out_dtype),
        compiler_params=_cparams(("arbitrary",)),
        name="row_gather",
    )(idx, nblk, src3)


def _new_expert(blk_e_ref, b):
    prev = blk_e_ref[jnp.maximum(b - 1, 0)]
    return (b == 0) | (blk_e_ref[b] != prev)


def _moe_up_kernel(blk_e_ref, nblk_ref, x_ref, wg_ref, wu_ref, h_ref, wg_bf, wu_bf):
    b = pl.program_id(1)

    @pl.when(b < nblk_ref[0])
    def _():
        @pl.when(_new_expert(blk_e_ref, b))
        def _():
            wg_bf[...] = wg_ref[...].astype(BF16)
            wu_bf[...] = wu_ref[...].astype(BF16)

        x = x_ref[...]
        gate = _dot(x, wg_bf[...])
        up = _dot(x, wu_bf[...])
        h_ref[...] = (jax.nn.silu(gate) * up).astype(BF16)

    @pl.when(b >= nblk_ref[0])
    def _():
        h_ref[...] = jnp.zeros(h_ref.shape, BF16)


def _moe_up(xg, w_gate, w_up, layer, blk_e, nblk):
    P, D = xg.shape
    De = w_gate.shape[3]
    tj = _pick(De, (512, 256, 128))
    nb = P // MOE_ROWS
    w_spec = pl.BlockSpec((None, None, D, tj), lambda j, b, be, nbk: (layer, be[b], 0, j))
    return pl.pallas_call(
        _moe_up_kernel,
        grid_spec=pltpu.PrefetchScalarGridSpec(
            num_scalar_prefetch=2,
            grid=(De // tj, nb),
            in_specs=[pl.BlockSpec((MOE_ROWS, D), lambda j, b, be, nbk: (b, 0)), w_spec, w_spec],
            out_specs=pl.BlockSpec((MOE_ROWS, tj), lambda j, b, be, nbk: (b, j)),
            scratch_shapes=[pltpu.VMEM((D, tj), BF16), pltpu.VMEM((D, tj), BF16)],
        ),
        out_shape=jax.ShapeDtypeStruct((P, De), BF16),
        compiler_params=_cparams(("arbitrary", "arbitrary")),
        name="moe_up",
    )(blk_e, nblk, xg, w_gate, w_up)


def _moe_down_kernel(blk_e_ref, nblk_ref, h_ref, wd_ref, y_ref, wd_bf):
    b = pl.program_id(1)

    @pl.when(b < nblk_ref[0])
    def _():
        @pl.when(_new_expert(blk_e_ref, b))
        def _():
            wd_bf[...] = wd_ref[...].astype(BF16)

        y = _dot(h_ref[...], wd_bf[...])
        for cc in range(y_ref.shape[1]):
            y_ref[:, cc, :] = y[:, cc * LANES:(cc + 1) * LANES]

    @pl.when(b >= nblk_ref[0])
    def _():
        y_ref[...] = jnp.zeros(y_ref.shape, F32)


def _moe_down(h, w_down, layer, blk_e, nblk):
    P, De = h.shape
    D = w_down.shape[3]
    tn = _pick(D, (2048, 1024, 512, 256, 128))
    nb = P // MOE_ROWS
    return pl.pallas_call(
        _moe_down_kernel,
        grid_spec=pltpu.PrefetchScalarGridSpec(
            num_scalar_prefetch=2,
            grid=(D // tn, nb),
            in_specs=[pl.BlockSpec((MOE_ROWS, De), lambda n, b, be, nbk: (b, 0)),
                      pl.BlockSpec((None, None, De, tn), lambda n, b, be, nbk: (layer, be[b], 0, n))],
            out_specs=pl.BlockSpec((MOE_ROWS, tn // LANES, LANES), lambda n, b, be, nbk: (b, n, 0)),
            scratch_shapes=[pltpu.VMEM((De, tn), BF16)],
        ),
        out_shape=jax.ShapeDtypeStruct((P, D // LANES, LANES), F32),
        compiler_params=_cparams(("arbitrary", "arbitrary")),
        name="moe_down",
    )(blk_e, nblk, h, w_down)


def _dispatch_plan(eid, rank, counts):
    R = eid.shape[0]
    S = R * TOP_K
    n_experts = counts.shape[0]
    e_flat = eid.reshape(-1)
    padded = (counts + MOE_ROWS - 1) // MOE_ROWS * MOE_ROWS
    pad_end = jnp.cumsum(padded)
    dest = ((pad_end - padded)[e_flat] + rank.reshape(-1)).astype(jnp.int32)
    nb = -(-S // MOE_ROWS) + n_experts
    buf_tok = jnp.zeros((nb * MOE_ROWS,), jnp.int32).at[dest].set(jnp.arange(S, dtype=jnp.int32) // TOP_K)
    blk_e = jnp.minimum(jnp.searchsorted(pad_end, jnp.arange(nb, dtype=jnp.int32) * MOE_ROWS, side='right'),
                        n_experts - 1).astype(jnp.int32)
    nblk = (pad_end[-1:] // MOE_ROWS).astype(jnp.int32)
    return dest, buf_tok, blk_e, nblk


def _ple_kernel(x_ref, y0_ref, y1_ref, wts_ref, p_ref, g_ref, gd_ref, gu_ref, win_ref, gf_ref, *o_refs,
                final, head_tiles):
    wts = wts_ref[...]
    x = x_ref[...] + y0_ref[...] * wts[:, 0:1] + y1_ref[...] * wts[:, 1:2]
    hn = _rms(x, g_ref[...]).astype(BF16)
    t = _dot(hn, gd_ref[...]).astype(BF16)
    gate = jax.nn.sigmoid(_dot(t, gu_ref[...]))
    pe = _dot(p_ref[...], win_ref[...])
    x = x + pe * gate
    if not final:
        o_refs[0][...] = x
        return
    x = _rms(x, gf_ref[...])
    m = pl.program_id(0)

    @pl.when(m < head_tiles)
    def _():
        o_refs[0][...] = x

    @pl.when(m >= head_tiles)
    def _():
        o_refs[1][...] = x


def _ple(x, yg, wts, p, g, gd, gu, win, gf, final, head_rows):
    R, D = x.shape
    Pd = p.shape[1]
    L = wts.shape[1]
    tm = _pick(R, (128, 64, 32, 16, 8))
    nt = R // tm
    row = pl.BlockSpec((tm, D), lambda m: (m, 0))
    assert head_rows % tm == 0
    ht = head_rows // tm
    if final:
        out_specs = [pl.BlockSpec((tm, D), lambda m: (jnp.minimum(m, ht - 1), 0)),
                     pl.BlockSpec((tm, D), lambda m: (jnp.maximum(m - ht, 0), 0))]
        out_shape = [jax.ShapeDtypeStruct((head_rows, D), F32), jax.ShapeDtypeStruct((R - head_rows, D), F32)]
    else:
        out_specs = row
        out_shape = jax.ShapeDtypeStruct((R, D), F32)
    return pl.pallas_call(
        functools.partial(_ple_kernel, final=final, head_tiles=ht),
        grid=(nt,),
        in_specs=[
            row,
            pl.BlockSpec((tm, D), lambda m: (m, 0)),
            pl.BlockSpec((tm, D), lambda m: (nt + m, 0)),
            pl.BlockSpec((tm, L), lambda m: (m, 0)),
            pl.BlockSpec((tm, Pd), lambda m: (m, 0)),
            pl.BlockSpec((1, D), lambda m: (0, 0)),
            pl.BlockSpec((D, Pd), lambda m: (0, 0)),
            pl.BlockSpec((Pd, D), lambda m: (0, 0)),
            pl.BlockSpec((Pd, D), lambda m: (0, 0)),
            pl.BlockSpec((1, D), lambda m: (0, 0)),
        ],
        out_specs=out_specs,
        out_shape=out_shape,
        compiler_params=_cparams(("arbitrary",)),
        name="moe_combine_ple",
    )(x, yg, yg, wts, p, g, gd, gu, win, gf)


def _moe_ple(x, p, layer, prm, final, head_rows):
    R, D = x.shape
    assert (R * TOP_K) % MOE_ROWS == 0 and D % LANES == 0
    n_grp = prm["moe_w_grp"].shape[2]
    n_exp = prm["moe_w_exp"].shape[2]
    L = LANES
    w_cat = jnp.concatenate([prm["moe_w_grp"][layer], prm["moe_w_exp"][layer],
                             jnp.zeros((D, L - n_grp - n_exp), F32)], axis=1).astype(BF16)
    b_cat = jnp.concatenate([prm["moe_b_grp"][layer], prm["moe_b_exp"][layer],
                             jnp.zeros((L - n_grp - n_exp,), F32)])[None, :]
    hn, wts, route, cnt = _router(x, prm["norm_ffn"][layer][None, :], w_cat, b_cat, n_grp, n_exp // n_grp)
    dest, buf_tok, blk_e, nblk = _dispatch_plan(route[:, :TOP_K], route[:, TOP_K:2 * TOP_K],
                                                cnt[0, :n_exp].astype(jnp.int32))
    xg = _gather_rows(hn.reshape(R, D // LANES, LANES), buf_tok, nblk, BF16)
    hmid = _moe_up(xg, prm["moe_w_gate"], prm["moe_w_up"], layer, blk_e, nblk)
    y3 = _moe_down(hmid, prm["moe_w_down"], layer, blk_e, nblk)
    slot_rows = dest.reshape(R, TOP_K).T.reshape(-1)
    all_blocks = jnp.full((1,), R * TOP_K // MOE_ROWS, jnp.int32)
    yg = _gather_rows(y3, slot_rows, all_blocks, F32)
    return _ple(x, yg, wts, p, prm["norm_ple"][layer][None, :],
                prm["ple_gate_down"][layer].astype(BF16), prm["ple_gate_up"][layer].astype(BF16),
                prm["ple_w_in"][layer].astype(BF16), prm["norm_final"][None, :], final, head_rows)


def kernel(x_prompt, x_sample, state_rwkv_shift, state_rwkv_wkv, state_pool, p_prompt, p_sample,
           rwkv_mu, rwkv_w_rkv, rwkv_w_o, rwkv_w0, rwkv_w1, rwkv_w2, rwkv_a0, rwkv_a1, rwkv_a2,
           rwkv_g1, rwkv_g2, rwkv_k_k, rwkv_k_a, rwkv_r_k, rwkv_lnx_w, rwkv_lnx_b,
           pool_w, pool_scale, norm_mix, norm_ffn, norm_ple, norm_final,
           moe_w_grp, moe_b_grp, moe_w_exp, moe_b_exp, moe_w_gate, moe_w_up, moe_w_down,
           ple_w_in, ple_gate_down, ple_gate_up):
    Bp, Tp, D = x_prompt.shape
    Bs, Ts, _ = x_sample.shape
    depth = norm_mix.shape[0]
    H, hs = rwkv_r_k.shape[1], rwkv_r_k.shape[2]
    W = QUAD * hs
    assert D % W == 0 and Ts <= CHUNK and Tp % CHUNK == 0
    lay = _SeqLayout(Bp, Tp, Bs, Ts)
    Rp = Bp * Tp
    prm = dict(moe_w_grp=moe_w_grp, moe_b_grp=moe_b_grp, moe_w_exp=moe_w_exp, moe_b_exp=moe_b_exp,
               moe_w_gate=moe_w_gate, moe_w_up=moe_w_up, moe_w_down=moe_w_down, norm_ffn=norm_ffn,
               norm_ple=norm_ple, norm_final=norm_final, ple_w_in=ple_w_in, ple_gate_down=ple_gate_down,
               ple_gate_up=ple_gate_up)

    x = jnp.concatenate([x_prompt.reshape(Rp, D), x_sample.reshape(Bs * Ts, D)], axis=0)
    Pd = p_prompt.shape[-1]
    p_all = jnp.concatenate([p_prompt.reshape(depth, Rp, Pd), p_sample.reshape(depth, Bs * Ts, Pd)],
                            axis=1).astype(BF16)

    def slabs(vec):
        return vec.reshape(D // W, 1, W)

    shift_p, wkv_p, pool_p, shift_s, wkv_s, pool_s = [], [], [], [], [], []
    for i in range(depth):
        j = i // 2
        g_mix = norm_mix[i][None, :]
        if i % 2 == 0:
            shift_all = jnp.concatenate([jnp.zeros((Bp, D), F32), state_rwkv_shift[j]], axis=0)[:, None, :]
            mixed, h_last = _norm_mix(x, shift_all, g_mix, rwkv_mu[j], lay)
            xr, xw, xk, xv, xa, xg = mixed
            w_rkv = rwkv_w_rkv[j].astype(BF16)
            r = _mm(xr, w_rkv[0:1], slab_out=W, name="proj_r")
            k = _mm(xk, w_rkv[1:2], slab_out=W, name="proj_k")
            v = _mm(xv, w_rkv[2:3], slab_out=W, name="proj_v")
            lw = _lora(xw, rwkv_w1[j].astype(BF16), rwkv_w2[j].astype(BF16), rwkv_w0[j][None, :], "decay", W)
            a = _lora(xa, rwkv_a1[j].astype(BF16), rwkv_a2[j].astype(BF16), rwkv_a0[j][None, :], "aaa", W)
            gl = rwkv_g1.shape[2]
            glp = -(-gl // LANES) * LANES
            g1 = jnp.pad(rwkv_g1[j], ((0, 0), (0, glp - gl))).astype(BF16)
            g2 = jnp.pad(rwkv_g2[j], ((0, glp - gl), (0, 0))).astype(BF16)
            g = _lora(xg, g1, g2, jnp.zeros((1, D), F32), "gate", W)
            sprm = (slabs(rwkv_k_k[j]), slabs(rwkv_k_a[j]), slabs(rwkv_r_k[j].reshape(D)),
                    slabs(rwkv_lnx_w[j]), slabs(rwkv_lnx_b[j]))
            o_p, sp = _scan(r, k, v, lw, a, g, sprm, None, row0=0, n_seq=Bp, t_len=Tp, hs=hs)
            o_s, ss = _scan(r, k, v, lw, a, g, sprm, _pack_state(state_rwkv_wkv[j], hs),
                            row0=Rp, n_seq=Bs, t_len=Ts, hs=hs)
            o = jnp.concatenate([o_p, o_s], axis=1)
            x = _mm(o, rwkv_w_o[j].astype(BF16)[None], res=x, slab_in=True, name="proj_o")
            shift_p.append(h_last[:Bp, -1])
            shift_s.append(h_last[Bp:, -1])
            wkv_p.append(_unpack_state(sp, hs))
            wkv_s.append(_unpack_state(ss, hs))
        else:
            hist = jnp.concatenate([jnp.zeros((Bp, 16, D), F32),
                                    jnp.pad(state_pool[j], ((0, 0), (1, 0), (0, 0)))], axis=0)
            d, h_last = _pool(x, hist, g_mix, lay)
            x = _mm(d, pool_w[j].astype(BF16), scale=pool_scale[j][None, :], res=x, name="pool_proj")
            nh = state_pool.shape[2]
            pool_p.append(h_last[:Bp, SEQ_TILE - nh:])
            pool_s.append(h_last[Bp:, SEQ_TILE - nh:])
        x = _moe_ple(x, p_all[i], i, prm, final=(i == depth - 1), head_rows=Rp)

    y_prompt = x[0].reshape(Bp, Tp, D)
    y_sample = x[1].reshape(Bs, Ts, D)
    return (y_prompt, y_sample, jnp.stack(shift_p), jnp.stack(wkv_p), jnp.stack(pool_p),
            jnp.stack(shift_s), jnp.stack(wkv_s), jnp.stack(pool_s))
```

```python
import functools

import jax
import jax.numpy as jnp
from jax import lax
from jax.experimental import pallas as pl
from jax.experimental.pallas import tpu as pltpu

F32 = jnp.float32
BF16 = jnp.bfloat16

NORM_EPS = 1e-6
GN_EPS = 64e-5
PAST_LEN = 4096
POOL_WINDOWS = (2, 4, 8, 16)
TOP_K = 2
LANES = 128
SEQ_TILE = 32
CHUNK = 64
QUAD = 4
SCAN_GROUP = 4
MOE_ROWS = 256
GATHER_PITCH = 36
VMEM_LIMIT = 56 * 1024 * 1024


def _cparams(sem):
    return pltpu.CompilerParams(dimension_semantics=sem, vmem_limit_bytes=VMEM_LIMIT)


def _pick(n, cands):
    for c in cands:
        if n % c == 0:
            return c
    return n


def _rms(x, g):
    return x * lax.rsqrt(jnp.mean(x * x, axis=-1, keepdims=True) + NORM_EPS) * g


_NN = (((1,), (0,)), ((), ()))
_NT = (((1,), (1,)), ((), ()))
_TN = (((0,), (0,)), ((), ()))


def _dot(a, b, dims=_NN, prec=None):
    return lax.dot_general(a, b, dims, precision=prec, preferred_element_type=F32)


def _mm_kernel(*refs, has_res, has_scale, a_slabs, o_slabs):
    a_ref, w_ref = refs[0], refs[1]
    k = 2
    if a_slabs:
        a = jnp.concatenate([a_ref[q] for q in range(a_slabs)], axis=1)
    else:
        a = a_ref[...]
    acc = _dot(a, w_ref[...])
    if has_scale:
        acc = acc * refs[k][...]
        k += 1
    if has_res:
        acc = refs[k][...] + acc
        k += 1
    o_ref = refs[k]
    if o_slabs:
        sw = o_ref.shape[2]
        for q in range(o_slabs):
            o_ref[q] = acc[:, q * sw:(q + 1) * sw].astype(o_ref.dtype)
    else:
        o_ref[...] = acc.astype(o_ref.dtype)


def _mm(a, w, *, res=None, scale=None, out_dtype=F32, slab_in=False, slab_out=0, name="mm"):
    M = a.shape[1] if slab_in else a.shape[0]
    G, Kg, Ng = w.shape
    tm = _pick(M, (512, 256, 128, 64, 32, 16, 8))
    tn = _pick(Ng, (1024, 512, 256, 128))
    nn = Ng // tn
    if slab_in:
        assert G == 1
        a_spec = pl.BlockSpec((a.shape[0], tm, a.shape[2]), lambda g, n, m: (0, m, 0))
    else:
        a_spec = pl.BlockSpec((tm, Kg), lambda g, n, m: (m, g))
    in_specs = [a_spec, pl.BlockSpec((None, Kg, tn), lambda g, n, m: (g, 0, n))]
    args = [a, w]
    if scale is not None:
        in_specs.append(pl.BlockSpec((1, tn), lambda g, n, m: (0, g * nn + n)))
        args.append(scale)
    if res is not None:
        in_specs.append(pl.BlockSpec((tm, tn), lambda g, n, m: (m, g * nn + n)))
        args.append(res)
    if slab_out:
        assert tn % slab_out == 0
        per = tn // slab_out
        out_spec = pl.BlockSpec((per, tm, slab_out), lambda g, n, m: (g * nn + n, m, 0))
        out_shape = jax.ShapeDtypeStruct((G * Ng // slab_out, M, slab_out), out_dtype)
    else:
        per = 0
        out_spec = pl.BlockSpec((tm, tn), lambda g, n, m: (m, g * nn + n))
        out_shape = jax.ShapeDtypeStruct((M, G * Ng), out_dtype)
    return pl.pallas_call(
        functools.partial(_mm_kernel, has_res=res is not None, has_scale=scale is not None,
                          a_slabs=a.shape[0] if slab_in else 0, o_slabs=per),
        grid=(G, nn, M // tm),
        in_specs=in_specs,
        out_specs=out_spec,
        out_shape=out_shape,
        compiler_params=_cparams(("arbitrary", "arbitrary", "arbitrary")),
        name=name,
    )(*args)


def _lora_kernel(a_ref, w1_ref, w2_ref, b_ref, o_ref, *, mode):
    t = _dot(a_ref[...], w1_ref[...])
    if mode == "decay":
        t = jnp.tanh(t)
    elif mode == "gate":
        t = jax.nn.sigmoid(t)
    z = _dot(t.astype(BF16), w2_ref[...]) + b_ref[...]
    if mode == "decay":
        z = -jnp.exp(-jax.nn.softplus(-z) - 0.5)
    elif mode == "aaa":
        z = jax.nn.sigmoid(z)
    sw = o_ref.shape[2]
    for q in range(o_ref.shape[0]):
        o_ref[q] = z[:, q * sw:(q + 1) * sw]


def _lora(a, w1, w2, bias, mode, sw):
    M, K = a.shape
    L = w1.shape[1]
    D = w2.shape[1]
    tm = _pick(M, (256, 128, 64, 32, 16, 8))
    return pl.pallas_call(
        functools.partial(_lora_kernel, mode=mode),
        grid=(M // tm,),
        in_specs=[
            pl.BlockSpec((tm, K), lambda m: (m, 0)),
            pl.BlockSpec((K, L), lambda m: (0, 0)),
            pl.BlockSpec((L, D), lambda m: (0, 0)),
            pl.BlockSpec((1, D), lambda m: (0, 0)),
        ],
        out_specs=pl.BlockSpec((D // sw, tm, sw), lambda m: (0, m, 0)),
        out_shape=jax.ShapeDtypeStruct((D // sw, M, sw), F32),
        compiler_params=_cparams(("arbitrary",)),
        name="lora_" + mode,
    )(a, w1, w2, bias)


class _SeqLayout:
    def __init__(self, bp, tp, bs, ts):
        assert tp % SEQ_TILE == 0 and ts % SEQ_TILE == 0
        self.bp, self.tp, self.bs, self.ts = bp, tp, bs, ts
        self.tiles_p = tp // SEQ_TILE
        self.tiles_s = ts // SEQ_TILE
        self.n_tiles_p = bp * self.tiles_p
        self.n_tiles = self.n_tiles_p + bs * self.tiles_s
        self.n_seq = bp + bs
        self.rows = bp * tp + bs * ts

    def seq_of_tile(self, i):
        return jnp.where(i < self.n_tiles_p, i // self.tiles_p,
                         self.bp + (i - self.n_tiles_p) // self.tiles_s)

    def tile_in_seq(self, i):
        return jnp.where(i < self.n_tiles_p, i % self.tiles_p, (i - self.n_tiles_p) % self.tiles_s)


def _norm_mix_kernel(x_ref, xp_ref, sh_ref, g_ref, mu_ref, *outs, lay):
    i = pl.program_id(0)
    g = g_ref[...]
    h = _rms(x_ref[...], g)
    h_before = _rms(xp_ref[7:8, :], g)
    first = lay.tile_in_seq(i) == 0
    prev_row = jnp.where(first, sh_ref[...], h_before)
    rows = lax.broadcasted_iota(jnp.int32, h.shape, 0)
    h_prev = jnp.where(rows == 0, prev_row, pltpu.roll(h, 1, axis=0))
    xx = h_prev - h
    for n in range(6):
        outs[n][...] = (h + xx * mu_ref[n:n + 1, :]).astype(BF16)
    outs[6][...] = h


def _norm_mix(x, shift_all, g, mu, lay):
    R, D = x.shape
    tpb = SEQ_TILE // 8
    row_spec = pl.BlockSpec((SEQ_TILE, D), lambda i: (i, 0))
    outs = pl.pallas_call(
        functools.partial(_norm_mix_kernel, lay=lay),
        grid=(lay.n_tiles,),
        in_specs=[
            row_spec,
            pl.BlockSpec((8, D), lambda i: (jnp.maximum(i * tpb - 1, 0), 0)),
            pl.BlockSpec((None, 1, D), lambda i: (lay.seq_of_tile(i), 0, 0)),
            pl.BlockSpec((1, D), lambda i: (0, 0)),
            pl.BlockSpec((6, D), lambda i: (0, 0)),
        ],
        out_specs=[row_spec] * 6 + [pl.BlockSpec((None, SEQ_TILE, D), lambda i: (lay.seq_of_tile(i), 0, 0))],
        out_shape=[jax.ShapeDtypeStruct((R, D), BF16)] * 6
        + [jax.ShapeDtypeStruct((lay.n_seq, SEQ_TILE, D), F32)],
        compiler_params=_cparams(("arbitrary",)),
        name="norm_mix",
    )(x, x, shift_all, g, mu)
    return outs[:6], outs[6]


def _scan_kernel(r_ref, k_ref, v_ref, lw_ref, a_ref, g_ref, kk_ref, ka_ref, rk_ref, lnw_ref, lnb_ref,
                 *rest, t_blk, hs, group, n_chunks):
    s0_ref = rest[0] if len(rest) == 3 else None
    o_ref, s_ref = rest[-2:]
    C = CHUNK
    nq, _, W = r_ref.shape
    c = pl.program_id(1)

    @pl.when(c == 0)
    def _():
        s_ref[...] = jnp.zeros(s_ref.shape, F32) if s0_ref is None else s0_ref[...]

    ri = lax.broadcasted_iota(jnp.int32, (W, W), 0)
    ci = lax.broadcasted_iota(jnp.int32, (W, W), 1)
    head_blk = (ri // hs) == (ci // hs)
    ones_blk = jnp.where(head_blk, 1.0, 0.0).astype(F32)
    ti = lax.broadcasted_iota(jnp.int32, (C, C), 0)
    si = lax.broadcasted_iota(jnp.int32, (C, C), 1)
    tri_incl = jnp.where(ti >= si, 1.0, 0.0).astype(F32)
    lane_head = lax.broadcasted_iota(jnp.int32, (C, W), 1) // hs
    rj = lax.broadcasted_iota(jnp.int32, (QUAD * C, QUAD * C), 0)
    cj = lax.broadcasted_iota(jnp.int32, (QUAD * C, QUAD * C), 1)
    t_in = rj % C
    s_in = cj % C
    strict = jnp.where(t_in > s_in, 1.0, 0.0).astype(F32)
    incl = jnp.where(t_in >= s_in, 1.0, 0.0).astype(F32)
    eye = jnp.where(rj == cj, 1.0, 0.0).astype(F32)

    ones_bf = ones_blk.astype(BF16)
    head_masks = [jnp.where(lane_head == h, 1.0, 0.0).astype(BF16) for h in range(QUAD)]

    def fold(x):
        y = x[0:C]
        for h in range(1, QUAD):
            y = y + x[h * C:(h + 1) * C]
        return y

    def group_body(it, carry, *, passes, group):
        qs = [it * group + j for j in range(group)]
        each = lambda f, *xs: [f(*t) for t in zip(*xs)]

        full = passes == 6

        def parts(x):
            if full:
                return (x,)
            hi = x.astype(BF16)
            if passes == 1:
                return (hi,)
            return (hi, (x - hi.astype(F32)).astype(BF16))

        def bdot(x, y, dims=_NN):
            if full:
                return _dot(x[0], y[0], dims, lax.Precision.HIGHEST)
            out = _dot(x[0], y[0], dims)
            if passes > 1:
                out = out + (_dot(x[0], y[1], dims) + _dot(x[1], y[0], dims))
            return out

        def head_sum(x):
            if full:
                return _dot(x, ones_blk, _NN, lax.Precision.HIGHEST)
            out = _dot(parts(x)[0], ones_bf)
            if passes > 1:
                out = out + _dot(parts(x)[1], ones_bf)
            return out

        def stack(x):
            return tuple(jnp.concatenate([xb * m for m in head_masks], axis=0) for xb in parts(x))

        def load(ref):
            def one(q):
                x = ref[q]
                if t_blk < C:
                    x = jnp.concatenate([x, jnp.zeros((C - t_blk, W), F32)], axis=0)
                return x
            return [one(q) for q in qs]

        r, k, v, lw, a = load(r_ref), load(k_ref), load(v_ref), load(lw_ref), load(a_ref)
        kk = each(lambda k_, q: k_ * kk_ref[q], k, qs)
        kk = each(lambda x: x * lax.rsqrt(jnp.maximum(head_sum(x * x), 1e-24)), kk)
        kf = each(lambda k_, a_, q: k_ * (1.0 + (a_ - 1.0) * ka_ref[q]), k, a, qs)
        bv = each(lambda x, a_: x * a_, kk, a)

        cum = each(lambda x: _dot(tri_incl, x, _NN, lax.Precision.HIGHEST), lw)
        e_neg = each(lambda x: jnp.exp(-x), cum)
        rt = each(lambda r_, x: r_ * jnp.exp(x), r, cum)
        at = each(lambda x, c_, l_: -x * jnp.exp(c_ - l_), kk, cum, lw)
        kt = each(lambda x, e: x * e, kf, e_neg)
        bt = each(lambda x, e: x * e, bv, e_neg)
        p_c = each(lambda x: jnp.exp(x[C - 1:C, :]), cum)

        a_st, r_st, b_st, k_st, v_st = (each(stack, x) for x in (at, rt, bt, kt, v))
        a_ab = each(lambda x, y: bdot(x, y, _NT) * strict, a_st, b_st)
        a_ak = each(lambda x, y: parts(bdot(x, y, _NT) * strict), a_st, k_st)
        a_rb = each(lambda x, y: parts(bdot(x, y, _NT) * incl), r_st, b_st)
        a_rk = each(lambda x, y: parts(bdot(x, y, _NT) * incl), r_st, k_st)

        t_inv = each(lambda x: eye + x, a_ab)
        n_pow = each(parts, a_ab)
        p = 1
        while 2 * p < C:
            n_pow = each(lambda x: parts(bdot(x, x)), n_pow)
            t_inv = each(lambda t, n: t + bdot(parts(t), n), t_inv, n_pow)
            p *= 2
        t_bf = each(parts, t_inv)

        s = [s_ref[q] for q in qs]
        s_bf = each(parts, s)
        u_in = each(lambda a_, s_, ak, v_: parts(bdot(a_, s_, _NT) + bdot(ak, v_)), a_st, s_bf, a_ak, v_st)
        u_st = each(bdot, t_bf, u_in)
        o_st = each(lambda r_, s_, rb, u_, rk, v_: bdot(r_, s_, _NT) + bdot(rb, parts(u_)) + bdot(rk, v_),
                    r_st, s_bf, a_rb, u_st, a_rk, v_st)
        o = each(fold, o_st)
        u = each(fold, u_st)
        upd = each(lambda u_, v_, b_, k_, pc: bdot(parts(jnp.concatenate([u_, v_], axis=0)),
                                                   parts(jnp.concatenate([b_ * pc, k_ * pc], axis=0)), _TN),
                   u, v, bt, kt, p_c)
        for q, s_, pc, up in zip(qs, s, p_c, upd):
            s_ref[q] = s_ * pc + jnp.where(head_blk, up, 0.0)

        inv_n = 1.0 / hs
        o_c = each(lambda x: x - head_sum(x) * inv_n, o)
        o_n = each(lambda x: x * lax.rsqrt(head_sum(x * x) * inv_n + GN_EPS), o_c)
        bonus = each(lambda r_, k_, q: head_sum(r_ * k_ * rk_ref[q]), r, kf, qs)
        for q, x, bo, v_ in zip(qs, o_n, bonus, v):
            y = x * lnw_ref[q] + lnb_ref[q] + bo * v_
            o_ref[q] = (y[:t_blk] * g_ref[q]).astype(o_ref.dtype)
        return carry

    def run(passes, grp):
        lax.fori_loop(0, nq // grp, functools.partial(group_body, passes=passes, group=grp), 0)

    if n_chunks > 1:
        last = n_chunks - 1

        @pl.when(c < last)
        def _():
            run(1, group)

        @pl.when(c == last)
        def _():
            run(6, _pick(nq, (2, 1)))
    else:
        run(1, group)


def _scan(r, k, v, lw, a, g, prm, s0, *, row0, n_seq, t_len, hs):
    nq, _, W = r.shape
    t_blk = min(CHUNK, t_len)
    n_chunks = t_len // t_blk
    blk0 = row0 // t_blk
    row_spec = pl.BlockSpec((nq, t_blk, W), lambda b, c: (0, blk0 + b * n_chunks + c, 0))
    prm_spec = pl.BlockSpec((nq, 1, W), lambda b, c: (0, 0, 0))
    st_spec = pl.BlockSpec((None, nq, W, W), lambda b, c: (b, 0, 0, 0))
    return pl.pallas_call(
        functools.partial(_scan_kernel, t_blk=t_blk, hs=hs, group=_pick(nq, (SCAN_GROUP, 2, 1)),
                          n_chunks=n_chunks),
        grid=(n_seq, n_chunks),
        in_specs=[row_spec] * 6 + [prm_spec] * 5 + ([] if s0 is None else [st_spec]),
        out_specs=[pl.BlockSpec((nq, t_blk, W), lambda b, c: (0, b * n_chunks + c, 0)), st_spec],
        out_shape=[jax.ShapeDtypeStruct((nq, n_seq * t_len, W), BF16),
                   jax.ShapeDtypeStruct((n_seq, nq, W, W), F32)],
        compiler_params=_cparams(("arbitrary", "arbitrary")),
        name="rwkv_scan",
    )(r, k, v, lw, a, g, *prm, *(() if s0 is None else (s0,)))


def _pack_state(s, hs):
    n, H = s.shape[0], s.shape[1]
    s = s.reshape(n, H // QUAD, QUAD, hs, hs)
    eye = jnp.eye(QUAD, dtype=s.dtype)
    bd = s[:, :, :, :, None, :] * eye[None, None, :, None, :, None]
    return bd.reshape(n, H // QUAD, QUAD * hs, QUAD * hs)


def _unpack_state(bd, hs):
    n, nq = bd.shape[0], bd.shape[1]
    x = bd.reshape(n, nq, QUAD, hs, QUAD, hs)
    x = jnp.stack([x[:, :, j, :, j, :] for j in range(QUAD)], axis=2)
    return x.reshape(n, nq * QUAD, hs, hs)


def _pool_kernel(x_ref, xp_ref, hist_ref, g_ref, d_ref, hl_ref, *, lay):
    i = pl.program_id(0)
    g = g_ref[...]
    h = _rms(x_ref[...], g)
    tis = lay.tile_in_seq(i)
    prev = jnp.where(tis == 0, hist_ref[...], _rms(xp_ref[...], g))
    ext = jnp.concatenate([prev, h], axis=0)
    D = h.shape[1]
    gw = D // len(POOL_WINDOWS)
    pos0 = jnp.where(i < lay.n_tiles_p, 0, PAST_LEN) + tis * SEQ_TILE
    pos = pos0 + lax.broadcasted_iota(jnp.int32, (SEQ_TILE, 1), 0)
    for gi, w in enumerate(POOL_WINDOWS):
        e = ext[:, gi * gw:(gi + 1) * gw]
        s = e
        step = 1
        while step < w:
            s = s + pltpu.roll(s, step, axis=0)
            step *= 2
        cnt = jnp.minimum(pos + 1, w).astype(F32)
        mean = s[16:] / cnt
        d_ref[:, gi * gw:(gi + 1) * gw] = (mean - h[:, gi * gw:(gi + 1) * gw]).astype(BF16)
    hl_ref[...] = h


def _pool(x, hist16, g, lay):
    R, D = x.shape
    tp16 = SEQ_TILE // 16
    return pl.pallas_call(
        functools.partial(_pool_kernel, lay=lay),
        grid=(lay.n_tiles,),
        in_specs=[
            pl.BlockSpec((SEQ_TILE, D), lambda i: (i, 0)),
            pl.BlockSpec((16, D), lambda i: (jnp.maximum(i * tp16 - 1, 0), 0)),
            pl.BlockSpec((None, 16, D), lambda i: (lay.seq_of_tile(i), 0, 0)),
            pl.BlockSpec((1, D), lambda i: (0, 0)),
        ],
        out_specs=[pl.BlockSpec((SEQ_TILE, D), lambda i: (i, 0)),
                   pl.BlockSpec((None, SEQ_TILE, D), lambda i: (lay.seq_of_tile(i), 0, 0))],
        out_shape=[jax.ShapeDtypeStruct((R, D), BF16),
                   jax.ShapeDtypeStruct((lay.n_seq, SEQ_TILE, D), F32)],
        compiler_params=_cparams(("arbitrary",)),
        name="pool_mix",
    )(x, x, hist16, g)


def _router_kernel(x_ref, g_ref, w_ref, b_ref, hn_ref, wts_ref, eid_ref, cnt_ref, *, n_grp, per_grp):
    hn = _rms(x_ref[...], g_ref[...])
    hn_ref[...] = hn
    logits = _dot(hn.astype(BF16), w_ref[...]) + b_ref[...]
    lane = lax.broadcasted_iota(jnp.int32, logits.shape, 1).astype(F32)
    neg = -jnp.inf
    big = float(logits.shape[1])
    gl = jnp.where(lane < n_grp, logits, neg)
    gmax = jnp.max(gl, axis=-1, keepdims=True)
    gsum = jnp.sum(jnp.exp(gl - gmax), axis=-1, keepdims=True)
    g_idx = jnp.min(jnp.where(gl == gmax, lane, big), axis=-1, keepdims=True)
    g_gate = 1.0 / gsum
    lo = n_grp + g_idx * per_grp
    el = jnp.where((lane >= lo) & (lane < lo + per_grp), logits, neg)
    e1 = jnp.max(el, axis=-1, keepdims=True)
    i1 = jnp.min(jnp.where(el == e1, lane, big), axis=-1, keepdims=True)
    el2 = jnp.where(lane == i1, neg, el)
    e2 = jnp.max(el2, axis=-1, keepdims=True)
    i2 = jnp.min(jnp.where(el2 == e2, lane, big), axis=-1, keepdims=True)
    esum = jnp.sum(jnp.exp(el - e1), axis=-1, keepdims=True)
    p1 = 1.0 / esum
    p2 = jnp.exp(e2 - e1) / esum
    w1 = g_gate * p1 / (p1 + p2)
    w2 = g_gate * p2 / (p1 + p2)
    wts_ref[...] = jnp.where(lane == 0, w1, jnp.where(lane == 1, w2, 0.0))

    @pl.when(pl.program_id(0) == 0)
    def _():
        cnt_ref[...] = jnp.zeros(cnt_ref.shape, F32)

    e1 = i1 - n_grp
    e2 = i2 - n_grp
    oh1 = jnp.where(lane == e1, 1.0, 0.0)
    oh2 = jnp.where(lane == e2, 1.0, 0.0)
    both = oh1 + oh2
    tm = both.shape[0]
    before = (lax.broadcasted_iota(jnp.int32, (tm, tm), 1) < lax.broadcasted_iota(jnp.int32, (tm, tm), 0))
    seen = _dot(jnp.where(before, 1.0, 0.0).astype(BF16), both.astype(BF16)) + cnt_ref[...]
    rank1 = jnp.sum(oh1 * seen, axis=-1, keepdims=True)
    rank2 = jnp.sum(oh2 * seen, axis=-1, keepdims=True)
    cnt_ref[...] += jnp.sum(both, axis=0, keepdims=True)
    eid_ref[...] = jnp.where(lane == 0, e1, jnp.where(lane == 1, e2, jnp.where(
        lane == 2, rank1, jnp.where(lane == 3, rank2, 0.0)))).astype(jnp.int32)


def _router(x, g, w_cat, b_cat, n_grp, per_grp):
    R, D = x.shape
    tm = _pick(R, (256, 128, 64, 32, 16, 8))
    L = w_cat.shape[1]
    return pl.pallas_call(
        functools.partial(_router_kernel, n_grp=n_grp, per_grp=per_grp),
        grid=(R // tm,),
        in_specs=[
            pl.BlockSpec((tm, D), lambda m: (m, 0)),
            pl.BlockSpec((1, D), lambda m: (0, 0)),
            pl.BlockSpec((D, L), lambda m: (0, 0)),
            pl.BlockSpec((1, L), lambda m: (0, 0)),
        ],
        out_specs=[pl.BlockSpec((tm, D), lambda m: (m, 0)),
                   pl.BlockSpec((tm, L), lambda m: (m, 0)),
                   pl.BlockSpec((tm, L), lambda m: (m, 0)),
                   pl.BlockSpec((1, L), lambda m: (0, 0))],
        out_shape=[jax.ShapeDtypeStruct((R, D), F32),
                   jax.ShapeDtypeStruct((R, L), F32),
                   jax.ShapeDtypeStruct((R, L), jnp.int32),
                   jax.ShapeDtypeStruct((1, L), F32)],
        compiler_params=_cparams(("arbitrary",)),
        name="moe_router",
    )(x, g, w_cat, b_cat)


def _gather_kernel(idx_ref, nblk_ref, src_ref, o_ref, buf0, buf1, sem):
    G = o_ref.shape[0]
    nc = src_ref.shape[1]
    b = pl.program_id(0)
    nblk = nblk_ref[0]
    bufs = (buf0, buf1)

    def row_copy(blk, slot, j):
        return pltpu.make_async_copy(src_ref.at[idx_ref[blk * G + j]],
                                     bufs[slot].at[pl.ds(j * GATHER_PITCH, nc)], sem.at[slot])

    def fetch(blk, slot):
        def issue(j, carry):
            row_copy(blk, slot, 2 * j).start(priority=0)
            row_copy(blk, slot, 2 * j + 1).start(priority=1)
            return carry
        lax.fori_loop(0, G // 2, issue, 0)

    @pl.when((b == 0) & (nblk > 0))
    def _():
        fetch(0, 0)

    for slot in range(2):
        @pl.when((b < nblk) & (b % 2 == slot))
        def _(slot=slot):
            @pl.when(b + 1 < nblk)
            def _():
                fetch(b + 1, 1 - slot)

            def drain(j, carry):
                row_copy(b, slot, j).wait()
                return carry

            lax.fori_loop(0, G, drain, 0)
            for cc in range(nc):
                o_ref[:, cc * LANES:(cc + 1) * LANES] = (
                    bufs[slot][pl.ds(cc, G, stride=GATHER_PITCH), :].astype(o_ref.dtype))

    @pl.when(b >= nblk_ref[0])
    def _():
        o_ref[...] = jnp.zeros(o_ref.shape, o_ref.dtype)


def _gather_rows(src3, idx, nblk, out_dtype):
    n = idx.shape[0]
    _, nc, lanes = src3.shape
    G = MOE_ROWS
    assert n % G == 0 and lanes == LANES
    return pl.pallas_call(
        _gather_kernel,
        grid_spec=pltpu.PrefetchScalarGridSpec(
            num_scalar_prefetch=2,
            grid=(n // G,),
            in_specs=[pl.BlockSpec(memory_space=pl.ANY)],
            out_specs=pl.BlockSpec((G, nc * LANES), lambda b, idx_r, nb_r: (b, 0)),
            scratch_shapes=[pltpu.VMEM((G * GATHER_PITCH, LANES), src3.dtype),
                            pltpu.VMEM((G * GATHER_PITCH, LANES), src3.dtype),
                            pltpu.SemaphoreType.DMA((2,))],
        ),
        out_shape=jax.ShapeDtypeStruct((n, nc * LANES), out_dtype),
        compiler_params=_cparams(("arbitrary",)),
        name="row_gather",
    )(idx, nblk, src3)


def _zero_tail(first_ref, zero_src, dst_block, sem, n_blocks):
    def body(t, carry):
        cp = pltpu.make_async_copy(zero_src, dst_block(t), sem)
        cp.start()
        cp.wait()
        return carry
    lax.fori_loop(first_ref[0], n_blocks, body, 0)


def _moe_up_kernel(start_ref, cnt_ref, tot_ref, x_hbm, wg_ref, wu_ref, h_hbm, wg_bf, wu_bf, xbuf, hbuf,
                   sem_in, sem_out):
    j, e = pl.program_id(0), pl.program_id(1)
    n, b0 = cnt_ref[e], start_ref[e]
    rows = xbuf.shape[1]

    def x_copy(i, slot):
        return pltpu.make_async_copy(x_hbm.at[pl.ds((b0 + i) * rows, rows)], xbuf.at[slot], sem_in.at[slot])

    def h_copy(blk, slot):
        return pltpu.make_async_copy(hbuf.at[slot], h_hbm.at[j, pl.ds(blk * rows, rows)], sem_out.at[slot])

    @pl.when(n > 0)
    def _():
        x_copy(0, 0).start()
        wg_bf[...] = wg_ref[...].astype(BF16)
        wu_bf[...] = wu_ref[...].astype(BF16)

        def body(i, carry):
            slot = i % 2

            @pl.when(i + 1 < n)
            def _():
                x_copy(i + 1, 1 - slot).start()

            x_copy(i, slot).wait()
            x = xbuf[slot]
            h = jax.nn.silu(_dot(x, wg_bf[...])) * _dot(x, wu_bf[...])

            @pl.when(i >= 2)
            def _():
                h_copy(b0 + i - 2, slot).wait()

            hbuf[slot] = h.astype(BF16)
            h_copy(b0 + i, slot).start()
            return carry

        lax.fori_loop(0, n, body, 0)

        @pl.when(n >= 2)
        def _():
            h_copy(b0 + n - 2, n % 2).wait()

        h_copy(b0 + n - 1, (n - 1) % 2).wait()

    @pl.when(e == pl.num_programs(1) - 1)
    def _():
        hbuf[0] = jnp.zeros(hbuf.shape[1:], BF16)
        _zero_tail(tot_ref, hbuf.at[0], lambda t: h_hbm.at[j, pl.ds(t * rows, rows)], sem_out.at[0],
                   h_hbm.shape[1] // rows)


def _moe_up(xg, w_gate, w_up, layer, start_blk, cnt_blk, tot_blk):
    P, D = xg.shape
    n_exp, De = w_gate.shape[1], w_gate.shape[3]
    tj = _pick(De, (512, 256, 128))
    w_spec = pl.BlockSpec((None, None, D, tj), lambda j, e, *_: (layer, e, 0, j))
    return pl.pallas_call(
        _moe_up_kernel,
        grid_spec=pltpu.PrefetchScalarGridSpec(
            num_scalar_prefetch=3,
            grid=(De // tj, n_exp),
            in_specs=[pl.BlockSpec(memory_space=pl.ANY), w_spec, w_spec],
            out_specs=pl.BlockSpec(memory_space=pl.ANY),
            scratch_shapes=[pltpu.VMEM((D, tj), BF16), pltpu.VMEM((D, tj), BF16),
                            pltpu.VMEM((2, MOE_ROWS, D), BF16), pltpu.VMEM((2, MOE_ROWS, tj), BF16),
                            pltpu.SemaphoreType.DMA((2,)), pltpu.SemaphoreType.DMA((2,))],
        ),
        out_shape=jax.ShapeDtypeStruct((De // tj, P, tj), BF16),
        compiler_params=_cparams(("arbitrary", "arbitrary")),
        name="moe_up",
    )(start_blk, cnt_blk, tot_blk, xg, w_gate, w_up)


def _moe_down_kernel(start_ref, cnt_ref, tot_ref, h_hbm, wd_ref, y_hbm, wd_bf, hbuf, ybuf0, ybuf1,
                     sem_in, sem_out):
    nh, e = pl.program_id(0), pl.program_id(1)
    n, b0 = cnt_ref[e], start_ref[e]
    nj, rows = hbuf.shape[1], hbuf.shape[2]
    ncol = ybuf0.shape[1]
    ybufs = (ybuf0, ybuf1)

    def h_copy(i, slot):
        return pltpu.make_async_copy(h_hbm.at[:, pl.ds((b0 + i) * rows, rows)], hbuf.at[slot], sem_in.at[slot])

    def y_copy(blk, slot):
        return pltpu.make_async_copy(ybufs[slot], y_hbm.at[pl.ds(blk * rows, rows), pl.ds(nh * ncol, ncol)],
                                     sem_out.at[slot])

    @pl.when(n > 0)
    def _():
        h_copy(0, 0).start()
        wd_bf[...] = wd_ref[...].astype(BF16)

        def body(i, carry):
            @pl.when(i + 1 < n)
            def _():
                h_copy(i + 1, (i + 1) % 2).start()

            for slot in range(2):
                @pl.when(i % 2 == slot)
                def _(slot=slot):
                    h_copy(i, slot).wait()
                    h = jnp.concatenate([hbuf[slot, jj] for jj in range(nj)], axis=1)
                    y = _dot(h, wd_bf[...])

                    @pl.when(i >= 2)
                    def _():
                        y_copy(b0 + i - 2, slot).wait()

                    for cc in range(ncol):
                        ybufs[slot][:, cc, :] = y[:, cc * LANES:(cc + 1) * LANES]
                    y_copy(b0 + i, slot).start()
            return carry

        lax.fori_loop(0, n, body, 0)

        for slot in range(2):
            @pl.when((n >= 2) & (n % 2 == slot))
            def _(slot=slot):
                y_copy(b0 + n - 2, slot).wait()

            @pl.when((n - 1) % 2 == slot)
            def _(slot=slot):
                y_copy(b0 + n - 1, slot).wait()

    @pl.when(e == pl.num_programs(1) - 1)
    def _():
        ybuf0[...] = jnp.zeros(ybuf0.shape, F32)
        _zero_tail(tot_ref, ybuf0, lambda t: y_hbm.at[pl.ds(t * rows, rows), pl.ds(nh * ncol, ncol)],
                   sem_out.at[0], y_hbm.shape[0] // rows)


def _moe_down(h, w_down, layer, start_blk, cnt_blk, tot_blk):
    nj, P, tj = h.shape
    n_exp, De, D = w_down.shape[1], w_down.shape[2], w_down.shape[3]
    assert nj * tj == De
    tn = _pick(D, (2048, 1024, 512, 256, 128))
    ncol = tn // LANES
    return pl.pallas_call(
        _moe_down_kernel,
        grid_spec=pltpu.PrefetchScalarGridSpec(
            num_scalar_prefetch=3,
            grid=(D // tn, n_exp),
            in_specs=[pl.BlockSpec(memory_space=pl.ANY),
                      pl.BlockSpec((None, None, De, tn), lambda nh, e, *_: (layer, e, 0, nh))],
            out_specs=pl.BlockSpec(memory_space=pl.ANY),
            scratch_shapes=[pltpu.VMEM((De, tn), BF16), pltpu.VMEM((2, nj, MOE_ROWS, tj), BF16),
                            pltpu.VMEM((MOE_ROWS, ncol, LANES), F32), pltpu.VMEM((MOE_ROWS, ncol, LANES), F32),
                            pltpu.SemaphoreType.DMA((2,)), pltpu.SemaphoreType.DMA((2,))],
        ),
        out_shape=jax.ShapeDtypeStruct((P, D // LANES, LANES), F32),
        compiler_params=_cparams(("arbitrary", "arbitrary")),
        name="moe_down",
    )(start_blk, cnt_blk, tot_blk, h, w_down)


def _dispatch_plan(eid, rank, counts):
    R = eid.shape[0]
    S = R * TOP_K
    n_experts = counts.shape[0]
    e_flat = eid.reshape(-1)
    padded = (counts + MOE_ROWS - 1) // MOE_ROWS * MOE_ROWS
    pad_end = jnp.cumsum(padded)
    dest = ((pad_end - padded)[e_flat] + rank.reshape(-1)).astype(jnp.int32)
    nb = -(-S // MOE_ROWS) + n_experts
    buf_tok = jnp.zeros((nb * MOE_ROWS,), jnp.int32).at[dest].set(jnp.arange(S, dtype=jnp.int32) // TOP_K)
    start_blk = ((pad_end - padded) // MOE_ROWS).astype(jnp.int32)
    cnt_blk = (padded // MOE_ROWS).astype(jnp.int32)
    tot_blk = (pad_end[-1:] // MOE_ROWS).astype(jnp.int32)
    return dest, buf_tok, start_blk, cnt_blk, tot_blk


def _ple_kernel(x_ref, y0_ref, y1_ref, wts_ref, p_ref, g_ref, gd_ref, gu_ref, win_ref, gf_ref, *o_refs,
                final, head_tiles):
    wts = wts_ref[...]
    x = x_ref[...] + y0_ref[...] * wts[:, 0:1] + y1_ref[...] * wts[:, 1:2]
    hn = _rms(x, g_ref[...]).astype(BF16)
    t = _dot(hn, gd_ref[...]).astype(BF16)
    gate = jax.nn.sigmoid(_dot(t, gu_ref[...]))
    pe = _dot(p_ref[...], win_ref[...])
    x = x + pe * gate
    if not final:
        o_refs[0][...] = x
        return
    x = _rms(x, gf_ref[...])
    m = pl.program_id(0)

    @pl.when(m < head_tiles)
    def _():
        o_refs[0][...] = x

    @pl.when(m >= head_tiles)
    def _():
        o_refs[1][...] = x


def _ple(x, yg, wts, p, g, gd, gu, win, gf, final, head_rows):
    R, D = x.shape
    Pd = p.shape[1]
    L = wts.shape[1]
    tm = _pick(R, (128, 64, 32, 16, 8))
    nt = R // tm
    row = pl.BlockSpec((tm, D), lambda m: (m, 0))
    assert head_rows % tm == 0
    ht = head_rows // tm
    if final:
        out_specs = [pl.BlockSpec((tm, D), lambda m: (jnp.minimum(m, ht - 1), 0)),
                     pl.BlockSpec((tm, D), lambda m: (jnp.maximum(m - ht, 0), 0))]
        out_shape = [jax.ShapeDtypeStruct((head_rows, D), F32), jax.ShapeDtypeStruct((R - head_rows, D), F32)]
    else:
        out_specs = row
        out_shape = jax.ShapeDtypeStruct((R, D), F32)
    return pl.pallas_call(
        functools.partial(_ple_kernel, final=final, head_tiles=ht),
        grid=(nt,),
        in_specs=[
            row,
            pl.BlockSpec((tm, D), lambda m: (m, 0)),
            pl.BlockSpec((tm, D), lambda m: (nt + m, 0)),
            pl.BlockSpec((tm, L), lambda m: (m, 0)),
            pl.BlockSpec((tm, Pd), lambda m: (m, 0)),
            pl.BlockSpec((1, D), lambda m: (0, 0)),
            pl.BlockSpec((D, Pd), lambda m: (0, 0)),
            pl.BlockSpec((Pd, D), lambda m: (0, 0)),
            pl.BlockSpec((Pd, D), lambda m: (0, 0)),
            pl.BlockSpec((1, D), lambda m: (0, 0)),
        ],
        out_specs=out_specs,
        out_shape=out_shape,
        compiler_params=_cparams(("arbitrary",)),
        name="moe_combine_ple",
    )(x, yg, yg, wts, p, g, gd, gu, win, gf)


def _moe_ple(x, p, layer, prm, final, head_rows):
    R, D = x.shape
    assert (R * TOP_K) % MOE_ROWS == 0 and D % LANES == 0
    n_grp = prm["moe_w_grp"].shape[2]
    n_exp = prm["moe_w_exp"].shape[2]
    L = LANES
    w_cat = jnp.concatenate([prm["moe_w_grp"][layer], prm["moe_w_exp"][layer],
                             jnp.zeros((D, L - n_grp - n_exp), F32)], axis=1).astype(BF16)
    b_cat = jnp.concatenate([prm["moe_b_grp"][layer], prm["moe_b_exp"][layer],
                             jnp.zeros((L - n_grp - n_exp,), F32)])[None, :]
    hn, wts, route, cnt = _router(x, prm["norm_ffn"][layer][None, :], w_cat, b_cat, n_grp, n_exp // n_grp)
    dest, buf_tok, start_blk, cnt_blk, tot_blk = _dispatch_plan(
        route[:, :TOP_K], route[:, TOP_K:2 * TOP_K], cnt[0, :n_exp].astype(jnp.int32))
    xg = _gather_rows(hn.reshape(R, D // LANES, LANES), buf_tok, tot_blk, BF16)
    hmid = _moe_up(xg, prm["moe_w_gate"], prm["moe_w_up"], layer, start_blk, cnt_blk, tot_blk)
    y3 = _moe_down(hmid, prm["moe_w_down"], layer, start_blk, cnt_blk, tot_blk)
    slot_rows = dest.reshape(R, TOP_K).T.reshape(-1)
    all_blocks = jnp.full((1,), R * TOP_K // MOE_ROWS, jnp.int32)
    yg = _gather_rows(y3, slot_rows, all_blocks, F32)
    return _ple(x, yg, wts, p, prm["norm_ple"][layer][None, :],
                prm["ple_gate_down"][layer].astype(BF16), prm["ple_gate_up"][layer].astype(BF16),
                prm["ple_w_in"][layer].astype(BF16), prm["norm_final"][None, :], final, head_rows)


def kernel(x_prompt, x_sample, state_rwkv_shift, state_rwkv_wkv, state_pool, p_prompt, p_sample,
           rwkv_mu, rwkv_w_rkv, rwkv_w_o, rwkv_w0, rwkv_w1, rwkv_w2, rwkv_a0, rwkv_a1, rwkv_a2,
           rwkv_g1, rwkv_g2, rwkv_k_k, rwkv_k_a, rwkv_r_k, rwkv_lnx_w, rwkv_lnx_b,
           pool_w, pool_scale, norm_mix, norm_ffn, norm_ple, norm_final,
           moe_w_grp, moe_b_grp, moe_w_exp, moe_b_exp, moe_w_gate, moe_w_up, moe_w_down,
           ple_w_in, ple_gate_down, ple_gate_up):
    Bp, Tp, D = x_prompt.shape
    Bs, Ts, _ = x_sample.shape
    depth = norm_mix.shape[0]
    H, hs = rwkv_r_k.shape[1], rwkv_r_k.shape[2]
    W = QUAD * hs
    assert D % W == 0 and Ts <= CHUNK and Tp % CHUNK == 0
    lay = _SeqLayout(Bp, Tp, Bs, Ts)
    Rp = Bp * Tp
    prm = dict(moe_w_grp=moe_w_grp, moe_b_grp=moe_b_grp, moe_w_exp=moe_w_exp, moe_b_exp=moe_b_exp,
               moe_w_gate=moe_w_gate, moe_w_up=moe_w_up, moe_w_down=moe_w_down, norm_ffn=norm_ffn,
               norm_ple=norm_ple, norm_final=norm_final, ple_w_in=ple_w_in, ple_gate_down=ple_gate_down,
               ple_gate_up=ple_gate_up)

    x = jnp.concatenate([x_prompt.reshape(Rp, D), x_sample.reshape(Bs * Ts, D)], axis=0)
    Pd = p_prompt.shape[-1]
    p_all = jnp.concatenate([p_prompt.reshape(depth, Rp, Pd), p_sample.reshape(depth, Bs * Ts, Pd)],
                            axis=1).astype(BF16)

    def slabs(vec):
        return vec.reshape(D // W, 1, W)

    shift_p, wkv_p, pool_p, shift_s, wkv_s, pool_s = [], [], [], [], [], []
    for i in range(depth):
        j = i // 2
        g_mix = norm_mix[i][None, :]
        if i % 2 == 0:
            shift_all = jnp.concatenate([jnp.zeros((Bp, D), F32), state_rwkv_shift[j]], axis=0)[:, None, :]
            mixed, h_last = _norm_mix(x, shift_all, g_mix, rwkv_mu[j], lay)
            xr, xw, xk, xv, xa, xg = mixed
            w_rkv = rwkv_w_rkv[j].astype(BF16)
            r = _mm(xr, w_rkv[0:1], slab_out=W, name="proj_r")
            k = _mm(xk, w_rkv[1:2], slab_out=W, name="proj_k")
            v = _mm(xv, w_rkv[2:3], slab_out=W, name="proj_v")
            lw = _lora(xw, rwkv_w1[j].astype(BF16), rwkv_w2[j].astype(BF16), rwkv_w0[j][None, :], "decay", W)
            a = _lora(xa, rwkv_a1[j].astype(BF16), rwkv_a2[j].astype(BF16), rwkv_a0[j][None, :], "aaa", W)
            gl = rwkv_g1.shape[2]
            glp = -(-gl // LANES) * LANES
            g1 = jnp.pad(rwkv_g1[j], ((0, 0), (0, glp - gl))).astype(BF16)
            g2 = jnp.pad(rwkv_g2[j], ((0, glp - gl), (0, 0))).astype(BF16)
            g = _lora(xg, g1, g2, jnp.zeros((1, D), F32), "gate", W)
            sprm = (slabs(rwkv_k_k[j]), slabs(rwkv_k_a[j]), slabs(rwkv_r_k[j].reshape(D)),
                    slabs(rwkv_lnx_w[j]), slabs(rwkv_lnx_b[j]))
            o_p, sp = _scan(r, k, v, lw, a, g, sprm, None, row0=0, n_seq=Bp, t_len=Tp, hs=hs)
            o_s, ss = _scan(r, k, v, lw, a, g, sprm, _pack_state(state_rwkv_wkv[j], hs),
                            row0=Rp, n_seq=Bs, t_len=Ts, hs=hs)
            o = jnp.concatenate([o_p, o_s], axis=1)
            x = _mm(o, rwkv_w_o[j].astype(BF16)[None], res=x, slab_in=True, name="proj_o")
            shift_p.append(h_last[:Bp, -1])
            shift_s.append(h_last[Bp:, -1])
            wkv_p.append(_unpack_state(sp, hs))
            wkv_s.append(_unpack_state(ss, hs))
        else:
            hist = jnp.concatenate([jnp.zeros((Bp, 16, D), F32),
                                    jnp.pad(state_pool[j], ((0, 0), (1, 0), (0, 0)))], axis=0)
            d, h_last = _pool(x, hist, g_mix, lay)
            x = _mm(d, pool_w[j].astype(BF16), scale=pool_scale[j][None, :], res=x, name="pool_proj")
            nh = state_pool.shape[2]
            pool_p.append(h_last[:Bp, SEQ_TILE - nh:])
            pool_s.append(h_last[Bp:, SEQ_TILE - nh:])
        x = _moe_ple(x, p_all[i], i, prm, final=(i == depth - 1), head_rows=Rp)

    y_prompt = x[0].reshape(Bp, Tp, D)
    y_sample = x[1].reshape(Bs, Ts, D)
    return (y_prompt, y_sample, jnp.stack(shift_p), jnp.stack(wkv_p), jnp.stack(pool_p),
            jnp.stack(shift_s), jnp.stack(wkv_s), jnp.stack(pool_s))
```

```python
import functools

import jax
import jax.numpy as jnp
from jax import lax
from jax.experimental import pallas as pl
from jax.experimental.pallas import tpu as pltpu

F32 = jnp.float32
BF16 = jnp.bfloat16

NORM_EPS = 1e-6
GN_EPS = 64e-5
PAST_LEN = 4096
POOL_WINDOWS = (2, 4, 8, 16)
TOP_K = 2
LANES = 128
SEQ_TILE = 32
CHUNK = 64
QUAD = 4
SCAN_GROUP = 4
MOE_ROWS = 256
VMEM_LIMIT = 56 * 1024 * 1024


def _cparams(sem):
    return pltpu.CompilerParams(dimension_semantics=sem, vmem_limit_bytes=VMEM_LIMIT)


def _pick(n, cands):
    for c in cands:
        if n % c == 0:
            return c
    return n


def _rms(x, g):
    return x * lax.rsqrt(jnp.mean(x * x, axis=-1, keepdims=True) + NORM_EPS) * g


_NN = (((1,), (0,)), ((), ()))
_NT = (((1,), (1,)), ((), ()))
_TN = (((0,), (0,)), ((), ()))


def _dot(a, b, dims=_NN, prec=None):
    return lax.dot_general(a, b, dims, precision=prec, preferred_element_type=F32)


def _mm_kernel(*refs, has_res, has_scale, a_slabs, o_slabs):
    a_ref, w_ref = refs[0], refs[1]
    k = 2
    if a_slabs:
        a = jnp.concatenate([a_ref[q] for q in range(a_slabs)], axis=1)
    else:
        a = a_ref[...]
    acc = _dot(a, w_ref[...])
    if has_scale:
        acc = acc * refs[k][...]
        k += 1
    if has_res:
        acc = refs[k][...] + acc
        k += 1
    o_ref = refs[k]
    if o_slabs:
        sw = o_ref.shape[2]
        for q in range(o_slabs):
            o_ref[q] = acc[:, q * sw:(q + 1) * sw].astype(o_ref.dtype)
    else:
        o_ref[...] = acc.astype(o_ref.dtype)


def _mm(a, w, *, res=None, scale=None, out_dtype=F32, slab_in=False, slab_out=0, name="mm"):
    M = a.shape[1] if slab_in else a.shape[0]
    G, Kg, Ng = w.shape
    tm = _pick(M, (512, 256, 128, 64, 32, 16, 8))
    tn = _pick(Ng, (1024, 512, 256, 128))
    nn = Ng // tn
    if slab_in:
        assert G == 1
        a_spec = pl.BlockSpec((a.shape[0], tm, a.shape[2]), lambda g, n, m: (0, m, 0))
    else:
        a_spec = pl.BlockSpec((tm, Kg), lambda g, n, m: (m, g))
    in_specs = [a_spec, pl.BlockSpec((None, Kg, tn), lambda g, n, m: (g, 0, n))]
    args = [a, w]
    if scale is not None:
        in_specs.append(pl.BlockSpec((1, tn), lambda g, n, m: (0, g * nn + n)))
        args.append(scale)
    if res is not None:
        in_specs.append(pl.BlockSpec((tm, tn), lambda g, n, m: (m, g * nn + n)))
        args.append(res)
    if slab_out:
        assert tn % slab_out == 0
        per = tn // slab_out
        out_spec = pl.BlockSpec((per, tm, slab_out), lambda g, n, m: (g * nn + n, m, 0))
        out_shape = jax.ShapeDtypeStruct((G * Ng // slab_out, M, slab_out), out_dtype)
    else:
        per = 0
        out_spec = pl.BlockSpec((tm, tn), lambda g, n, m: (m, g * nn + n))
        out_shape = jax.ShapeDtypeStruct((M, G * Ng), out_dtype)
    return pl.pallas_call(
        functools.partial(_mm_kernel, has_res=res is not None, has_scale=scale is not None,
                          a_slabs=a.shape[0] if slab_in else 0, o_slabs=per),
        grid=(G, nn, M // tm),
        in_specs=in_specs,
        out_specs=out_spec,
        out_shape=out_shape,
        compiler_params=_cparams(("arbitrary", "arbitrary", "arbitrary")),
        name=name,
    )(*args)


def _lora_kernel(a_ref, w1_ref, w2_ref, b_ref, o_ref, *, mode):
    t = _dot(a_ref[...], w1_ref[...])
    if mode == "decay":
        t = jnp.tanh(t)
    elif mode == "gate":
        t = jax.nn.sigmoid(t)
    z = _dot(t.astype(BF16), w2_ref[...]) + b_ref[...]
    if mode == "decay":
        z = -jnp.exp(-jax.nn.softplus(-z) - 0.5)
    elif mode == "aaa":
        z = jax.nn.sigmoid(z)
    sw = o_ref.shape[2]
    for q in range(o_ref.shape[0]):
        o_ref[q] = z[:, q * sw:(q + 1) * sw]


def _lora(a, w1, w2, bias, mode, sw):
    M, K = a.shape
    L = w1.shape[1]
    D = w2.shape[1]
    tm = _pick(M, (256, 128, 64, 32, 16, 8))
    return pl.pallas_call(
        functools.partial(_lora_kernel, mode=mode),
        grid=(M // tm,),
        in_specs=[
            pl.BlockSpec((tm, K), lambda m: (m, 0)),
            pl.BlockSpec((K, L), lambda m: (0, 0)),
            pl.BlockSpec((L, D), lambda m: (0, 0)),
            pl.BlockSpec((1, D), lambda m: (0, 0)),
        ],
        out_specs=pl.BlockSpec((D // sw, tm, sw), lambda m: (0, m, 0)),
        out_shape=jax.ShapeDtypeStruct((D // sw, M, sw), F32),
        compiler_params=_cparams(("arbitrary",)),
        name="lora_" + mode,
    )(a, w1, w2, bias)


class _SeqLayout:
    def __init__(self, bp, tp, bs, ts):
        assert tp % SEQ_TILE == 0 and ts % SEQ_TILE == 0
        self.bp, self.tp, self.bs, self.ts = bp, tp, bs, ts
        self.tiles_p = tp // SEQ_TILE
        self.tiles_s = ts // SEQ_TILE
        self.n_tiles_p = bp * self.tiles_p
        self.n_tiles = self.n_tiles_p + bs * self.tiles_s
        self.n_seq = bp + bs
        self.rows = bp * tp + bs * ts

    def seq_of_tile(self, i):
        return jnp.where(i < self.n_tiles_p, i // self.tiles_p,
                         self.bp + (i - self.n_tiles_p) // self.tiles_s)

    def tile_in_seq(self, i):
        return jnp.where(i < self.n_tiles_p, i % self.tiles_p, (i - self.n_tiles_p) % self.tiles_s)


def _norm_mix_kernel(x_ref, xp_ref, sh_ref, g_ref, mu_ref, *outs, lay):
    i = pl.program_id(0)
    g = g_ref[...]
    h = _rms(x_ref[...], g)
    h_before = _rms(xp_ref[7:8, :], g)
    first = lay.tile_in_seq(i) == 0
    prev_row = jnp.where(first, sh_ref[...], h_before)
    rows = lax.broadcasted_iota(jnp.int32, h.shape, 0)
    h_prev = jnp.where(rows == 0, prev_row, pltpu.roll(h, 1, axis=0))
    xx = h_prev - h
    for n in range(6):
        outs[n][...] = (h + xx * mu_ref[n:n + 1, :]).astype(BF16)
    outs[6][...] = h


def _norm_mix(x, shift_all, g, mu, lay):
    R, D = x.shape
    tpb = SEQ_TILE // 8
    row_spec = pl.BlockSpec((SEQ_TILE, D), lambda i: (i, 0))
    outs = pl.pallas_call(
        functools.partial(_norm_mix_kernel, lay=lay),
        grid=(lay.n_tiles,),
        in_specs=[
            row_spec,
            pl.BlockSpec((8, D), lambda i: (jnp.maximum(i * tpb - 1, 0), 0)),
            pl.BlockSpec((None, 1, D), lambda i: (lay.seq_of_tile(i), 0, 0)),
            pl.BlockSpec((1, D), lambda i: (0, 0)),
            pl.BlockSpec((6, D), lambda i: (0, 0)),
        ],
        out_specs=[row_spec] * 6 + [pl.BlockSpec((None, SEQ_TILE, D), lambda i: (lay.seq_of_tile(i), 0, 0))],
        out_shape=[jax.ShapeDtypeStruct((R, D), BF16)] * 6
        + [jax.ShapeDtypeStruct((lay.n_seq, SEQ_TILE, D), F32)],
        compiler_params=_cparams(("arbitrary",)),
        name="norm_mix",
    )(x, x, shift_all, g, mu)
    return outs[:6], outs[6]


def _scan_kernel(r_ref, k_ref, v_ref, lw_ref, a_ref, g_ref, kk_ref, ka_ref, rk_ref, lnw_ref, lnb_ref,
                 *rest, t_blk, hs, group, n_chunks):
    s0_ref = rest[0] if len(rest) == 3 else None
    o_ref, s_ref = rest[-2:]
    C = CHUNK
    nq, _, W = r_ref.shape
    c = pl.program_id(1)

    @pl.when(c == 0)
    def _():
        s_ref[...] = jnp.zeros(s_ref.shape, F32) if s0_ref is None else s0_ref[...]

    ri = lax.broadcasted_iota(jnp.int32, (W, W), 0)
    ci = lax.broadcasted_iota(jnp.int32, (W, W), 1)
    head_blk = (ri // hs) == (ci // hs)
    ones_blk = jnp.where(head_blk, 1.0, 0.0).astype(F32)
    ti = lax.broadcasted_iota(jnp.int32, (C, C), 0)
    si = lax.broadcasted_iota(jnp.int32, (C, C), 1)
    tri_incl = jnp.where(ti >= si, 1.0, 0.0).astype(F32)
    lane_head = lax.broadcasted_iota(jnp.int32, (C, W), 1) // hs
    rj = lax.broadcasted_iota(jnp.int32, (QUAD * C, QUAD * C), 0)
    cj = lax.broadcasted_iota(jnp.int32, (QUAD * C, QUAD * C), 1)
    t_in = rj % C
    s_in = cj % C
    strict = jnp.where(t_in > s_in, 1.0, 0.0).astype(F32)
    incl = jnp.where(t_in >= s_in, 1.0, 0.0).astype(F32)
    eye = jnp.where(rj == cj, 1.0, 0.0).astype(F32)

    ones_bf = ones_blk.astype(BF16)
    head_masks = [jnp.where(lane_head == h, 1.0, 0.0).astype(BF16) for h in range(QUAD)]

    def fold(x):
        y = x[0:C]
        for h in range(1, QUAD):
            y = y + x[h * C:(h + 1) * C]
        return y

    def group_body(it, carry, *, passes, group):
        qs = [it * group + j for j in range(group)]
        each = lambda f, *xs: [f(*t) for t in zip(*xs)]

        full = passes == 6

        def parts(x):
            if full:
                return (x,)
            hi = x.astype(BF16)
            if passes == 1:
                return (hi,)
            return (hi, (x - hi.astype(F32)).astype(BF16))

        def bdot(x, y, dims=_NN):
            if full:
                return _dot(x[0], y[0], dims, lax.Precision.HIGHEST)
            out = _dot(x[0], y[0], dims)
            if passes > 1:
                out = out + (_dot(x[0], y[1], dims) + _dot(x[1], y[0], dims))
            return out

        def head_sum(x):
            if full:
                return _dot(x, ones_blk, _NN, lax.Precision.HIGHEST)
            out = _dot(parts(x)[0], ones_bf)
            if passes > 1:
                out = out + _dot(parts(x)[1], ones_bf)
            return out

        def stack(x):
            return tuple(jnp.concatenate([xb * m for m in head_masks], axis=0) for xb in parts(x))

        def load(ref):
            def one(q):
                x = ref[q]
                if t_blk < C:
                    x = jnp.concatenate([x, jnp.zeros((C - t_blk, W), F32)], axis=0)
                return x
            return [one(q) for q in qs]

        r, k, v, lw, a = load(r_ref), load(k_ref), load(v_ref), load(lw_ref), load(a_ref)
        kk = each(lambda k_, q: k_ * kk_ref[q], k, qs)
        kk = each(lambda x: x * lax.rsqrt(jnp.maximum(head_sum(x * x), 1e-24)), kk)
        kf = each(lambda k_, a_, q: k_ * (1.0 + (a_ - 1.0) * ka_ref[q]), k, a, qs)
        bv = each(lambda x, a_: x * a_, kk, a)

        cum = each(lambda x: _dot(tri_incl, x, _NN, lax.Precision.HIGHEST), lw)
        e_neg = each(lambda x: jnp.exp(-x), cum)
        rt = each(lambda r_, x: r_ * jnp.exp(x), r, cum)
        at = each(lambda x, c_, l_: -x * jnp.exp(c_ - l_), kk, cum, lw)
        kt = each(lambda x, e: x * e, kf, e_neg)
        bt = each(lambda x, e: x * e, bv, e_neg)
        p_c = each(lambda x: jnp.exp(x[C - 1:C, :]), cum)

        a_st, r_st, b_st, k_st, v_st = (each(stack, x) for x in (at, rt, bt, kt, v))
        a_ab = each(lambda x, y: bdot(x, y, _NT) * strict, a_st, b_st)
        a_ak = each(lambda x, y: parts(bdot(x, y, _NT) * strict), a_st, k_st)
        a_rb = each(lambda x, y: parts(bdot(x, y, _NT) * incl), r_st, b_st)
        a_rk = each(lambda x, y: parts(bdot(x, y, _NT) * incl), r_st, k_st)

        t_inv = each(lambda x: eye + x, a_ab)
        n_pow = each(parts, a_ab)
        p = 1
        while 2 * p < C:
            n_pow = each(lambda x: parts(bdot(x, x)), n_pow)
            t_inv = each(lambda t, n: t + bdot(parts(t), n), t_inv, n_pow)
            p *= 2
        t_bf = each(parts, t_inv)

        s = [s_ref[q] for q in qs]
        s_bf = each(parts, s)
        u_in = each(lambda a_, s_, ak, v_: parts(bdot(a_, s_, _NT) + bdot(ak, v_)), a_st, s_bf, a_ak, v_st)
        u_st = each(bdot, t_bf, u_in)
        o_st = each(lambda r_, s_, rb, u_, rk, v_: bdot(r_, s_, _NT) + bdot(rb, parts(u_)) + bdot(rk, v_),
                    r_st, s_bf, a_rb, u_st, a_rk, v_st)
        o = each(fold, o_st)
        u = each(fold, u_st)
        upd = each(lambda u_, v_, b_, k_, pc: bdot(parts(jnp.concatenate([u_, v_], axis=0)),
                                                   parts(jnp.concatenate([b_ * pc, k_ * pc], axis=0)), _TN),
                   u, v, bt, kt, p_c)
        for q, s_, pc, up in zip(qs, s, p_c, upd):
            s_ref[q] = s_ * pc + jnp.where(head_blk, up, 0.0)

        inv_n = 1.0 / hs
        o_c = each(lambda x: x - head_sum(x) * inv_n, o)
        o_n = each(lambda x: x * lax.rsqrt(head_sum(x * x) * inv_n + GN_EPS), o_c)
        bonus = each(lambda r_, k_, q: head_sum(r_ * k_ * rk_ref[q]), r, kf, qs)
        for q, x, bo, v_ in zip(qs, o_n, bonus, v):
            y = x * lnw_ref[q] + lnb_ref[q] + bo * v_
            o_ref[q] = (y[:t_blk] * g_ref[q]).astype(o_ref.dtype)
        return carry

    def run(passes, grp):
        lax.fori_loop(0, nq // grp, functools.partial(group_body, passes=passes, group=grp), 0)

    if n_chunks > 1:
        last = n_chunks - 1

        @pl.when(c < last)
        def _():
            run(1, group)

        @pl.when(c == last)
        def _():
            run(6, _pick(nq, (2, 1)))
    else:
        run(1, group)


def _scan(r, k, v, lw, a, g, prm, s0, *, row0, n_seq, t_len, hs):
    nq, _, W = r.shape
    t_blk = min(CHUNK, t_len)
    n_chunks = t_len // t_blk
    blk0 = row0 // t_blk
    row_spec = pl.BlockSpec((nq, t_blk, W), lambda b, c: (0, blk0 + b * n_chunks + c, 0))
    prm_spec = pl.BlockSpec((nq, 1, W), lambda b, c: (0, 0, 0))
    st_spec = pl.BlockSpec((None, nq, W, W), lambda b, c: (b, 0, 0, 0))
    return pl.pallas_call(
        functools.partial(_scan_kernel, t_blk=t_blk, hs=hs, group=_pick(nq, (SCAN_GROUP, 2, 1)),
                          n_chunks=n_chunks),
        grid=(n_seq, n_chunks),
        in_specs=[row_spec] * 6 + [prm_spec] * 5 + ([] if s0 is None else [st_spec]),
        out_specs=[pl.BlockSpec((nq, t_blk, W), lambda b, c: (0, b * n_chunks + c, 0)), st_spec],
        out_shape=[jax.ShapeDtypeStruct((nq, n_seq * t_len, W), BF16),
                   jax.ShapeDtypeStruct((n_seq, nq, W, W), F32)],
        compiler_params=_cparams(("arbitrary", "arbitrary")),
        name="rwkv_scan",
    )(r, k, v, lw, a, g, *prm, *(() if s0 is None else (s0,)))


def _pack_state(s, hs):
    n, H = s.shape[0], s.shape[1]
    s = s.reshape(n, H // QUAD, QUAD, hs, hs)
    eye = jnp.eye(QUAD, dtype=s.dtype)
    bd = s[:, :, :, :, None, :] * eye[None, None, :, None, :, None]
    return bd.reshape(n, H // QUAD, QUAD * hs, QUAD * hs)


def _unpack_state(bd, hs):
    n, nq = bd.shape[0], bd.shape[1]
    x = bd.reshape(n, nq, QUAD, hs, QUAD, hs)
    x = jnp.stack([x[:, :, j, :, j, :] for j in range(QUAD)], axis=2)
    return x.reshape(n, nq * QUAD, hs, hs)


def _pool_kernel(x_ref, xp_ref, hist_ref, g_ref, d_ref, hl_ref, *, lay):
    i = pl.program_id(0)
    g = g_ref[...]
    h = _rms(x_ref[...], g)
    tis = lay.tile_in_seq(i)
    prev = jnp.where(tis == 0, hist_ref[...], _rms(xp_ref[...], g))
    ext = jnp.concatenate([prev, h], axis=0)
    D = h.shape[1]
    gw = D // len(POOL_WINDOWS)
    pos0 = jnp.where(i < lay.n_tiles_p, 0, PAST_LEN) + tis * SEQ_TILE
    pos = pos0 + lax.broadcasted_iota(jnp.int32, (SEQ_TILE, 1), 0)
    for gi, w in enumerate(POOL_WINDOWS):
        e = ext[:, gi * gw:(gi + 1) * gw]
        s = e
        step = 1
        while step < w:
            s = s + pltpu.roll(s, step, axis=0)
            step *= 2
        cnt = jnp.minimum(pos + 1, w).astype(F32)
        mean = s[16:] / cnt
        d_ref[:, gi * gw:(gi + 1) * gw] = (mean - h[:, gi * gw:(gi + 1) * gw]).astype(BF16)
    hl_ref[...] = h


def _pool(x, hist16, g, lay):
    R, D = x.shape
    tp16 = SEQ_TILE // 16
    return pl.pallas_call(
        functools.partial(_pool_kernel, lay=lay),
        grid=(lay.n_tiles,),
        in_specs=[
            pl.BlockSpec((SEQ_TILE, D), lambda i: (i, 0)),
            pl.BlockSpec((16, D), lambda i: (jnp.maximum(i * tp16 - 1, 0), 0)),
            pl.BlockSpec((None, 16, D), lambda i: (lay.seq_of_tile(i), 0, 0)),
            pl.BlockSpec((1, D), lambda i: (0, 0)),
        ],
        out_specs=[pl.BlockSpec((SEQ_TILE, D), lambda i: (i, 0)),
                   pl.BlockSpec((None, SEQ_TILE, D), lambda i: (lay.seq_of_tile(i), 0, 0))],
        out_shape=[jax.ShapeDtypeStruct((R, D), BF16),
                   jax.ShapeDtypeStruct((lay.n_seq, SEQ_TILE, D), F32)],
        compiler_params=_cparams(("arbitrary",)),
        name="pool_mix",
    )(x, x, hist16, g)


def _router_kernel(x_ref, g_ref, w_ref, b_ref, hn_ref, wts_ref, eid_ref, cnt_ref, *, n_grp, per_grp):
    hn = _rms(x_ref[...], g_ref[...])
    hb = hn.astype(BF16)
    bits = lax.bitcast_convert_type(hb.astype(F32), jnp.uint32)
    for c in range(hn_ref.shape[1] // LANES):
        lo = bits[:, (2 * c) * LANES:(2 * c + 1) * LANES] >> 16
        hi = bits[:, (2 * c + 1) * LANES:(2 * c + 2) * LANES] & jnp.uint32(0xFFFF0000)
        hn_ref[:, c * LANES:(c + 1) * LANES] = lo | hi
    logits = _dot(hb, w_ref[...]) + b_ref[...]
    lane = lax.broadcasted_iota(jnp.int32, logits.shape, 1).astype(F32)
    neg = -jnp.inf
    big = float(logits.shape[1])
    gl = jnp.where(lane < n_grp, logits, neg)
    gmax = jnp.max(gl, axis=-1, keepdims=True)
    gsum = jnp.sum(jnp.exp(gl - gmax), axis=-1, keepdims=True)
    g_idx = jnp.min(jnp.where(gl == gmax, lane, big), axis=-1, keepdims=True)
    g_gate = 1.0 / gsum
    lo = n_grp + g_idx * per_grp
    el = jnp.where((lane >= lo) & (lane < lo + per_grp), logits, neg)
    e1 = jnp.max(el, axis=-1, keepdims=True)
    i1 = jnp.min(jnp.where(el == e1, lane, big), axis=-1, keepdims=True)
    el2 = jnp.where(lane == i1, neg, el)
    e2 = jnp.max(el2, axis=-1, keepdims=True)
    i2 = jnp.min(jnp.where(el2 == e2, lane, big), axis=-1, keepdims=True)
    esum = jnp.sum(jnp.exp(el - e1), axis=-1, keepdims=True)
    p1 = 1.0 / esum
    p2 = jnp.exp(e2 - e1) / esum
    w1 = g_gate * p1 / (p1 + p2)
    w2 = g_gate * p2 / (p1 + p2)
    wts_ref[...] = jnp.where(lane == 0, w1, jnp.where(lane == 1, w2, 0.0))

    @pl.when(pl.program_id(0) == 0)
    def _():
        cnt_ref[...] = jnp.zeros(cnt_ref.shape, F32)

    e1 = i1 - n_grp
    e2 = i2 - n_grp
    oh1 = jnp.where(lane == e1, 1.0, 0.0)
    oh2 = jnp.where(lane == e2, 1.0, 0.0)
    both = oh1 + oh2
    tm = both.shape[0]
    before = (lax.broadcasted_iota(jnp.int32, (tm, tm), 1) < lax.broadcasted_iota(jnp.int32, (tm, tm), 0))
    seen = _dot(jnp.where(before, 1.0, 0.0).astype(BF16), both.astype(BF16)) + cnt_ref[...]
    rank1 = jnp.sum(oh1 * seen, axis=-1, keepdims=True)
    rank2 = jnp.sum(oh2 * seen, axis=-1, keepdims=True)
    cnt_ref[...] += jnp.sum(both, axis=0, keepdims=True)
    eid_ref[...] = jnp.where(lane == 0, e1, jnp.where(lane == 1, e2, jnp.where(
        lane == 2, rank1, jnp.where(lane == 3, rank2, 0.0)))).astype(jnp.int32)


def _router(x, g, w_cat, b_cat, n_grp, per_grp):
    R, D = x.shape
    tm = _pick(R, (256, 128, 64, 32, 16, 8))
    L = w_cat.shape[1]
    return pl.pallas_call(
        functools.partial(_router_kernel, n_grp=n_grp, per_grp=per_grp),
        grid=(R // tm,),
        in_specs=[
            pl.BlockSpec((tm, D), lambda m: (m, 0)),
            pl.BlockSpec((1, D), lambda m: (0, 0)),
            pl.BlockSpec((D, L), lambda m: (0, 0)),
            pl.BlockSpec((1, L), lambda m: (0, 0)),
        ],
        out_specs=[pl.BlockSpec((tm, D // 2), lambda m: (m, 0)),
                   pl.BlockSpec((tm, L), lambda m: (m, 0)),
                   pl.BlockSpec((tm, L), lambda m: (m, 0)),
                   pl.BlockSpec((1, L), lambda m: (0, 0))],
        out_shape=[jax.ShapeDtypeStruct((R, D // 2), jnp.uint32),
                   jax.ShapeDtypeStruct((R, L), F32),
                   jax.ShapeDtypeStruct((R, L), jnp.int32),
                   jax.ShapeDtypeStruct((1, L), F32)],
        compiler_params=_cparams(("arbitrary",)),
        name="moe_router",
    )(x, g, w_cat, b_cat)


def _pitch(nc):
    return nc + 4


def _issue_rows(src_ref, row_of, n_rows, buf, sem):
    nc = src_ref.shape[1]

    def issue(j, carry):
        for par in range(2):
            r = 2 * j + par
            pltpu.make_async_copy(src_ref.at[row_of(r)], buf.at[pl.ds(r * _pitch(nc), nc)], sem).start(priority=par)
        return carry

    lax.fori_loop(0, n_rows // 2, issue, 0)


def _wait_rows(src_ref, n_rows, buf, sem):
    nc = src_ref.shape[1]

    def drain(r, carry):
        pltpu.make_async_copy(src_ref.at[0], buf.at[pl.ds(r * _pitch(nc), nc)], sem).wait()
        return carry

    lax.fori_loop(0, n_rows, drain, 0)


def _gather_kernel(idx_ref, nblk_ref, src_ref, o_ref, buf0, buf1, sem):
    G = o_ref.shape[0]
    nc = src_ref.shape[1]
    b = pl.program_id(0)
    nblk = nblk_ref[0]
    bufs = (buf0, buf1)

    def fetch(blk, slot):
        _issue_rows(src_ref, lambda r: idx_ref[blk * G + r], G, bufs[slot], sem.at[slot])

    @pl.when((b == 0) & (nblk > 0))
    def _():
        fetch(0, 0)

    for slot in range(2):
        @pl.when((b < nblk) & (b % 2 == slot))
        def _(slot=slot):
            @pl.when(b + 1 < nblk)
            def _():
                fetch(b + 1, 1 - slot)

            _wait_rows(src_ref, G, bufs[slot], sem.at[slot])
            for cc in range(nc):
                w = bufs[slot][pl.ds(cc, G, stride=_pitch(nc)), :]
                lo = lax.bitcast_convert_type(w << 16, F32)
                hi = lax.bitcast_convert_type(w & jnp.uint32(0xFFFF0000), F32)
                o_ref[:, (2 * cc) * LANES:(2 * cc + 1) * LANES] = lo.astype(BF16)
                o_ref[:, (2 * cc + 1) * LANES:(2 * cc + 2) * LANES] = hi.astype(BF16)

    @pl.when(b >= nblk_ref[0])
    def _():
        o_ref[...] = jnp.zeros(o_ref.shape, o_ref.dtype)


def _gather_rows(src3, idx, nblk):
    n = idx.shape[0]
    _, nc, lanes = src3.shape
    G = MOE_ROWS
    assert n % G == 0 and lanes == LANES and src3.dtype == jnp.uint32
    return pl.pallas_call(
        _gather_kernel,
        grid_spec=pltpu.PrefetchScalarGridSpec(
            num_scalar_prefetch=2,
            grid=(n // G,),
            in_specs=[pl.BlockSpec(memory_space=pl.ANY)],
            out_specs=pl.BlockSpec((G, 2 * nc * LANES), lambda b, idx_r, nb_r: (b, 0)),
            scratch_shapes=[pltpu.VMEM((G * _pitch(nc), LANES), jnp.uint32),
                            pltpu.VMEM((G * _pitch(nc), LANES), jnp.uint32),
                            pltpu.SemaphoreType.DMA((2,))],
        ),
        out_shape=jax.ShapeDtypeStruct((n, 2 * nc * LANES), BF16),
        compiler_params=_cparams(("arbitrary",)),
        name="row_gather",
    )(idx, nblk, src3)


def _new_expert(blk_e_ref, b):
    prev = blk_e_ref[jnp.maximum(b - 1, 0)]
    return (b == 0) | (blk_e_ref[b] != prev)


def _moe_up_kernel(blk_e_ref, nblk_ref, x_ref, wg_ref, wu_ref, h_ref, wg_bf, wu_bf):
    b = pl.program_id(1)

    @pl.when(b < nblk_ref[0])
    def _():
        @pl.when(_new_expert(blk_e_ref, b))
        def _():
            wg_bf[...] = wg_ref[...].astype(BF16)
            wu_bf[...] = wu_ref[...].astype(BF16)

        x = x_ref[...]
        gate = _dot(x, wg_bf[...])
        up = _dot(x, wu_bf[...])
        h_ref[...] = (jax.nn.silu(gate) * up).astype(BF16)

    @pl.when(b >= nblk_ref[0])
    def _():
        h_ref[...] = jnp.zeros(h_ref.shape, BF16)


def _moe_up(xg, w_gate, w_up, layer, blk_e, nblk):
    P, D = xg.shape
    De = w_gate.shape[3]
    tj = _pick(De, (512, 256, 128))
    nb = P // MOE_ROWS
    w_spec = pl.BlockSpec((None, None, D, tj), lambda j, b, be, nbk: (layer, be[b], 0, j))
    return pl.pallas_call(
        _moe_up_kernel,
        grid_spec=pltpu.PrefetchScalarGridSpec(
            num_scalar_prefetch=2,
            grid=(De // tj, nb),
            in_specs=[pl.BlockSpec((MOE_ROWS, D), lambda j, b, be, nbk: (b, 0)), w_spec, w_spec],
            out_specs=pl.BlockSpec((MOE_ROWS, tj), lambda j, b, be, nbk: (b, j)),
            scratch_shapes=[pltpu.VMEM((D, tj), BF16), pltpu.VMEM((D, tj), BF16)],
        ),
        out_shape=jax.ShapeDtypeStruct((P, De), BF16),
        compiler_params=_cparams(("arbitrary", "arbitrary")),
        name="moe_up",
    )(blk_e, nblk, xg, w_gate, w_up)


def _moe_down_kernel(blk_e_ref, nblk_ref, h_ref, wd_ref, y_ref, wd_bf):
    b = pl.program_id(1)

    @pl.when(b < nblk_ref[0])
    def _():
        @pl.when(_new_expert(blk_e_ref, b))
        def _():
            wd_bf[...] = wd_ref[...].astype(BF16)

        y = _dot(h_ref[...], wd_bf[...])
        for cc in range(y_ref.shape[1]):
            y_ref[:, cc, :] = y[:, cc * LANES:(cc + 1) * LANES]

    @pl.when(b >= nblk_ref[0])
    def _():
        y_ref[...] = jnp.zeros(y_ref.shape, F32)


def _moe_down(h, w_down, layer, blk_e, nblk):
    P, De = h.shape
    D = w_down.shape[3]
    tn = _pick(D, (2048, 1024, 512, 256, 128))
    nb = P // MOE_ROWS
    return pl.pallas_call(
        _moe_down_kernel,
        grid_spec=pltpu.PrefetchScalarGridSpec(
            num_scalar_prefetch=2,
            grid=(D // tn, nb),
            in_specs=[pl.BlockSpec((MOE_ROWS, De), lambda n, b, be, nbk: (b, 0)),
                      pl.BlockSpec((None, None, De, tn), lambda n, b, be, nbk: (layer, be[b], 0, n))],
            out_specs=pl.BlockSpec((MOE_ROWS, tn // LANES, LANES), lambda n, b, be, nbk: (b, n, 0)),
            scratch_shapes=[pltpu.VMEM((De, tn), BF16)],
        ),
        out_shape=jax.ShapeDtypeStruct((P, D // LANES, LANES), F32),
        compiler_params=_cparams(("arbitrary", "arbitrary")),
        name="moe_down",
    )(blk_e, nblk, h, w_down)


def _dispatch_plan(eid, rank, counts):
    R = eid.shape[0]
    S = R * TOP_K
    n_experts = counts.shape[0]
    e_flat = eid.reshape(-1)
    padded = (counts + MOE_ROWS - 1) // MOE_ROWS * MOE_ROWS
    pad_end = jnp.cumsum(padded)
    dest = ((pad_end - padded)[e_flat] + rank.reshape(-1)).astype(jnp.int32)
    nb = -(-S // MOE_ROWS) + n_experts
    buf_tok = jnp.zeros((nb * MOE_ROWS,), jnp.int32).at[dest].set(jnp.arange(S, dtype=jnp.int32) // TOP_K)
    blk_e = jnp.minimum(jnp.searchsorted(pad_end, jnp.arange(nb, dtype=jnp.int32) * MOE_ROWS, side='right'),
                        n_experts - 1).astype(jnp.int32)
    nblk = (pad_end[-1:] // MOE_ROWS).astype(jnp.int32)
    return dest, buf_tok, blk_e, nblk


def _ple_kernel(slot_ref, x_ref, y_hbm, wts_ref, p_ref, g_ref, gd_ref, gu_ref, win_ref, gf_ref, *rest,
                final, head_tiles, n_tok):
    n_out = 2 if final else 1
    o_refs = rest[:n_out]
    buf0, buf1, ysel, sem = rest[n_out:]
    bufs = (buf0, buf1)
    m = pl.program_id(0)
    tm = x_ref.shape[0]
    nc = y_hbm.shape[1]

    def fetch(tile, slot):
        _issue_rows(y_hbm, lambda r: slot_ref[(r // tm) * n_tok + tile * tm + r % tm], TOP_K * tm,
                    bufs[slot], sem.at[slot])

    @pl.when(m == 0)
    def _():
        fetch(0, 0)

    for slot in range(2):
        @pl.when(m % 2 == slot)
        def _(slot=slot):
            @pl.when(m + 1 < pl.num_programs(0))
            def _():
                fetch(m + 1, 1 - slot)

            _wait_rows(y_hbm, TOP_K * tm, bufs[slot], sem.at[slot])
            for k in range(TOP_K):
                for cc in range(nc):
                    ysel[k, :, cc * LANES:(cc + 1) * LANES] = (
                        bufs[slot][pl.ds(k * tm * _pitch(nc) + cc, tm, stride=_pitch(nc)), :])

    wts = wts_ref[...]
    x = x_ref[...]
    for k in range(TOP_K):
        x = x + ysel[k] * wts[:, k:k + 1]
    hn = _rms(x, g_ref[...]).astype(BF16)
    t = _dot(hn, gd_ref[...]).astype(BF16)
    gate = jax.nn.sigmoid(_dot(t, gu_ref[...]))
    pe = _dot(p_ref[...], win_ref[...])
    x = x + pe * gate
    if not final:
        o_refs[0][...] = x
        return
    x = _rms(x, gf_ref[...])
    m = pl.program_id(0)

    @pl.when(m < head_tiles)
    def _():
        o_refs[0][...] = x

    @pl.when(m >= head_tiles)
    def _():
        o_refs[1][...] = x


def _ple(x, y3, slot_rows, wts, p, g, gd, gu, win, gf, final, head_rows):
    R, D = x.shape
    Pd = p.shape[1]
    L = wts.shape[1]
    nc = y3.shape[1]
    tm = _pick(R, (128, 64, 32, 16, 8))
    nt = R // tm
    row = pl.BlockSpec((tm, D), lambda m, sl: (m, 0))
    fixed = lambda shape: pl.BlockSpec(shape, lambda m, sl: (0, 0))
    assert head_rows % tm == 0
    ht = head_rows // tm
    if final:
        out_specs = [pl.BlockSpec((tm, D), lambda m, sl: (jnp.minimum(m, ht - 1), 0)),
                     pl.BlockSpec((tm, D), lambda m, sl: (jnp.maximum(m - ht, 0), 0))]
        out_shape = [jax.ShapeDtypeStruct((head_rows, D), F32), jax.ShapeDtypeStruct((R - head_rows, D), F32)]
    else:
        out_specs = row
        out_shape = jax.ShapeDtypeStruct((R, D), F32)
    stage = pltpu.VMEM((TOP_K * tm * _pitch(nc), LANES), F32)
    return pl.pallas_call(
        functools.partial(_ple_kernel, final=final, head_tiles=ht, n_tok=R),
        grid_spec=pltpu.PrefetchScalarGridSpec(
            num_scalar_prefetch=1,
            grid=(nt,),
            in_specs=[
                row,
                pl.BlockSpec(memory_space=pl.ANY),
                pl.BlockSpec((tm, L), lambda m, sl: (m, 0)),
                pl.BlockSpec((tm, Pd), lambda m, sl: (m, 0)),
                fixed((1, D)), fixed((D, Pd)), fixed((Pd, D)), fixed((Pd, D)), fixed((1, D)),
            ],
            out_specs=out_specs,
            scratch_shapes=[stage, stage, pltpu.VMEM((TOP_K, tm, D), F32), pltpu.SemaphoreType.DMA((2,))],
        ),
        out_shape=out_shape,
        compiler_params=_cparams(("arbitrary",)),
        name="moe_combine_ple",
    )(slot_rows, x, y3, wts, p, g, gd, gu, win, gf)


def _moe_ple(x, p, layer, prm, final, head_rows):
    R, D = x.shape
    assert (R * TOP_K) % MOE_ROWS == 0 and D % LANES == 0
    n_grp = prm["moe_w_grp"].shape[2]
    n_exp = prm["moe_w_exp"].shape[2]
    L = LANES
    w_cat = jnp.concatenate([prm["moe_w_grp"][layer], prm["moe_w_exp"][layer],
                             jnp.zeros((D, L - n_grp - n_exp), F32)], axis=1).astype(BF16)
    b_cat = jnp.concatenate([prm["moe_b_grp"][layer], prm["moe_b_exp"][layer],
                             jnp.zeros((L - n_grp - n_exp,), F32)])[None, :]
    hn, wts, route, cnt = _router(x, prm["norm_ffn"][layer][None, :], w_cat, b_cat, n_grp, n_exp // n_grp)
    dest, buf_tok, blk_e, nblk = _dispatch_plan(
        route[:, :TOP_K], route[:, TOP_K:2 * TOP_K], cnt[0, :n_exp].astype(jnp.int32))
    xg = _gather_rows(hn.reshape(R, D // (2 * LANES), LANES), buf_tok, nblk)
    hmid = _moe_up(xg, prm["moe_w_gate"], prm["moe_w_up"], layer, blk_e, nblk)
    y3 = _moe_down(hmid, prm["moe_w_down"], layer, blk_e, nblk)
    slot_rows = dest.reshape(R, TOP_K).T.reshape(-1)
    return _ple(x, y3, slot_rows, wts, p, prm["norm_ple"][layer][None, :],
                prm["ple_gate_down"][layer].astype(BF16), prm["ple_gate_up"][layer].astype(BF16),
                prm["ple_w_in"][layer].astype(BF16), prm["norm_final"][None, :], final, head_rows)


def kernel(x_prompt, x_sample, state_rwkv_shift, state_rwkv_wkv, state_pool, p_prompt, p_sample,
           rwkv_mu, rwkv_w_rkv, rwkv_w_o, rwkv_w0, rwkv_w1, rwkv_w2, rwkv_a0, rwkv_a1, rwkv_a2,
           rwkv_g1, rwkv_g2, rwkv_k_k, rwkv_k_a, rwkv_r_k, rwkv_lnx_w, rwkv_lnx_b,
           pool_w, pool_scale, norm_mix, norm_ffn, norm_ple, norm_final,
           moe_w_grp, moe_b_grp, moe_w_exp, moe_b_exp, moe_w_gate, moe_w_up, moe_w_down,
           ple_w_in, ple_gate_down, ple_gate_up):
    Bp, Tp, D = x_prompt.shape
    Bs, Ts, _ = x_sample.shape
    depth = norm_mix.shape[0]
    H, hs = rwkv_r_k.shape[1], rwkv_r_k.shape[2]
    W = QUAD * hs
    assert D % W == 0 and Ts <= CHUNK and Tp % CHUNK == 0
    lay = _SeqLayout(Bp, Tp, Bs, Ts)
    Rp = Bp * Tp
    prm = dict(moe_w_grp=moe_w_grp, moe_b_grp=moe_b_grp, moe_w_exp=moe_w_exp, moe_b_exp=moe_b_exp,
               moe_w_gate=moe_w_gate, moe_w_up=moe_w_up, moe_w_down=moe_w_down, norm_ffn=norm_ffn,
               norm_ple=norm_ple, norm_final=norm_final, ple_w_in=ple_w_in, ple_gate_down=ple_gate_down,
               ple_gate_up=ple_gate_up)

    x = jnp.concatenate([x_prompt.reshape(Rp, D), x_sample.reshape(Bs * Ts, D)], axis=0)
    Pd = p_prompt.shape[-1]
    p_all = jnp.concatenate([p_prompt.reshape(depth, Rp, Pd), p_sample.reshape(depth, Bs * Ts, Pd)],
                            axis=1).astype(BF16)

    def slabs(vec):
        return vec.reshape(D // W, 1, W)

    shift_p, wkv_p, pool_p, shift_s, wkv_s, pool_s = [], [], [], [], [], []
    for i in range(depth):
        j = i // 2
        g_mix = norm_mix[i][None, :]
        if i % 2 == 0:
            shift_all = jnp.concatenate([jnp.zeros((Bp, D), F32), state_rwkv_shift[j]], axis=0)[:, None, :]
            mixed, h_last = _norm_mix(x, shift_all, g_mix, rwkv_mu[j], lay)
            xr, xw, xk, xv, xa, xg = mixed
            w_rkv = rwkv_w_rkv[j].astype(BF16)
            r = _mm(xr, w_rkv[0:1], slab_out=W, name="proj_r")
            k = _mm(xk, w_rkv[1:2], slab_out=W, name="proj_k")
            v = _mm(xv, w_rkv[2:3], slab_out=W, name="proj_v")
            lw = _lora(xw, rwkv_w1[j].astype(BF16), rwkv_w2[j].astype(BF16), rwkv_w0[j][None, :], "decay", W)
            a = _lora(xa, rwkv_a1[j].astype(BF16), rwkv_a2[j].astype(BF16), rwkv_a0[j][None, :], "aaa", W)
            gl = rwkv_g1.shape[2]
            glp = -(-gl // LANES) * LANES
            g1 = jnp.pad(rwkv_g1[j], ((0, 0), (0, glp - gl))).astype(BF16)
            g2 = jnp.pad(rwkv_g2[j], ((0, glp - gl), (0, 0))).astype(BF16)
            g = _lora(xg, g1, g2, jnp.zeros((1, D), F32), "gate", W)
            sprm = (slabs(rwkv_k_k[j]), slabs(rwkv_k_a[j]), slabs(rwkv_r_k[j].reshape(D)),
                    slabs(rwkv_lnx_w[j]), slabs(rwkv_lnx_b[j]))
            o_p, sp = _scan(r, k, v, lw, a, g, sprm, None, row0=0, n_seq=Bp, t_len=Tp, hs=hs)
            o_s, ss = _scan(r, k, v, lw, a, g, sprm, _pack_state(state_rwkv_wkv[j], hs),
                            row0=Rp, n_seq=Bs, t_len=Ts, hs=hs)
            o = jnp.concatenate([o_p, o_s], axis=1)
            x = _mm(o, rwkv_w_o[j].astype(BF16)[None], res=x, slab_in=True, name="proj_o")
            shift_p.append(h_last[:Bp, -1])
            shift_s.append(h_last[Bp:, -1])
            wkv_p.append(_unpack_state(sp, hs))
            wkv_s.append(_unpack_state(ss, hs))
        else:
            hist = jnp.concatenate([jnp.zeros((Bp, 16, D), F32),
                                    jnp.pad(state_pool[j], ((0, 0), (1, 0), (0, 0)))], axis=0)
            d, h_last = _pool(x, hist, g_mix, lay)
            x = _mm(d, pool_w[j].astype(BF16), scale=pool_scale[j][None, :], res=x, name="pool_proj")
            nh = state_pool.shape[2]
            pool_p.append(h_last[:Bp, SEQ_TILE - nh:])
            pool_s.append(h_last[Bp:, SEQ_TILE - nh:])
        x = _moe_ple(x, p_all[i], i, prm, final=(i == depth - 1), head_rows=Rp)

    y_prompt = x[0].reshape(Bp, Tp, D)
    y_sample = x[1].reshape(Bs, Ts, D)
    return (y_prompt, y_sample, jnp.stack(shift_p), jnp.stack(wkv_p), jnp.stack(pool_p),
            jnp.stack(shift_s), jnp.stack(wkv_s), jnp.stack(pool_s))
```

```python
import functools

import jax
import jax.numpy as jnp
from jax import lax
from jax.experimental import pallas as pl
from jax.experimental.pallas import tpu as pltpu

F32 = jnp.float32
BF16 = jnp.bfloat16

NORM_EPS = 1e-6
GN_EPS = 64e-5
PAST_LEN = 4096
POOL_WINDOWS = (2, 4, 8, 16)
TOP_K = 2
LANES = 128
SEQ_TILE = 32
CHUNK = 64
QUAD = 4
SCAN_GROUP = 4
MOE_ROWS = 256
VMEM_LIMIT = 56 * 1024 * 1024


def _cparams(sem):
    return pltpu.CompilerParams(dimension_semantics=sem, vmem_limit_bytes=VMEM_LIMIT)


def _pick(n, cands):
    for c in cands:
        if n % c == 0:
            return c
    return n


def _rms(x, g):
    return x * lax.rsqrt(jnp.mean(x * x, axis=-1, keepdims=True) + NORM_EPS) * g


_NN = (((1,), (0,)), ((), ()))
_NT = (((1,), (1,)), ((), ()))
_TN = (((0,), (0,)), ((), ()))


def _dot(a, b, dims=_NN, prec=None):
    return lax.dot_general(a, b, dims, precision=prec, preferred_element_type=F32)


def _mm_kernel(*refs, has_res, has_scale, a_slabs, o_slabs):
    a_ref, w_ref = refs[0], refs[1]
    k = 2
    if a_slabs:
        a = jnp.concatenate([a_ref[q] for q in range(a_slabs)], axis=1)
    else:
        a = a_ref[...]
    acc = _dot(a, w_ref[...])
    if has_scale:
        acc = acc * refs[k][...]
        k += 1
    if has_res:
        acc = refs[k][...] + acc
        k += 1
    o_ref = refs[k]
    if o_slabs:
        sw = o_ref.shape[2]
        for q in range(o_slabs):
            o_ref[q] = acc[:, q * sw:(q + 1) * sw].astype(o_ref.dtype)
    else:
        o_ref[...] = acc.astype(o_ref.dtype)


def _mm(a, w, *, res=None, scale=None, out_dtype=F32, slab_in=False, slab_out=0, name="mm"):
    M = a.shape[1] if slab_in else a.shape[0]
    G, Kg, Ng = w.shape
    tm = _pick(M, (512, 256, 128, 64, 32, 16, 8))
    tn = _pick(Ng, (1024, 512, 256, 128))
    nn = Ng // tn
    if slab_in:
        assert G == 1
        a_spec = pl.BlockSpec((a.shape[0], tm, a.shape[2]), lambda g, n, m: (0, m, 0))
    else:
        a_spec = pl.BlockSpec((tm, Kg), lambda g, n, m: (m, g))
    in_specs = [a_spec, pl.BlockSpec((None, Kg, tn), lambda g, n, m: (g, 0, n))]
    args = [a, w]
    if scale is not None:
        in_specs.append(pl.BlockSpec((1, tn), lambda g, n, m: (0, g * nn + n)))
        args.append(scale)
    if res is not None:
        in_specs.append(pl.BlockSpec((tm, tn), lambda g, n, m: (m, g * nn + n)))
        args.append(res)
    if slab_out:
        assert tn % slab_out == 0
        per = tn // slab_out
        out_spec = pl.BlockSpec((per, tm, slab_out), lambda g, n, m: (g * nn + n, m, 0))
        out_shape = jax.ShapeDtypeStruct((G * Ng // slab_out, M, slab_out), out_dtype)
    else:
        per = 0
        out_spec = pl.BlockSpec((tm, tn), lambda g, n, m: (m, g * nn + n))
        out_shape = jax.ShapeDtypeStruct((M, G * Ng), out_dtype)
    return pl.pallas_call(
        functools.partial(_mm_kernel, has_res=res is not None, has_scale=scale is not None,
                          a_slabs=a.shape[0] if slab_in else 0, o_slabs=per),
        grid=(G, nn, M // tm),
        in_specs=in_specs,
        out_specs=out_spec,
        out_shape=out_shape,
        compiler_params=_cparams(("arbitrary", "arbitrary", "arbitrary")),
        name=name,
    )(*args)


def _lora_kernel(a_ref, w1_ref, w2_ref, b_ref, o_ref, *, mode):
    t = _dot(a_ref[...], w1_ref[...])
    if mode == "decay":
        t = jnp.tanh(t)
    elif mode == "gate":
        t = jax.nn.sigmoid(t)
    z = _dot(t.astype(BF16), w2_ref[...]) + b_ref[...]
    if mode == "decay":
        z = -jnp.exp(-jax.nn.softplus(-z) - 0.5)
    elif mode == "aaa":
        z = jax.nn.sigmoid(z)
    sw = o_ref.shape[2]
    for q in range(o_ref.shape[0]):
        o_ref[q] = z[:, q * sw:(q + 1) * sw]


def _lora(a, w1, w2, bias, mode, sw):
    M, K = a.shape
    L = w1.shape[1]
    D = w2.shape[1]
    tm = _pick(M, (256, 128, 64, 32, 16, 8))
    return pl.pallas_call(
        functools.partial(_lora_kernel, mode=mode),
        grid=(M // tm,),
        in_specs=[
            pl.BlockSpec((tm, K), lambda m: (m, 0)),
            pl.BlockSpec((K, L), lambda m: (0, 0)),
            pl.BlockSpec((L, D), lambda m: (0, 0)),
            pl.BlockSpec((1, D), lambda m: (0, 0)),
        ],
        out_specs=pl.BlockSpec((D // sw, tm, sw), lambda m: (0, m, 0)),
        out_shape=jax.ShapeDtypeStruct((D // sw, M, sw), F32),
        compiler_params=_cparams(("arbitrary",)),
        name="lora_" + mode,
    )(a, w1, w2, bias)


class _SeqLayout:
    def __init__(self, bp, tp, bs, ts):
        assert tp % SEQ_TILE == 0 and ts % SEQ_TILE == 0
        self.bp, self.tp, self.bs, self.ts = bp, tp, bs, ts
        self.tiles_p = tp // SEQ_TILE
        self.tiles_s = ts // SEQ_TILE
        self.n_tiles_p = bp * self.tiles_p
        self.n_tiles = self.n_tiles_p + bs * self.tiles_s
        self.n_seq = bp + bs
        self.rows = bp * tp + bs * ts

    def seq_of_tile(self, i):
        return jnp.where(i < self.n_tiles_p, i // self.tiles_p,
                         self.bp + (i - self.n_tiles_p) // self.tiles_s)

    def tile_in_seq(self, i):
        return jnp.where(i < self.n_tiles_p, i % self.tiles_p, (i - self.n_tiles_p) % self.tiles_s)


def _norm_mix_kernel(x_ref, xp_ref, sh_ref, g_ref, mu_ref, *outs, lay):
    i = pl.program_id(0)
    g = g_ref[...]
    h = _rms(x_ref[...], g)
    h_before = _rms(xp_ref[7:8, :], g)
    first = lay.tile_in_seq(i) == 0
    prev_row = jnp.where(first, sh_ref[...], h_before)
    rows = lax.broadcasted_iota(jnp.int32, h.shape, 0)
    h_prev = jnp.where(rows == 0, prev_row, pltpu.roll(h, 1, axis=0))
    xx = h_prev - h
    for n in range(6):
        outs[n][...] = (h + xx * mu_ref[n:n + 1, :]).astype(BF16)
    outs[6][...] = h


def _norm_mix(x, shift_all, g, mu, lay):
    R, D = x.shape
    tpb = SEQ_TILE // 8
    row_spec = pl.BlockSpec((SEQ_TILE, D), lambda i: (i, 0))
    outs = pl.pallas_call(
        functools.partial(_norm_mix_kernel, lay=lay),
        grid=(lay.n_tiles,),
        in_specs=[
            row_spec,
            pl.BlockSpec((8, D), lambda i: (jnp.maximum(i * tpb - 1, 0), 0)),
            pl.BlockSpec((None, 1, D), lambda i: (lay.seq_of_tile(i), 0, 0)),
            pl.BlockSpec((1, D), lambda i: (0, 0)),
            pl.BlockSpec((6, D), lambda i: (0, 0)),
        ],
        out_specs=[row_spec] * 6 + [pl.BlockSpec((None, SEQ_TILE, D), lambda i: (lay.seq_of_tile(i), 0, 0))],
        out_shape=[jax.ShapeDtypeStruct((R, D), BF16)] * 6
        + [jax.ShapeDtypeStruct((lay.n_seq, SEQ_TILE, D), F32)],
        compiler_params=_cparams(("arbitrary",)),
        name="norm_mix",
    )(x, x, shift_all, g, mu)
    return outs[:6], outs[6]


def _scan_kernel(r_ref, k_ref, v_ref, lw_ref, a_ref, g_ref, kk_ref, ka_ref, rk_ref, lnw_ref, lnb_ref,
                 *rest, t_blk, hs, group, n_chunks):
    s0_ref = rest[0] if len(rest) == 3 else None
    o_ref, s_ref = rest[-2:]
    C = CHUNK
    nq, _, W = r_ref.shape
    c = pl.program_id(1)

    @pl.when(c == 0)
    def _():
        s_ref[...] = jnp.zeros(s_ref.shape, F32) if s0_ref is None else s0_ref[...]

    ri = lax.broadcasted_iota(jnp.int32, (W, W), 0)
    ci = lax.broadcasted_iota(jnp.int32, (W, W), 1)
    head_blk = (ri // hs) == (ci // hs)
    ones_blk = jnp.where(head_blk, 1.0, 0.0).astype(F32)
    ti = lax.broadcasted_iota(jnp.int32, (C, C), 0)
    si = lax.broadcasted_iota(jnp.int32, (C, C), 1)
    tri_incl = jnp.where(ti >= si, 1.0, 0.0).astype(F32)
    lane_head = lax.broadcasted_iota(jnp.int32, (C, W), 1) // hs
    rj = lax.broadcasted_iota(jnp.int32, (QUAD * C, QUAD * C), 0)
    cj = lax.broadcasted_iota(jnp.int32, (QUAD * C, QUAD * C), 1)
    t_in = rj % C
    s_in = cj % C
    strict = jnp.where(t_in > s_in, 1.0, 0.0).astype(F32)
    incl = jnp.where(t_in >= s_in, 1.0, 0.0).astype(F32)
    eye = jnp.where(rj == cj, 1.0, 0.0).astype(F32)

    ones_bf = ones_blk.astype(BF16)
    head_masks = [jnp.where(lane_head == h, 1.0, 0.0).astype(BF16) for h in range(QUAD)]

    def fold(x):
        y = x[0:C]
        for h in range(1, QUAD):
            y = y + x[h * C:(h + 1) * C]
        return y

    def group_body(it, carry, *, passes, group):
        qs = [it * group + j for j in range(group)]
        each = lambda f, *xs: [f(*t) for t in zip(*xs)]

        full = passes == 6

        def parts(x):
            if full:
                return (x,)
            hi = x.astype(BF16)
            if passes == 1:
                return (hi,)
            return (hi, (x - hi.astype(F32)).astype(BF16))

        def bdot(x, y, dims=_NN):
            if full:
                return _dot(x[0], y[0], dims, lax.Precision.HIGHEST)
            out = _dot(x[0], y[0], dims)
            if passes > 1:
                out = out + (_dot(x[0], y[1], dims) + _dot(x[1], y[0], dims))
            return out

        def head_sum(x):
            if full:
                return _dot(x, ones_blk, _NN, lax.Precision.HIGHEST)
            out = _dot(parts(x)[0], ones_bf)
            if passes > 1:
                out = out + _dot(parts(x)[1], ones_bf)
            return out

        def stack(x):
            return tuple(jnp.concatenate([xb * m for m in head_masks], axis=0) for xb in parts(x))

        def load(ref):
            def one(q):
                x = ref[q]
                if t_blk < C:
                    x = jnp.concatenate([x, jnp.zeros((C - t_blk, W), F32)], axis=0)
                return x
            return [one(q) for q in qs]

        r, k, v, lw, a = load(r_ref), load(k_ref), load(v_ref), load(lw_ref), load(a_ref)
        kk = each(lambda k_, q: k_ * kk_ref[q], k, qs)
        kk = each(lambda x: x * lax.rsqrt(jnp.maximum(head_sum(x * x), 1e-24)), kk)
        kf = each(lambda k_, a_, q: k_ * (1.0 + (a_ - 1.0) * ka_ref[q]), k, a, qs)
        bv = each(lambda x, a_: x * a_, kk, a)

        cum = each(lambda x: _dot(tri_incl, x, _NN, lax.Precision.HIGHEST), lw)
        e_neg = each(lambda x: jnp.exp(-x), cum)
        rt = each(lambda r_, x: r_ * jnp.exp(x), r, cum)
        at = each(lambda x, c_, l_: -x * jnp.exp(c_ - l_), kk, cum, lw)
        kt = each(lambda x, e: x * e, kf, e_neg)
        bt = each(lambda x, e: x * e, bv, e_neg)
        p_c = each(lambda x: jnp.exp(x[C - 1:C, :]), cum)

        a_st, r_st, b_st, k_st, v_st = (each(stack, x) for x in (at, rt, bt, kt, v))
        a_ab = each(lambda x, y: bdot(x, y, _NT) * strict, a_st, b_st)
        a_ak = each(lambda x, y: parts(bdot(x, y, _NT) * strict), a_st, k_st)
        a_rb = each(lambda x, y: parts(bdot(x, y, _NT) * incl), r_st, b_st)
        a_rk = each(lambda x, y: parts(bdot(x, y, _NT) * incl), r_st, k_st)

        t_inv = each(lambda x: eye + x, a_ab)
        n_pow = each(parts, a_ab)
        p = 1
        while 2 * p < C:
            n_pow = each(lambda x: parts(bdot(x, x)), n_pow)
            t_inv = each(lambda t, n: t + bdot(parts(t), n), t_inv, n_pow)
            p *= 2
        t_bf = each(parts, t_inv)

        s = [s_ref[q] for q in qs]
        s_bf = each(parts, s)
        u_in = each(lambda a_, s_, ak, v_: parts(bdot(a_, s_, _NT) + bdot(ak, v_)), a_st, s_bf, a_ak, v_st)
        u_st = each(bdot, t_bf, u_in)
        o_st = each(lambda r_, s_, rb, u_, rk, v_: bdot(r_, s_, _NT) + bdot(rb, parts(u_)) + bdot(rk, v_),
                    r_st, s_bf, a_rb, u_st, a_rk, v_st)
        o = each(fold, o_st)
        u = each(fold, u_st)
        upd = each(lambda u_, v_, b_, k_, pc: bdot(parts(jnp.concatenate([u_, v_], axis=0)),
                                                   parts(jnp.concatenate([b_ * pc, k_ * pc], axis=0)), _TN),
                   u, v, bt, kt, p_c)
        for q, s_, pc, up in zip(qs, s, p_c, upd):
            s_ref[q] = s_ * pc + jnp.where(head_blk, up, 0.0)

        inv_n = 1.0 / hs
        o_c = each(lambda x: x - head_sum(x) * inv_n, o)
        o_n = each(lambda x: x * lax.rsqrt(head_sum(x * x) * inv_n + GN_EPS), o_c)
        bonus = each(lambda r_, k_, q: head_sum(r_ * k_ * rk_ref[q]), r, kf, qs)
        for q, x, bo, v_ in zip(qs, o_n, bonus, v):
            y = x * lnw_ref[q] + lnb_ref[q] + bo * v_
            o_ref[q] = (y[:t_blk] * g_ref[q]).astype(o_ref.dtype)
        return carry

    def run(passes, grp):
        lax.fori_loop(0, nq // grp, functools.partial(group_body, passes=passes, group=grp), 0)

    if n_chunks > 1:
        last = n_chunks - 1

        @pl.when(c < last)
        def _():
            run(1, group)

        @pl.when(c == last)
        def _():
            run(6, _pick(nq, (2, 1)))
    else:
        run(1, group)


def _scan(r, k, v, lw, a, g, prm, s0, *, row0, n_seq, t_len, hs):
    nq, _, W = r.shape
    t_blk = min(CHUNK, t_len)
    n_chunks = t_len // t_blk
    blk0 = row0 // t_blk
    row_spec = pl.BlockSpec((nq, t_blk, W), lambda b, c: (0, blk0 + b * n_chunks + c, 0))
    prm_spec = pl.BlockSpec((nq, 1, W), lambda b, c: (0, 0, 0))
    st_spec = pl.BlockSpec((None, nq, W, W), lambda b, c: (b, 0, 0, 0))
    return pl.pallas_call(
        functools.partial(_scan_kernel, t_blk=t_blk, hs=hs, group=_pick(nq, (SCAN_GROUP, 2, 1)),
                          n_chunks=n_chunks),
        grid=(n_seq, n_chunks),
        in_specs=[row_spec] * 6 + [prm_spec] * 5 + ([] if s0 is None else [st_spec]),
        out_specs=[pl.BlockSpec((nq, t_blk, W), lambda b, c: (0, b * n_chunks + c, 0)), st_spec],
        out_shape=[jax.ShapeDtypeStruct((nq, n_seq * t_len, W), BF16),
                   jax.ShapeDtypeStruct((n_seq, nq, W, W), F32)],
        compiler_params=_cparams(("arbitrary", "arbitrary")),
        name="rwkv_scan",
    )(r, k, v, lw, a, g, *prm, *(() if s0 is None else (s0,)))


def _pack_state(s, hs):
    n, H = s.shape[0], s.shape[1]
    s = s.reshape(n, H // QUAD, QUAD, hs, hs)
    eye = jnp.eye(QUAD, dtype=s.dtype)
    bd = s[:, :, :, :, None, :] * eye[None, None, :, None, :, None]
    return bd.reshape(n, H // QUAD, QUAD * hs, QUAD * hs)


def _unpack_state(bd, hs):
    n, nq = bd.shape[0], bd.shape[1]
    x = bd.reshape(n, nq, QUAD, hs, QUAD, hs)
    x = jnp.stack([x[:, :, j, :, j, :] for j in range(QUAD)], axis=2)
    return x.reshape(n, nq * QUAD, hs, hs)


def _pool_kernel(x_ref, xp_ref, hist_ref, g_ref, d_ref, hl_ref, *, lay):
    i = pl.program_id(0)
    g = g_ref[...]
    h = _rms(x_ref[...], g)
    tis = lay.tile_in_seq(i)
    prev = jnp.where(tis == 0, hist_ref[...], _rms(xp_ref[...], g))
    ext = jnp.concatenate([prev, h], axis=0)
    D = h.shape[1]
    gw = D // len(POOL_WINDOWS)
    pos0 = jnp.where(i < lay.n_tiles_p, 0, PAST_LEN) + tis * SEQ_TILE
    pos = pos0 + lax.broadcasted_iota(jnp.int32, (SEQ_TILE, 1), 0)
    for gi, w in enumerate(POOL_WINDOWS):
        e = ext[:, gi * gw:(gi + 1) * gw]
        s = e
        step = 1
        while step < w:
            s = s + pltpu.roll(s, step, axis=0)
            step *= 2
        cnt = jnp.minimum(pos + 1, w).astype(F32)
        mean = s[16:] / cnt
        d_ref[:, gi * gw:(gi + 1) * gw] = (mean - h[:, gi * gw:(gi + 1) * gw]).astype(BF16)
    hl_ref[...] = h


def _pool(x, hist16, g, lay):
    R, D = x.shape
    tp16 = SEQ_TILE // 16
    return pl.pallas_call(
        functools.partial(_pool_kernel, lay=lay),
        grid=(lay.n_tiles,),
        in_specs=[
            pl.BlockSpec((SEQ_TILE, D), lambda i: (i, 0)),
            pl.BlockSpec((16, D), lambda i: (jnp.maximum(i * tp16 - 1, 0), 0)),
            pl.BlockSpec((None, 16, D), lambda i: (lay.seq_of_tile(i), 0, 0)),
            pl.BlockSpec((1, D), lambda i: (0, 0)),
        ],
        out_specs=[pl.BlockSpec((SEQ_TILE, D), lambda i: (i, 0)),
                   pl.BlockSpec((None, SEQ_TILE, D), lambda i: (lay.seq_of_tile(i), 0, 0))],
        out_shape=[jax.ShapeDtypeStruct((R, D), BF16),
                   jax.ShapeDtypeStruct((lay.n_seq, SEQ_TILE, D), F32)],
        compiler_params=_cparams(("arbitrary",)),
        name="pool_mix",
    )(x, x, hist16, g)


def _router_kernel(x_ref, g_ref, w_ref, b_ref, hn_ref, wts_ref, eid_ref, cnt_ref, *, n_grp, per_grp):
    hn = _rms(x_ref[...], g_ref[...])
    hb = hn.astype(BF16)
    bits = lax.bitcast_convert_type(hb.astype(F32), jnp.uint32)
    for c in range(hn_ref.shape[1]):
        lo = bits[:, (2 * c) * LANES:(2 * c + 1) * LANES] >> 16
        hi = bits[:, (2 * c + 1) * LANES:(2 * c + 2) * LANES] & jnp.uint32(0xFFFF0000)
        hn_ref[:, c, :] = lo | hi
    logits = _dot(hn, w_ref[...], prec=lax.Precision.HIGHEST) + b_ref[...]
    lane = lax.broadcasted_iota(jnp.int32, logits.shape, 1).astype(F32)
    neg = -jnp.inf
    big = float(logits.shape[1])
    gl = jnp.where(lane < n_grp, logits, neg)
    gmax = jnp.max(gl, axis=-1, keepdims=True)
    gsum = jnp.sum(jnp.exp(gl - gmax), axis=-1, keepdims=True)
    g_idx = jnp.min(jnp.where(gl == gmax, lane, big), axis=-1, keepdims=True)
    g_gate = 1.0 / gsum
    lo = n_grp + g_idx * per_grp
    el = jnp.where((lane >= lo) & (lane < lo + per_grp), logits, neg)
    e1 = jnp.max(el, axis=-1, keepdims=True)
    i1 = jnp.min(jnp.where(el == e1, lane, big), axis=-1, keepdims=True)
    el2 = jnp.where(lane == i1, neg, el)
    e2 = jnp.max(el2, axis=-1, keepdims=True)
    i2 = jnp.min(jnp.where(el2 == e2, lane, big), axis=-1, keepdims=True)
    esum = jnp.sum(jnp.exp(el - e1), axis=-1, keepdims=True)
    p1 = 1.0 / esum
    p2 = jnp.exp(e2 - e1) / esum
    w1 = g_gate * p1 / (p1 + p2)
    w2 = g_gate * p2 / (p1 + p2)
    wts_ref[...] = jnp.where(lane == 0, w1, jnp.where(lane == 1, w2, 0.0))

    @pl.when(pl.program_id(0) == 0)
    def _():
        cnt_ref[...] = jnp.zeros(cnt_ref.shape, F32)

    e1 = i1 - n_grp
    e2 = i2 - n_grp
    oh1 = jnp.where(lane == e1, 1.0, 0.0)
    oh2 = jnp.where(lane == e2, 1.0, 0.0)
    both = oh1 + oh2
    tm = both.shape[0]
    before = (lax.broadcasted_iota(jnp.int32, (tm, tm), 1) < lax.broadcasted_iota(jnp.int32, (tm, tm), 0))
    seen = _dot(jnp.where(before, 1.0, 0.0).astype(BF16), both.astype(BF16)) + cnt_ref[...]
    rank1 = jnp.sum(oh1 * seen, axis=-1, keepdims=True)
    rank2 = jnp.sum(oh2 * seen, axis=-1, keepdims=True)
    cnt_ref[...] += jnp.sum(both, axis=0, keepdims=True)
    eid_ref[...] = jnp.where(lane == 0, e1, jnp.where(lane == 1, e2, jnp.where(
        lane == 2, rank1, jnp.where(lane == 3, rank2, 0.0)))).astype(jnp.int32)


def _router(x, g, w_cat, b_cat, n_grp, per_grp):
    R, D = x.shape
    tm = _pick(R, (256, 128, 64, 32, 16, 8))
    L = w_cat.shape[1]
    return pl.pallas_call(
        functools.partial(_router_kernel, n_grp=n_grp, per_grp=per_grp),
        grid=(R // tm,),
        in_specs=[
            pl.BlockSpec((tm, D), lambda m: (m, 0)),
            pl.BlockSpec((1, D), lambda m: (0, 0)),
            pl.BlockSpec((D, L), lambda m: (0, 0)),
            pl.BlockSpec((1, L), lambda m: (0, 0)),
        ],
        out_specs=[pl.BlockSpec((tm, D // (2 * LANES), LANES), lambda m: (m, 0, 0)),
                   pl.BlockSpec((tm, L), lambda m: (m, 0)),
                   pl.BlockSpec((tm, L), lambda m: (m, 0)),
                   pl.BlockSpec((1, L), lambda m: (0, 0))],
        out_shape=[jax.ShapeDtypeStruct((R, D // (2 * LANES), LANES), jnp.uint32),
                   jax.ShapeDtypeStruct((R, L), F32),
                   jax.ShapeDtypeStruct((R, L), jnp.int32),
                   jax.ShapeDtypeStruct((1, L), F32)],
        compiler_params=_cparams(("arbitrary",)),
        name="moe_router",
    )(x, g, w_cat, b_cat)


def _pitch(nc):
    return nc + 4


def _issue_rows(src_ref, row_of, n_rows, buf, sem):
    nc = src_ref.shape[1]

    def issue(j, carry):
        for par in range(2):
            r = 2 * j + par
            pltpu.make_async_copy(src_ref.at[row_of(r)], buf.at[pl.ds(r * _pitch(nc), nc)], sem).start(priority=par)
        return carry

    lax.fori_loop(0, n_rows // 2, issue, 0)


def _wait_rows(src_ref, n_rows, buf, sem):
    nc = src_ref.shape[1]

    def drain(r, carry):
        pltpu.make_async_copy(src_ref.at[0], buf.at[pl.ds(r * _pitch(nc), nc)], sem).wait()
        return carry

    lax.fori_loop(0, n_rows, drain, 0)


def _gather_kernel(idx_ref, nblk_ref, src_ref, o_ref, buf0, buf1, sem):
    G = o_ref.shape[0]
    nc = src_ref.shape[1]
    b = pl.program_id(0)
    nblk = nblk_ref[0]
    bufs = (buf0, buf1)

    def fetch(blk, slot):
        _issue_rows(src_ref, lambda r: idx_ref[blk * G + r], G, bufs[slot], sem.at[slot])

    @pl.when((b == 0) & (nblk > 0))
    def _():
        fetch(0, 0)

    for slot in range(2):
        @pl.when((b < nblk) & (b % 2 == slot))
        def _(slot=slot):
            @pl.when(b + 1 < nblk)
            def _():
                fetch(b + 1, 1 - slot)

            _wait_rows(src_ref, G, bufs[slot], sem.at[slot])
            for cc in range(nc):
                w = bufs[slot][pl.ds(cc, G, stride=_pitch(nc)), :]
                lo = lax.bitcast_convert_type(w << 16, F32)
                hi = lax.bitcast_convert_type(w & jnp.uint32(0xFFFF0000), F32)
                o_ref[:, (2 * cc) * LANES:(2 * cc + 1) * LANES] = lo.astype(BF16)
                o_ref[:, (2 * cc + 1) * LANES:(2 * cc + 2) * LANES] = hi.astype(BF16)

    @pl.when(b >= nblk_ref[0])
    def _():
        o_ref[...] = jnp.zeros(o_ref.shape, o_ref.dtype)


def _gather_rows(src3, idx, nblk):
    n = idx.shape[0]
    _, nc, lanes = src3.shape
    G = MOE_ROWS
    assert n % G == 0 and lanes == LANES and src3.dtype == jnp.uint32
    return pl.pallas_call(
        _gather_kernel,
        grid_spec=pltpu.PrefetchScalarGridSpec(
            num_scalar_prefetch=2,
            grid=(n // G,),
            in_specs=[pl.BlockSpec(memory_space=pl.ANY)],
            out_specs=pl.BlockSpec((G, 2 * nc * LANES), lambda b, idx_r, nb_r: (b, 0)),
            scratch_shapes=[pltpu.VMEM((G * _pitch(nc), LANES), jnp.uint32),
                            pltpu.VMEM((G * _pitch(nc), LANES), jnp.uint32),
                            pltpu.SemaphoreType.DMA((2,))],
        ),
        out_shape=jax.ShapeDtypeStruct((n, 2 * nc * LANES), BF16),
        compiler_params=_cparams(("arbitrary",)),
        name="row_gather",
    )(idx, nblk, src3)


def _new_expert(blk_e_ref, b):
    prev = blk_e_ref[jnp.maximum(b - 1, 0)]
    return (b == 0) | (blk_e_ref[b] != prev)


def _next_weights(plan, w_hbm, w_f32, w_bf, sem, layer):
    blk_e_ref, _, run_ref, next_ref, nruns_ref = plan
    s, b = pl.program_id(0), pl.program_id(1)
    e = blk_e_ref[b]
    ts = w_bf[0].shape[1]

    def copies(ee, ss, slot):
        return [pltpu.make_async_copy(src.at[layer, ee, :, pl.ds(ss * ts, ts)], dst.at[slot], sem.at[i, slot])
                for i, (src, dst) in enumerate(zip(w_hbm, w_f32))]

    slot = (run_ref[e] + s * nruns_ref[0]) % 2

    @pl.when((s == 0) & (b == 0))
    def _():
        for cp in copies(e, s, slot):
            cp.start()

    e_next = next_ref[e]
    s_next = s + (e_next <= e).astype(jnp.int32)

    @pl.when(s_next < pl.num_programs(0))
    def _():
        for cp in copies(e_next, s_next, 1 - slot):
            cp.start()

    for cp in copies(e, s, slot):
        cp.wait()
    for dst, src in zip(w_bf, w_f32):
        dst[...] = src[slot].astype(BF16)


def _moe_up_kernel(*refs, layer):
    plan, (x_ref, wg_hbm, wu_hbm, h_ref, wg_f32, wu_f32, wg_bf, wu_bf, sem) = refs[:5], refs[5:]
    blk_e_ref, nblk_ref = plan[0], plan[1]
    b = pl.program_id(1)

    @pl.when(b < nblk_ref[0])
    def _():
        @pl.when(_new_expert(blk_e_ref, b))
        def _():
            _next_weights(plan, (wg_hbm, wu_hbm), (wg_f32, wu_f32), (wg_bf, wu_bf), sem, layer)

        x = x_ref[...]
        gate = _dot(x, wg_bf[...])
        up = _dot(x, wu_bf[...])
        h_ref[...] = (jax.nn.silu(gate) * up).astype(BF16)

    @pl.when(b >= nblk_ref[0])
    def _():
        h_ref[...] = jnp.zeros(h_ref.shape, BF16)


def _moe_up(xg, w_gate, w_up, layer, plan):
    P, D = xg.shape
    De = w_gate.shape[3]
    tj = _pick(De, (512, 256, 128))
    nb = P // MOE_ROWS
    any_spec = pl.BlockSpec(memory_space=pl.ANY)
    return pl.pallas_call(
        functools.partial(_moe_up_kernel, layer=layer),
        grid_spec=pltpu.PrefetchScalarGridSpec(
            num_scalar_prefetch=len(plan),
            grid=(De // tj, nb),
            in_specs=[pl.BlockSpec((MOE_ROWS, D), lambda j, b, *_: (b, 0)), any_spec, any_spec],
            out_specs=pl.BlockSpec((MOE_ROWS, tj), lambda j, b, *_: (b, j)),
            scratch_shapes=[pltpu.VMEM((2, D, tj), F32), pltpu.VMEM((2, D, tj), F32),
                            pltpu.VMEM((D, tj), BF16), pltpu.VMEM((D, tj), BF16),
                            pltpu.SemaphoreType.DMA((2, 2))],
        ),
        out_shape=jax.ShapeDtypeStruct((P, De), BF16),
        compiler_params=_cparams(("arbitrary", "arbitrary")),
        name="moe_up",
    )(*plan, xg, w_gate, w_up)


def _moe_down_kernel(*refs, layer):
    plan, (h_ref, wd_hbm, y_ref, wd_f32, wd_bf, sem) = refs[:5], refs[5:]
    blk_e_ref, nblk_ref = plan[0], plan[1]
    b = pl.program_id(1)

    @pl.when(b < nblk_ref[0])
    def _():
        @pl.when(_new_expert(blk_e_ref, b))
        def _():
            _next_weights(plan, (wd_hbm,), (wd_f32,), (wd_bf,), sem, layer)

        y = _dot(h_ref[...], wd_bf[...])
        for cc in range(y_ref.shape[1]):
            y_ref[:, cc, :] = y[:, cc * LANES:(cc + 1) * LANES]

    @pl.when(b >= nblk_ref[0])
    def _():
        y_ref[...] = jnp.zeros(y_ref.shape, F32)


def _moe_down(h, w_down, layer, plan):
    P, De = h.shape
    D = w_down.shape[3]
    tn = _pick(D, (2048, 1024, 512, 256, 128))
    nb = P // MOE_ROWS
    return pl.pallas_call(
        functools.partial(_moe_down_kernel, layer=layer),
        grid_spec=pltpu.PrefetchScalarGridSpec(
            num_scalar_prefetch=len(plan),
            grid=(D // tn, nb),
            in_specs=[pl.BlockSpec((MOE_ROWS, De), lambda n, b, *_: (b, 0)),
                      pl.BlockSpec(memory_space=pl.ANY)],
            out_specs=pl.BlockSpec((MOE_ROWS, tn // LANES, LANES), lambda n, b, *_: (b, n, 0)),
            scratch_shapes=[pltpu.VMEM((2, De, tn), F32), pltpu.VMEM((De, tn), BF16),
                            pltpu.SemaphoreType.DMA((1, 2))],
        ),
        out_shape=jax.ShapeDtypeStruct((P, D // LANES, LANES), F32),
        compiler_params=_cparams(("arbitrary", "arbitrary")),
        name="moe_down",
    )(*plan, h, w_down)


def _dispatch_plan(eid, rank, counts):
    R = eid.shape[0]
    S = R * TOP_K
    n_experts = counts.shape[0]
    e_flat = eid.reshape(-1)
    padded = (counts + MOE_ROWS - 1) // MOE_ROWS * MOE_ROWS
    pad_end = jnp.cumsum(padded)
    dest = ((pad_end - padded)[e_flat] + rank.reshape(-1)).astype(jnp.int32)
    nb = -(-S // MOE_ROWS) + n_experts
    buf_tok = jnp.zeros((nb * MOE_ROWS,), jnp.int32).at[dest].set(jnp.arange(S, dtype=jnp.int32) // TOP_K)
    blk_e = jnp.minimum(jnp.searchsorted(pad_end, jnp.arange(nb, dtype=jnp.int32) * MOE_ROWS, side='right'),
                        n_experts - 1).astype(jnp.int32)
    nblk = (pad_end[-1:] // MOE_ROWS).astype(jnp.int32)
    ids = jnp.arange(n_experts, dtype=jnp.int32)
    live = counts > 0
    later = jnp.where(live[None, :] & (ids[None, :] > ids[:, None]), ids[None, :], n_experts)
    nxt = jnp.min(later, axis=1)
    next_e = jnp.where(nxt == n_experts, jnp.argmax(live), nxt).astype(jnp.int32)
    run_idx = (jnp.cumsum(live) - 1).astype(jnp.int32)
    n_runs = jnp.sum(live).astype(jnp.int32)[None]
    return dest, buf_tok, (blk_e, nblk, run_idx, next_e, n_runs)


def _ple_kernel(slot_ref, x_ref, y_hbm, wts_ref, p_ref, g_ref, gd_ref, gu_ref, win_ref, gf_ref, *rest,
                final, head_tiles, n_tok):
    n_out = 2 if final else 1
    o_refs = rest[:n_out]
    buf0, buf1, ysel, sem = rest[n_out:]
    bufs = (buf0, buf1)
    m = pl.program_id(0)
    tm = x_ref.shape[0]
    nc = y_hbm.shape[1]

    def fetch(tile, slot):
        _issue_rows(y_hbm, lambda r: slot_ref[(r // tm) * n_tok + tile * tm + r % tm], TOP_K * tm,
                    bufs[slot], sem.at[slot])

    @pl.when(m == 0)
    def _():
        fetch(0, 0)

    for slot in range(2):
        @pl.when(m % 2 == slot)
        def _(slot=slot):
            @pl.when(m + 1 < pl.num_programs(0))
            def _():
                fetch(m + 1, 1 - slot)

            _wait_rows(y_hbm, TOP_K * tm, bufs[slot], sem.at[slot])
            for k in range(TOP_K):
                for cc in range(nc):
                    ysel[k, :, cc * LANES:(cc + 1) * LANES] = (
                        bufs[slot][pl.ds(k * tm * _pitch(nc) + cc, tm, stride=_pitch(nc)), :])

    wts = wts_ref[...]
    x = x_ref[...]
    for k in range(TOP_K):
        x = x + ysel[k] * wts[:, k:k + 1]
    hn = _rms(x, g_ref[...]).astype(BF16)
    t = _dot(hn, gd_ref[...]).astype(BF16)
    gate = jax.nn.sigmoid(_dot(t, gu_ref[...]))
    pe = _dot(p_ref[...], win_ref[...])
    x = x + pe * gate
    if not final:
        o_refs[0][...] = x
        return
    x = _rms(x, gf_ref[...])
    m = pl.program_id(0)

    @pl.when(m < head_tiles)
    def _():
        o_refs[0][...] = x

    @pl.when(m >= head_tiles)
    def _():
        o_refs[1][...] = x


def _ple(x, y3, slot_rows, wts, p, g, gd, gu, win, gf, final, head_rows):
    R, D = x.shape
    Pd = p.shape[1]
    L = wts.shape[1]
    nc = y3.shape[1]
    tm = _pick(R, (128, 64, 32, 16, 8))
    nt = R // tm
    row = pl.BlockSpec((tm, D), lambda m, sl: (m, 0))
    fixed = lambda shape: pl.BlockSpec(shape, lambda m, sl: (0, 0))
    assert head_rows % tm == 0
    ht = head_rows // tm
    if final:
        out_specs = [pl.BlockSpec((tm, D), lambda m, sl: (jnp.minimum(m, ht - 1), 0)),
                     pl.BlockSpec((tm, D), lambda m, sl: (jnp.maximum(m - ht, 0), 0))]
        out_shape = [jax.ShapeDtypeStruct((head_rows, D), F32), jax.ShapeDtypeStruct((R - head_rows, D), F32)]
    else:
        out_specs = row
        out_shape = jax.ShapeDtypeStruct((R, D), F32)
    stage = pltpu.VMEM((TOP_K * tm * _pitch(nc), LANES), F32)
    return pl.pallas_call(
        functools.partial(_ple_kernel, final=final, head_tiles=ht, n_tok=R),
        grid_spec=pltpu.PrefetchScalarGridSpec(
            num_scalar_prefetch=1,
            grid=(nt,),
            in_specs=[
                row,
                pl.BlockSpec(memory_space=pl.ANY),
                pl.BlockSpec((tm, L), lambda m, sl: (m, 0)),
                pl.BlockSpec((tm, Pd), lambda m, sl: (m, 0)),
                fixed((1, D)), fixed((D, Pd)), fixed((Pd, D)), fixed((Pd, D)), fixed((1, D)),
            ],
            out_specs=out_specs,
            scratch_shapes=[stage, stage, pltpu.VMEM((TOP_K, tm, D), F32), pltpu.SemaphoreType.DMA((2,))],
        ),
        out_shape=out_shape,
        compiler_params=_cparams(("arbitrary",)),
        name="moe_combine_ple",
    )(slot_rows, x, y3, wts, p, g, gd, gu, win, gf)


def _moe_ple(x, p, layer, prm, final, head_rows):
    R, D = x.shape
    assert (R * TOP_K) % MOE_ROWS == 0 and D % LANES == 0
    n_grp = prm["moe_w_grp"].shape[2]
    n_exp = prm["moe_w_exp"].shape[2]
    L = LANES
    w_cat = jnp.concatenate([prm["moe_w_grp"][layer], prm["moe_w_exp"][layer],
                             jnp.zeros((D, L - n_grp - n_exp), F32)], axis=1)
    b_cat = jnp.concatenate([prm["moe_b_grp"][layer], prm["moe_b_exp"][layer],
                             jnp.zeros((L - n_grp - n_exp,), F32)])[None, :]
    hn, wts, route, cnt = _router(x, prm["norm_ffn"][layer][None, :], w_cat, b_cat, n_grp, n_exp // n_grp)
    dest, buf_tok, plan = _dispatch_plan(
        route[:, :TOP_K], route[:, TOP_K:2 * TOP_K], cnt[0, :n_exp].astype(jnp.int32))
    xg = _gather_rows(hn, buf_tok, plan[1])
    hmid = _moe_up(xg, prm["moe_w_gate"], prm["moe_w_up"], layer, plan)
    y3 = _moe_down(hmid, prm["moe_w_down"], layer, plan)
    slot_rows = dest.reshape(R, TOP_K).T.reshape(-1)
    return _ple(x, y3, slot_rows, wts, p, prm["norm_ple"][layer][None, :],
                prm["ple_gate_down"][layer].astype(BF16), prm["ple_gate_up"][layer].astype(BF16),
                prm["ple_w_in"][layer].astype(BF16), prm["norm_final"][None, :], final, head_rows)


def kernel(x_prompt, x_sample, state_rwkv_shift, state_rwkv_wkv, state_pool, p_prompt, p_sample,
           rwkv_mu, rwkv_w_rkv, rwkv_w_o, rwkv_w0, rwkv_w1, rwkv_w2, rwkv_a0, rwkv_a1, rwkv_a2,
           rwkv_g1, rwkv_g2, rwkv_k_k, rwkv_k_a, rwkv_r_k, rwkv_lnx_w, rwkv_lnx_b,
           pool_w, pool_scale, norm_mix, norm_ffn, norm_ple, norm_final,
           moe_w_grp, moe_b_grp, moe_w_exp, moe_b_exp, moe_w_gate, moe_w_up, moe_w_down,
           ple_w_in, ple_gate_down, ple_gate_up):
    Bp, Tp, D = x_prompt.shape
    Bs, Ts, _ = x_sample.shape
    depth = norm_mix.shape[0]
    H, hs = rwkv_r_k.shape[1], rwkv_r_k.shape[2]
    W = QUAD * hs
    assert D % W == 0 and Ts <= CHUNK and Tp % CHUNK == 0
    lay = _SeqLayout(Bp, Tp, Bs, Ts)
    Rp = Bp * Tp
    prm = dict(moe_w_grp=moe_w_grp, moe_b_grp=moe_b_grp, moe_w_exp=moe_w_exp, moe_b_exp=moe_b_exp,
               moe_w_gate=moe_w_gate, moe_w_up=moe_w_up, moe_w_down=moe_w_down, norm_ffn=norm_ffn,
               norm_ple=norm_ple, norm_final=norm_final, ple_w_in=ple_w_in, ple_gate_down=ple_gate_down,
               ple_gate_up=ple_gate_up)

    x = jnp.concatenate([x_prompt.reshape(Rp, D), x_sample.reshape(Bs * Ts, D)], axis=0)
    Pd = p_prompt.shape[-1]
    p_all = jnp.concatenate([p_prompt.reshape(depth, Rp, Pd), p_sample.reshape(depth, Bs * Ts, Pd)],
                            axis=1).astype(BF16)

    def slabs(vec):
        return vec.reshape(D // W, 1, W)

    shift_p, wkv_p, pool_p, shift_s, wkv_s, pool_s = [], [], [], [], [], []
    for i in range(depth):
        j = i // 2
        g_mix = norm_mix[i][None, :]
        if i % 2 == 0:
            shift_all = jnp.concatenate([jnp.zeros((Bp, D), F32), state_rwkv_shift[j]], axis=0)[:, None, :]
            mixed, h_last = _norm_mix(x, shift_all, g_mix, rwkv_mu[j], lay)
            xr, xw, xk, xv, xa, xg = mixed
            w_rkv = rwkv_w_rkv[j].astype(BF16)
            r = _mm(xr, w_rkv[0:1], slab_out=W, name="proj_r")
            k = _mm(xk, w_rkv[1:2], slab_out=W, name="proj_k")
            v = _mm(xv, w_rkv[2:3], slab_out=W, name="proj_v")
            lw = _lora(xw, rwkv_w1[j].astype(BF16), rwkv_w2[j].astype(BF16), rwkv_w0[j][None, :], "decay", W)
            a = _lora(xa, rwkv_a1[j].astype(BF16), rwkv_a2[j].astype(BF16), rwkv_a0[j][None, :], "aaa", W)
            gl = rwkv_g1.shape[2]
            glp = -(-gl // LANES) * LANES
            g1 = jnp.pad(rwkv_g1[j], ((0, 0), (0, glp - gl))).astype(BF16)
            g2 = jnp.pad(rwkv_g2[j], ((0, glp - gl), (0, 0))).astype(BF16)
            g = _lora(xg, g1, g2, jnp.zeros((1, D), F32), "gate", W)
            sprm = (slabs(rwkv_k_k[j]), slabs(rwkv_k_a[j]), slabs(rwkv_r_k[j].reshape(D)),
                    slabs(rwkv_lnx_w[j]), slabs(rwkv_lnx_b[j]))
            o_p, sp = _scan(r, k, v, lw, a, g, sprm, None, row0=0, n_seq=Bp, t_len=Tp, hs=hs)
            o_s, ss = _scan(r, k, v, lw, a, g, sprm, _pack_state(state_rwkv_wkv[j], hs),
                            row0=Rp, n_seq=Bs, t_len=Ts, hs=hs)
            o = jnp.concatenate([o_p, o_s], axis=1)
            x = _mm(o, rwkv_w_o[j].astype(BF16)[None], res=x, slab_in=True, name="proj_o")
            shift_p.append(h_last[:Bp, -1])
            shift_s.append(h_last[Bp:, -1])
            wkv_p.append(_unpack_state(sp, hs))
            wkv_s.append(_unpack_state(ss, hs))
        else:
            hist = jnp.concatenate([jnp.zeros((Bp, 16, D), F32),
                                    jnp.pad(state_pool[j], ((0, 0), (1, 0), (0, 0)))], axis=0)
            d, h_last = _pool(x, hist, g_mix, lay)
            x = _mm(d, pool_w[j].astype(BF16), scale=pool_scale[j][None, :], res=x, name="pool_proj")
            nh = state_pool.shape[2]
            pool_p.append(h_last[:Bp, SEQ_TILE - nh:])
            pool_s.append(h_last[Bp:, SEQ_TILE - nh:])
        x = _moe_ple(x, p_all[i], i, prm, final=(i == depth - 1), head_rows=Rp)

    y_prompt = x[0].reshape(Bp, Tp, D)
    y_sample = x[1].reshape(Bs, Ts, D)
    return (y_prompt, y_sample, jnp.stack(shift_p), jnp.stack(wkv_p), jnp.stack(pool_p),
            jnp.stack(shift_s), jnp.stack(wkv_s), jnp.stack(pool_s))
```

```python
import functools

import jax
import jax.numpy as jnp
from jax import lax
from jax.experimental import pallas as pl
from jax.experimental.pallas import tpu as pltpu

F32 = jnp.float32
BF16 = jnp.bfloat16

NORM_EPS = 1e-6
GN_EPS = 64e-5
PAST_LEN = 4096
POOL_WINDOWS = (2, 4, 8, 16)
TOP_K = 2
LANES = 128
SEQ_TILE = 32
CHUNK = 64
QUAD = 4
SCAN_GROUP = 4
MOE_ROWS = 256
VMEM_LIMIT = 56 * 1024 * 1024


def _cparams(sem):
    return pltpu.CompilerParams(dimension_semantics=sem, vmem_limit_bytes=VMEM_LIMIT)


def _pick(n, cands):
    for c in cands:
        if n % c == 0:
            return c
    return n


def _rms(x, g):
    return x * lax.rsqrt(jnp.mean(x * x, axis=-1, keepdims=True) + NORM_EPS) * g


_NN = (((1,), (0,)), ((), ()))
_NT = (((1,), (1,)), ((), ()))
_TN = (((0,), (0,)), ((), ()))


def _dot(a, b, dims=_NN, prec=None):
    return lax.dot_general(a, b, dims, precision=prec, preferred_element_type=F32)


def _mm_kernel(*refs, has_res, has_scale, a_slabs, o_slabs):
    a_ref, w_ref = refs[0], refs[1]
    k = 2
    if a_slabs:
        a = jnp.concatenate([a_ref[q] for q in range(a_slabs)], axis=1)
    else:
        a = a_ref[...]
    acc = _dot(a, w_ref[...])
    if has_scale:
        acc = acc * refs[k][...]
        k += 1
    if has_res:
        acc = refs[k][...] + acc
        k += 1
    o_ref = refs[k]
    if o_slabs:
        sw = o_ref.shape[2]
        for q in range(o_slabs):
            o_ref[q] = acc[:, q * sw:(q + 1) * sw].astype(o_ref.dtype)
    else:
        o_ref[...] = acc.astype(o_ref.dtype)


def _mm(a, w, *, res=None, scale=None, out_dtype=F32, slab_in=False, slab_out=0, w_group=None, name="mm"):
    M = a.shape[1] if slab_in else a.shape[0]
    G, Kg, Ng = w.shape
    w0 = 0
    if w_group is not None:
        G, w0 = 1, w_group
    tm = _pick(M, (512, 256, 128, 64, 32, 16, 8))
    tn = _pick(Ng, (1024, 512, 256, 128))
    nn = Ng // tn
    if slab_in:
        assert G == 1
        a_spec = pl.BlockSpec((a.shape[0], tm, a.shape[2]), lambda g, n, m: (0, m, 0))
    else:
        a_spec = pl.BlockSpec((tm, Kg), lambda g, n, m: (m, g))
    in_specs = [a_spec, pl.BlockSpec((None, Kg, tn), lambda g, n, m: (w0 + g, 0, n))]
    args = [a, w]
    if scale is not None:
        in_specs.append(pl.BlockSpec((1, tn), lambda g, n, m: (0, g * nn + n)))
        args.append(scale)
    if res is not None:
        in_specs.append(pl.BlockSpec((tm, tn), lambda g, n, m: (m, g * nn + n)))
        args.append(res)
    if slab_out:
        assert tn % slab_out == 0
        per = tn // slab_out
        out_spec = pl.BlockSpec((per, tm, slab_out), lambda g, n, m: (g * nn + n, m, 0))
        out_shape = jax.ShapeDtypeStruct((G * Ng // slab_out, M, slab_out), out_dtype)
    else:
        per = 0
        out_spec = pl.BlockSpec((tm, tn), lambda g, n, m: (m, g * nn + n))
        out_shape = jax.ShapeDtypeStruct((M, G * Ng), out_dtype)
    return pl.pallas_call(
        functools.partial(_mm_kernel, has_res=res is not None, has_scale=scale is not None,
                          a_slabs=a.shape[0] if slab_in else 0, o_slabs=per),
        grid=(G, nn, M // tm),
        in_specs=in_specs,
        out_specs=out_spec,
        out_shape=out_shape,
        compiler_params=_cparams(("arbitrary", "arbitrary", "arbitrary")),
        name=name,
    )(*args)


def _lora_kernel(a_ref, w1_ref, w2_ref, b_ref, o_ref, *, mode):
    t = _dot(a_ref[...], w1_ref[...])
    if mode == "decay":
        t = jnp.tanh(t)
    elif mode == "gate":
        t = jax.nn.sigmoid(t)
    z = _dot(t.astype(BF16), w2_ref[...]) + b_ref[...]
    if mode == "decay":
        z = -jnp.exp(-jax.nn.softplus(-z) - 0.5)
    elif mode == "aaa":
        z = jax.nn.sigmoid(z)
    sw = o_ref.shape[2]
    for q in range(o_ref.shape[0]):
        o_ref[q] = z[:, q * sw:(q + 1) * sw]


def _lora(a, w1, w2, bias, mode, sw):
    M, K = a.shape
    L = w1.shape[1]
    D = w2.shape[1]
    tm = _pick(M, (256, 128, 64, 32, 16, 8))
    return pl.pallas_call(
        functools.partial(_lora_kernel, mode=mode),
        grid=(M // tm,),
        in_specs=[
            pl.BlockSpec((tm, K), lambda m: (m, 0)),
            pl.BlockSpec((K, L), lambda m: (0, 0)),
            pl.BlockSpec((L, D), lambda m: (0, 0)),
            pl.BlockSpec((1, D), lambda m: (0, 0)),
        ],
        out_specs=pl.BlockSpec((D // sw, tm, sw), lambda m: (0, m, 0)),
        out_shape=jax.ShapeDtypeStruct((D // sw, M, sw), F32),
        compiler_params=_cparams(("arbitrary",)),
        name="lora_" + mode,
    )(a, w1, w2, bias)


class _SeqLayout:
    def __init__(self, bp, tp, bs, ts):
        assert tp % SEQ_TILE == 0 and ts % SEQ_TILE == 0
        self.bp, self.tp, self.bs, self.ts = bp, tp, bs, ts
        self.tiles_p = tp // SEQ_TILE
        self.tiles_s = ts // SEQ_TILE
        self.n_tiles_p = bp * self.tiles_p
        self.n_tiles = self.n_tiles_p + bs * self.tiles_s
        self.n_seq = bp + bs
        self.rows = bp * tp + bs * ts

    def seq_of_tile(self, i):
        return jnp.where(i < self.n_tiles_p, i // self.tiles_p,
                         self.bp + (i - self.n_tiles_p) // self.tiles_s)

    def tile_in_seq(self, i):
        return jnp.where(i < self.n_tiles_p, i % self.tiles_p, (i - self.n_tiles_p) % self.tiles_s)


def _norm_mix_kernel(x_ref, xp_ref, sh_ref, g_ref, mu_ref, *outs, lay):
    i = pl.program_id(0)
    g = g_ref[...]
    h = _rms(x_ref[...], g)
    h_before = _rms(xp_ref[7:8, :], g)
    first = lay.tile_in_seq(i) == 0
    prev_row = jnp.where(first, sh_ref[...], h_before)
    rows = lax.broadcasted_iota(jnp.int32, h.shape, 0)
    h_prev = jnp.where(rows == 0, prev_row, pltpu.roll(h, 1, axis=0))
    xx = h_prev - h
    for n in range(6):
        outs[n][...] = (h + xx * mu_ref[n:n + 1, :]).astype(BF16)
    outs[6][...] = h


def _norm_mix(x, shift_all, g, mu, lay):
    R, D = x.shape
    tpb = SEQ_TILE // 8
    row_spec = pl.BlockSpec((SEQ_TILE, D), lambda i: (i, 0))
    outs = pl.pallas_call(
        functools.partial(_norm_mix_kernel, lay=lay),
        grid=(lay.n_tiles,),
        in_specs=[
            row_spec,
            pl.BlockSpec((8, D), lambda i: (jnp.maximum(i * tpb - 1, 0), 0)),
            pl.BlockSpec((None, 1, D), lambda i: (lay.seq_of_tile(i), 0, 0)),
            pl.BlockSpec((1, D), lambda i: (0, 0)),
            pl.BlockSpec((6, D), lambda i: (0, 0)),
        ],
        out_specs=[row_spec] * 6 + [pl.BlockSpec((None, SEQ_TILE, D), lambda i: (lay.seq_of_tile(i), 0, 0))],
        out_shape=[jax.ShapeDtypeStruct((R, D), BF16)] * 6
        + [jax.ShapeDtypeStruct((lay.n_seq, SEQ_TILE, D), F32)],
        compiler_params=_cparams(("arbitrary",)),
        name="norm_mix",
    )(x, x, shift_all, g, mu)
    return outs[:6], outs[6]


def _scan_kernel(r_ref, k_ref, v_ref, lw_ref, a_ref, g_ref, kk_ref, ka_ref, rk_ref, lnw_ref, lnb_ref,
                 *rest, t_blk, hs, group, n_chunks):
    s0_ref = rest[0] if len(rest) == 4 else None
    o_ref, sout_ref, s_ref = rest[-3:]
    C = CHUNK
    nq, _, W = r_ref.shape
    c = pl.program_id(1)

    def head_block(q, j):
        return (q, slice(j * hs, (j + 1) * hs), slice(j * hs, (j + 1) * hs))

    @pl.when(c == 0)
    def _():
        s_ref[...] = jnp.zeros(s_ref.shape, F32)
        if s0_ref is not None:
            for q in range(nq):
                for j in range(QUAD):
                    s_ref[head_block(q, j)] = s0_ref[q * QUAD + j]

    ri = lax.broadcasted_iota(jnp.int32, (W, W), 0)
    ci = lax.broadcasted_iota(jnp.int32, (W, W), 1)
    head_blk = (ri // hs) == (ci // hs)
    ones_blk = jnp.where(head_blk, 1.0, 0.0).astype(F32)
    ti = lax.broadcasted_iota(jnp.int32, (C, C), 0)
    si = lax.broadcasted_iota(jnp.int32, (C, C), 1)
    tri_incl = jnp.where(ti >= si, 1.0, 0.0).astype(F32)
    lane_head = lax.broadcasted_iota(jnp.int32, (C, W), 1) // hs
    rj = lax.broadcasted_iota(jnp.int32, (QUAD * C, QUAD * C), 0)
    cj = lax.broadcasted_iota(jnp.int32, (QUAD * C, QUAD * C), 1)
    t_in = rj % C
    s_in = cj % C
    strict = jnp.where(t_in > s_in, 1.0, 0.0).astype(F32)
    incl = jnp.where(t_in >= s_in, 1.0, 0.0).astype(F32)
    eye = jnp.where(rj == cj, 1.0, 0.0).astype(F32)

    ones_bf = ones_blk.astype(BF16)
    head_masks = [jnp.where(lane_head == h, 1.0, 0.0).astype(BF16) for h in range(QUAD)]

    def fold(x):
        y = x[0:C]
        for h in range(1, QUAD):
            y = y + x[h * C:(h + 1) * C]
        return y

    def group_body(it, carry, *, passes, group):
        qs = [it * group + j for j in range(group)]
        each = lambda f, *xs: [f(*t) for t in zip(*xs)]

        full = passes == 6

        def parts(x):
            if full:
                return (x,)
            hi = x.astype(BF16)
            if passes == 1:
                return (hi,)
            return (hi, (x - hi.astype(F32)).astype(BF16))

        def bdot(x, y, dims=_NN):
            if full:
                return _dot(x[0], y[0], dims, lax.Precision.HIGHEST)
            out = _dot(x[0], y[0], dims)
            if passes > 1:
                out = out + (_dot(x[0], y[1], dims) + _dot(x[1], y[0], dims))
            return out

        def head_sum(x):
            if full:
                return _dot(x, ones_blk, _NN, lax.Precision.HIGHEST)
            out = _dot(parts(x)[0], ones_bf)
            if passes > 1:
                out = out + _dot(parts(x)[1], ones_bf)
            return out

        def stack(x):
            return tuple(jnp.concatenate([xb * m for m in head_masks], axis=0) for xb in parts(x))

        def load(ref):
            def one(q):
                x = ref[q]
                if t_blk < C:
                    x = jnp.concatenate([x, jnp.zeros((C - t_blk, W), F32)], axis=0)
                return x
            return [one(q) for q in qs]

        r, k, v, lw, a = load(r_ref), load(k_ref), load(v_ref), load(lw_ref), load(a_ref)
        kk = each(lambda k_, q: k_ * kk_ref[q], k, qs)
        kk = each(lambda x: x * lax.rsqrt(jnp.maximum(head_sum(x * x), 1e-24)), kk)
        kf = each(lambda k_, a_, q: k_ * (1.0 + (a_ - 1.0) * ka_ref[q]), k, a, qs)
        bv = each(lambda x, a_: x * a_, kk, a)

        cum = each(lambda x: _dot(tri_incl, x, _NN, lax.Precision.HIGHEST), lw)
        e_neg = each(lambda x: jnp.exp(-x), cum)
        rt = each(lambda r_, x: r_ * jnp.exp(x), r, cum)
        at = each(lambda x, c_, l_: -x * jnp.exp(c_ - l_), kk, cum, lw)
        kt = each(lambda x, e: x * e, kf, e_neg)
        bt = each(lambda x, e: x * e, bv, e_neg)
        p_c = each(lambda x: jnp.exp(x[C - 1:C, :]), cum)

        a_st, r_st, b_st, k_st, v_st = (each(stack, x) for x in (at, rt, bt, kt, v))
        a_ab = each(lambda x, y: bdot(x, y, _NT) * strict, a_st, b_st)
        a_ak = each(lambda x, y: parts(bdot(x, y, _NT) * strict), a_st, k_st)
        a_rb = each(lambda x, y: parts(bdot(x, y, _NT) * incl), r_st, b_st)
        a_rk = each(lambda x, y: parts(bdot(x, y, _NT) * incl), r_st, k_st)

        t_inv = each(lambda x: eye + x, a_ab)
        n_pow = each(parts, a_ab)
        p = 1
        while 2 * p < C:
            n_pow = each(lambda x: parts(bdot(x, x)), n_pow)
            t_inv = each(lambda t, n: t + bdot(parts(t), n), t_inv, n_pow)
            p *= 2
        t_bf = each(parts, t_inv)

        s = [s_ref[q] for q in qs]
        s_bf = each(parts, s)
        u_in = each(lambda a_, s_, ak, v_: parts(bdot(a_, s_, _NT) + bdot(ak, v_)), a_st, s_bf, a_ak, v_st)
        u_st = each(bdot, t_bf, u_in)
        o_st = each(lambda r_, s_, rb, u_, rk, v_: bdot(r_, s_, _NT) + bdot(rb, parts(u_)) + bdot(rk, v_),
                    r_st, s_bf, a_rb, u_st, a_rk, v_st)
        o = each(fold, o_st)
        u = each(fold, u_st)
        upd = each(lambda u_, v_, b_, k_, pc: bdot(parts(jnp.concatenate([u_, v_], axis=0)),
                                                   parts(jnp.concatenate([b_ * pc, k_ * pc], axis=0)), _TN),
                   u, v, bt, kt, p_c)
        for q, s_, pc, up in zip(qs, s, p_c, upd):
            s_ref[q] = s_ * pc + jnp.where(head_blk, up, 0.0)

        inv_n = 1.0 / hs
        o_c = each(lambda x: x - head_sum(x) * inv_n, o)
        o_n = each(lambda x: x * lax.rsqrt(head_sum(x * x) * inv_n + GN_EPS), o_c)
        bonus = each(lambda r_, k_, q: head_sum(r_ * k_ * rk_ref[q]), r, kf, qs)
        for q, x, bo, v_ in zip(qs, o_n, bonus, v):
            y = x * lnw_ref[q] + lnb_ref[q] + bo * v_
            o_ref[q] = (y[:t_blk] * g_ref[q]).astype(o_ref.dtype)
        return carry

    def run(passes, grp):
        lax.fori_loop(0, nq // grp, functools.partial(group_body, passes=passes, group=grp), 0)

    if n_chunks > 1:
        last = n_chunks - 1

        @pl.when(c < last)
        def _():
            run(1, group)

        @pl.when(c == last)
        def _():
            run(6, _pick(nq, (2, 1)))
    else:
        run(1, group)

    @pl.when(c == n_chunks - 1)
    def _():
        for q in range(nq):
            for j in range(QUAD):
                sout_ref[q * QUAD + j] = s_ref[head_block(q, j)]


def _scan(r, k, v, lw, a, g, prm, s0, *, row0, n_seq, t_len, hs):
    nq, _, W = r.shape
    H = nq * QUAD
    t_blk = min(CHUNK, t_len)
    n_chunks = t_len // t_blk
    blk0 = row0 // t_blk
    row_spec = pl.BlockSpec((nq, t_blk, W), lambda b, c: (0, blk0 + b * n_chunks + c, 0))
    prm_spec = pl.BlockSpec((nq, 1, W), lambda b, c: (0, 0, 0))
    st_spec = pl.BlockSpec((None, H, hs, hs), lambda b, c: (b, 0, 0, 0))
    return pl.pallas_call(
        functools.partial(_scan_kernel, t_blk=t_blk, hs=hs, group=_pick(nq, (SCAN_GROUP, 2, 1)),
                          n_chunks=n_chunks),
        grid=(n_seq, n_chunks),
        in_specs=[row_spec] * 6 + [prm_spec] * 5 + ([] if s0 is None else [st_spec]),
        out_specs=[pl.BlockSpec((nq, t_blk, W), lambda b, c: (0, b * n_chunks + c, 0)), st_spec],
        out_shape=[jax.ShapeDtypeStruct((nq, n_seq * t_len, W), BF16),
                   jax.ShapeDtypeStruct((n_seq, H, hs, hs), F32)],
        scratch_shapes=[pltpu.VMEM((nq, W, W), F32)],
        compiler_params=_cparams(("arbitrary", "arbitrary")),
        name="rwkv_scan",
    )(r, k, v, lw, a, g, *prm, *(() if s0 is None else (s0,)))


def _pool_kernel(x_ref, xp_ref, hist_ref, g_ref, d_ref, hl_ref, *, lay):
    i = pl.program_id(0)
    g = g_ref[...]
    h = _rms(x_ref[...], g)
    tis = lay.tile_in_seq(i)
    prev = jnp.where(tis == 0, hist_ref[...], _rms(xp_ref[...], g))
    ext = jnp.concatenate([prev, h], axis=0)
    D = h.shape[1]
    gw = D // len(POOL_WINDOWS)
    pos0 = jnp.where(i < lay.n_tiles_p, 0, PAST_LEN) + tis * SEQ_TILE
    pos = pos0 + lax.broadcasted_iota(jnp.int32, (SEQ_TILE, 1), 0)
    for gi, w in enumerate(POOL_WINDOWS):
        e = ext[:, gi * gw:(gi + 1) * gw]
        s = e
        step = 1
        while step < w:
            s = s + pltpu.roll(s, step, axis=0)
            step *= 2
        cnt = jnp.minimum(pos + 1, w).astype(F32)
        mean = s[16:] / cnt
        d_ref[:, gi * gw:(gi + 1) * gw] = (mean - h[:, gi * gw:(gi + 1) * gw]).astype(BF16)
    hl_ref[...] = h


def _pool(x, hist16, g, lay):
    R, D = x.shape
    tp16 = SEQ_TILE // 16
    return pl.pallas_call(
        functools.partial(_pool_kernel, lay=lay),
        grid=(lay.n_tiles,),
        in_specs=[
            pl.BlockSpec((SEQ_TILE, D), lambda i: (i, 0)),
            pl.BlockSpec((16, D), lambda i: (jnp.maximum(i * tp16 - 1, 0), 0)),
            pl.BlockSpec((None, 16, D), lambda i: (lay.seq_of_tile(i), 0, 0)),
            pl.BlockSpec((1, D), lambda i: (0, 0)),
        ],
        out_specs=[pl.BlockSpec((SEQ_TILE, D), lambda i: (i, 0)),
                   pl.BlockSpec((None, SEQ_TILE, D), lambda i: (lay.seq_of_tile(i), 0, 0))],
        out_shape=[jax.ShapeDtypeStruct((R, D), BF16),
                   jax.ShapeDtypeStruct((lay.n_seq, SEQ_TILE, D), F32)],
        compiler_params=_cparams(("arbitrary",)),
        name="pool_mix",
    )(x, x, hist16, g)


def _router_kernel(x_ref, g_ref, w_ref, b_ref, hn_ref, wts_ref, eid_ref, cnt_ref, *, n_grp, per_grp):
    hn = _rms(x_ref[...], g_ref[...])
    hb = hn.astype(BF16)
    bits = lax.bitcast_convert_type(hb.astype(F32), jnp.uint32)
    for c in range(hn_ref.shape[1]):
        lo = bits[:, (2 * c) * LANES:(2 * c + 1) * LANES] >> 16
        hi = bits[:, (2 * c + 1) * LANES:(2 * c + 2) * LANES] & jnp.uint32(0xFFFF0000)
        hn_ref[:, c, :] = lo | hi
    logits = _dot(hn, w_ref[...], prec=lax.Precision.HIGHEST) + b_ref[...]
    lane = lax.broadcasted_iota(jnp.int32, logits.shape, 1).astype(F32)
    neg = -jnp.inf
    big = float(logits.shape[1])
    gl = jnp.where(lane < n_grp, logits, neg)
    gmax = jnp.max(gl, axis=-1, keepdims=True)
    gsum = jnp.sum(jnp.exp(gl - gmax), axis=-1, keepdims=True)
    g_idx = jnp.min(jnp.where(gl == gmax, lane, big), axis=-1, keepdims=True)
    g_gate = 1.0 / gsum
    lo = n_grp + g_idx * per_grp
    el = jnp.where((lane >= lo) & (lane < lo + per_grp), logits, neg)
    e1 = jnp.max(el, axis=-1, keepdims=True)
    i1 = jnp.min(jnp.where(el == e1, lane, big), axis=-1, keepdims=True)
    el2 = jnp.where(lane == i1, neg, el)
    e2 = jnp.max(el2, axis=-1, keepdims=True)
    i2 = jnp.min(jnp.where(el2 == e2, lane, big), axis=-1, keepdims=True)
    esum = jnp.sum(jnp.exp(el - e1), axis=-1, keepdims=True)
    p1 = 1.0 / esum
    p2 = jnp.exp(e2 - e1) / esum
    w1 = g_gate * p1 / (p1 + p2)
    w2 = g_gate * p2 / (p1 + p2)
    wts_ref[...] = jnp.where(lane == 0, w1, jnp.where(lane == 1, w2, 0.0))

    @pl.when(pl.program_id(0) == 0)
    def _():
        cnt_ref[...] = jnp.zeros(cnt_ref.shape, F32)

    e1 = i1 - n_grp
    e2 = i2 - n_grp
    oh1 = jnp.where(lane == e1, 1.0, 0.0)
    oh2 = jnp.where(lane == e2, 1.0, 0.0)
    both = oh1 + oh2
    tm = both.shape[0]
    before = (lax.broadcasted_iota(jnp.int32, (tm, tm), 1) < lax.broadcasted_iota(jnp.int32, (tm, tm), 0))
    seen = _dot(jnp.where(before, 1.0, 0.0).astype(BF16), both.astype(BF16)) + cnt_ref[...]
    rank1 = jnp.sum(oh1 * seen, axis=-1, keepdims=True)
    rank2 = jnp.sum(oh2 * seen, axis=-1, keepdims=True)
    cnt_ref[...] += jnp.sum(both, axis=0, keepdims=True)
    eid_ref[...] = jnp.where(lane == 0, e1, jnp.where(lane == 1, e2, jnp.where(
        lane == 2, rank1, jnp.where(lane == 3, rank2, 0.0)))).astype(jnp.int32)


def _router(x, g, w_cat, b_cat, n_grp, per_grp):
    R, D = x.shape
    tm = _pick(R, (256, 128, 64, 32, 16, 8))
    L = w_cat.shape[1]
    return pl.pallas_call(
        functools.partial(_router_kernel, n_grp=n_grp, per_grp=per_grp),
        grid=(R // tm,),
        in_specs=[
            pl.BlockSpec((tm, D), lambda m: (m, 0)),
            pl.BlockSpec((1, D), lambda m: (0, 0)),
            pl.BlockSpec((D, L), lambda m: (0, 0)),
            pl.BlockSpec((1, L), lambda m: (0, 0)),
        ],
        out_specs=[pl.BlockSpec((tm, D // (2 * LANES), LANES), lambda m: (m, 0, 0)),
                   pl.BlockSpec((tm, L), lambda m: (m, 0)),
                   pl.BlockSpec((tm, L), lambda m: (m, 0)),
                   pl.BlockSpec((1, L), lambda m: (0, 0))],
        out_shape=[jax.ShapeDtypeStruct((R, D // (2 * LANES), LANES), jnp.uint32),
                   jax.ShapeDtypeStruct((R, L), F32),
                   jax.ShapeDtypeStruct((R, L), jnp.int32),
                   jax.ShapeDtypeStruct((1, L), F32)],
        compiler_params=_cparams(("arbitrary",)),
        name="moe_router",
    )(x, g, w_cat, b_cat)


def _pitch(nc):
    return nc + 4


def _issue_rows(src_ref, row_of, n_rows, buf, sem):
    nc = src_ref.shape[1]

    def issue(j, carry):
        for par in range(2):
            r = 2 * j + par
            pltpu.make_async_copy(src_ref.at[row_of(r)], buf.at[pl.ds(r * _pitch(nc), nc)], sem).start(priority=par)
        return carry

    lax.fori_loop(0, n_rows // 2, issue, 0)


def _wait_rows(src_ref, n_rows, buf, sem):
    nc = src_ref.shape[1]

    def drain(r, carry):
        pltpu.make_async_copy(src_ref.at[0], buf.at[pl.ds(r * _pitch(nc), nc)], sem).wait()
        return carry

    lax.fori_loop(0, n_rows, drain, 0)


def _gather_kernel(idx_ref, nblk_ref, src_ref, o_ref, buf0, buf1, sem):
    G = o_ref.shape[0]
    nc = src_ref.shape[1]
    b = pl.program_id(0)
    nblk = nblk_ref[0]
    bufs = (buf0, buf1)

    def fetch(blk, slot):
        _issue_rows(src_ref, lambda r: idx_ref[blk * G + r], G, bufs[slot], sem.at[slot])

    @pl.when((b == 0) & (nblk > 0))
    def _():
        fetch(0, 0)

    for slot in range(2):
        @pl.when((b < nblk) & (b % 2 == slot))
        def _(slot=slot):
            @pl.when(b + 1 < nblk)
            def _():
                fetch(b + 1, 1 - slot)

            _wait_rows(src_ref, G, bufs[slot], sem.at[slot])
            for cc in range(nc):
                w = bufs[slot][pl.ds(cc, G, stride=_pitch(nc)), :]
                lo = lax.bitcast_convert_type(w << 16, F32)
                hi = lax.bitcast_convert_type(w & jnp.uint32(0xFFFF0000), F32)
                o_ref[:, (2 * cc) * LANES:(2 * cc + 1) * LANES] = lo.astype(BF16)
                o_ref[:, (2 * cc + 1) * LANES:(2 * cc + 2) * LANES] = hi.astype(BF16)

    @pl.when(b >= nblk_ref[0])
    def _():
        o_ref[...] = jnp.zeros(o_ref.shape, o_ref.dtype)


def _gather_rows(src3, idx, nblk):
    n = idx.shape[0]
    _, nc, lanes = src3.shape
    G = MOE_ROWS
    assert n % G == 0 and lanes == LANES and src3.dtype == jnp.uint32
    return pl.pallas_call(
        _gather_kernel,
        grid_spec=pltpu.PrefetchScalarGridSpec(
            num_scalar_prefetch=2,
            grid=(n // G,),
            in_specs=[pl.BlockSpec(memory_space=pl.ANY)],
            out_specs=pl.BlockSpec((G, 2 * nc * LANES), lambda b, idx_r, nb_r: (b, 0)),
            scratch_shapes=[pltpu.VMEM((G * _pitch(nc), LANES), jnp.uint32),
                            pltpu.VMEM((G * _pitch(nc), LANES), jnp.uint32),
                            pltpu.SemaphoreType.DMA((2,))],
        ),
        out_shape=jax.ShapeDtypeStruct((n, 2 * nc * LANES), BF16),
        compiler_params=_cparams(("arbitrary",)),
        name="row_gather",
    )(idx, nblk, src3)


def _new_expert(blk_e_ref, b):
    prev = blk_e_ref[jnp.maximum(b - 1, 0)]
    return (b == 0) | (blk_e_ref[b] != prev)


def _next_weights(plan, w_hbm, w_f32, w_bf, sem, layer):
    blk_e_ref, _, run_ref, next_ref, nruns_ref = plan
    s, b = pl.program_id(0), pl.program_id(1)
    e = blk_e_ref[b]
    ts = w_bf[0].shape[1]

    def copies(ee, ss, slot):
        return [pltpu.make_async_copy(src.at[layer, ee, :, pl.ds(ss * ts, ts)], dst.at[slot], sem.at[i, slot])
                for i, (src, dst) in enumerate(zip(w_hbm, w_f32))]

    slot = (run_ref[e] + s * nruns_ref[0]) % 2

    @pl.when((s == 0) & (b == 0))
    def _():
        for cp in copies(e, s, slot):
            cp.start()

    e_next = next_ref[e]
    s_next = s + (e_next <= e).astype(jnp.int32)

    @pl.when(s_next < pl.num_programs(0))
    def _():
        for cp in copies(e_next, s_next, 1 - slot):
            cp.start()

    for cp in copies(e, s, slot):
        cp.wait()
    for dst, src in zip(w_bf, w_f32):
        dst[...] = src[slot].astype(BF16)


def _moe_up_kernel(*refs, layer):
    plan, (x_ref, wg_hbm, wu_hbm, h_ref, wg_f32, wu_f32, wg_bf, wu_bf, sem) = refs[:5], refs[5:]
    blk_e_ref, nblk_ref = plan[0], plan[1]
    b = pl.program_id(1)

    @pl.when(b < nblk_ref[0])
    def _():
        @pl.when(_new_expert(blk_e_ref, b))
        def _():
            _next_weights(plan, (wg_hbm, wu_hbm), (wg_f32, wu_f32), (wg_bf, wu_bf), sem, layer)

        x = x_ref[...]
        gate = _dot(x, wg_bf[...])
        up = _dot(x, wu_bf[...])
        h_ref[...] = (jax.nn.silu(gate) * up).astype(BF16)

    @pl.when(b >= nblk_ref[0])
    def _():
        h_ref[...] = jnp.zeros(h_ref.shape, BF16)


def _moe_up(xg, w_gate, w_up, layer, plan):
    P, D = xg.shape
    De = w_gate.shape[3]
    tj = _pick(De, (512, 256, 128))
    nb = P // MOE_ROWS
    any_spec = pl.BlockSpec(memory_space=pl.ANY)
    return pl.pallas_call(
        functools.partial(_moe_up_kernel, layer=layer),
        grid_spec=pltpu.PrefetchScalarGridSpec(
            num_scalar_prefetch=len(plan),
            grid=(De // tj, nb),
            in_specs=[pl.BlockSpec((MOE_ROWS, D), lambda j, b, *_: (b, 0)), any_spec, any_spec],
            out_specs=pl.BlockSpec((MOE_ROWS, tj), lambda j, b, *_: (b, j)),
            scratch_shapes=[pltpu.VMEM((2, D, tj), F32), pltpu.VMEM((2, D, tj), F32),
                            pltpu.VMEM((D, tj), BF16), pltpu.VMEM((D, tj), BF16),
                            pltpu.SemaphoreType.DMA((2, 2))],
        ),
        out_shape=jax.ShapeDtypeStruct((P, De), BF16),
        compiler_params=_cparams(("arbitrary", "arbitrary")),
        name="moe_up",
    )(*plan, xg, w_gate, w_up)


def _moe_down_kernel(*refs, layer):
    plan, (h_ref, wd_hbm, y_ref, wd_f32, wd_bf, sem) = refs[:5], refs[5:]
    blk_e_ref, nblk_ref = plan[0], plan[1]
    b = pl.program_id(1)

    @pl.when(b < nblk_ref[0])
    def _():
        @pl.when(_new_expert(blk_e_ref, b))
        def _():
            _next_weights(plan, (wd_hbm,), (wd_f32,), (wd_bf,), sem, layer)

        y = _dot(h_ref[...], wd_bf[...])
        for cc in range(y_ref.shape[1]):
            y_ref[:, cc, :] = y[:, cc * LANES:(cc + 1) * LANES]

    @pl.when(b >= nblk_ref[0])
    def _():
        y_ref[...] = jnp.zeros(y_ref.shape, F32)


def _moe_down(h, w_down, layer, plan):
    P, De = h.shape
    D = w_down.shape[3]
    tn = _pick(D, (2048, 1024, 512, 256, 128))
    nb = P // MOE_ROWS
    return pl.pallas_call(
        functools.partial(_moe_down_kernel, layer=layer),
        grid_spec=pltpu.PrefetchScalarGridSpec(
            num_scalar_prefetch=len(plan),
            grid=(D // tn, nb),
            in_specs=[pl.BlockSpec((MOE_ROWS, De), lambda n, b, *_: (b, 0)),
                      pl.BlockSpec(memory_space=pl.ANY)],
            out_specs=pl.BlockSpec((MOE_ROWS, tn // LANES, LANES), lambda n, b, *_: (b, n, 0)),
            scratch_shapes=[pltpu.VMEM((2, De, tn), F32), pltpu.VMEM((De, tn), BF16),
                            pltpu.SemaphoreType.DMA((1, 2))],
        ),
        out_shape=jax.ShapeDtypeStruct((P, D // LANES, LANES), F32),
        compiler_params=_cparams(("arbitrary", "arbitrary")),
        name="moe_down",
    )(*plan, h, w_down)


def _dispatch_plan(eid, rank, counts):
    R = eid.shape[0]
    S = R * TOP_K
    n_experts = counts.shape[0]
    e_flat = eid.reshape(-1)
    padded = (counts + MOE_ROWS - 1) // MOE_ROWS * MOE_ROWS
    pad_end = jnp.cumsum(padded)
    dest = ((pad_end - padded)[e_flat] + rank.reshape(-1)).astype(jnp.int32)
    nb = -(-S // MOE_ROWS) + n_experts
    buf_tok = jnp.zeros((nb * MOE_ROWS,), jnp.int32).at[dest].set(jnp.arange(S, dtype=jnp.int32) // TOP_K)
    blk_e = jnp.minimum(jnp.searchsorted(pad_end, jnp.arange(nb, dtype=jnp.int32) * MOE_ROWS, side='right'),
                        n_experts - 1).astype(jnp.int32)
    nblk = (pad_end[-1:] // MOE_ROWS).astype(jnp.int32)
    ids = jnp.arange(n_experts, dtype=jnp.int32)
    live = counts > 0
    later = jnp.where(live[None, :] & (ids[None, :] > ids[:, None]), ids[None, :], n_experts)
    nxt = jnp.min(later, axis=1)
    next_e = jnp.where(nxt == n_experts, jnp.argmax(live), nxt).astype(jnp.int32)
    run_idx = (jnp.cumsum(live) - 1).astype(jnp.int32)
    n_runs = jnp.sum(live).astype(jnp.int32)[None]
    return dest, buf_tok, (blk_e, nblk, run_idx, next_e, n_runs)


def _ple_kernel(slot_ref, x_ref, y_hbm, wts_ref, p_ref, g_ref, gd_ref, gu_ref, win_ref, gf_ref, *rest,
                final, head_tiles, n_tok):
    n_out = 2 if final else 1
    o_refs = rest[:n_out]
    buf0, buf1, ysel, sem = rest[n_out:]
    bufs = (buf0, buf1)
    m = pl.program_id(0)
    tm = x_ref.shape[0]
    nc = y_hbm.shape[1]

    def fetch(tile, slot):
        _issue_rows(y_hbm, lambda r: slot_ref[(r // tm) * n_tok + tile * tm + r % tm], TOP_K * tm,
                    bufs[slot], sem.at[slot])

    @pl.when(m == 0)
    def _():
        fetch(0, 0)

    for slot in range(2):
        @pl.when(m % 2 == slot)
        def _(slot=slot):
            @pl.when(m + 1 < pl.num_programs(0))
            def _():
                fetch(m + 1, 1 - slot)

            _wait_rows(y_hbm, TOP_K * tm, bufs[slot], sem.at[slot])
            for k in range(TOP_K):
                for cc in range(nc):
                    ysel[k, :, cc * LANES:(cc + 1) * LANES] = (
                        bufs[slot][pl.ds(k * tm * _pitch(nc) + cc, tm, stride=_pitch(nc)), :])

    wts = wts_ref[...]
    x = x_ref[...]
    for k in range(TOP_K):
        x = x + ysel[k] * wts[:, k:k + 1]
    hn = _rms(x, g_ref[...]).astype(BF16)
    t = _dot(hn, gd_ref[...]).astype(BF16)
    gate = jax.nn.sigmoid(_dot(t, gu_ref[...]))
    pe = _dot(p_ref[...], win_ref[...])
    x = x + pe * gate
    if not final:
        o_refs[0][...] = x
        return
    x = _rms(x, gf_ref[...])
    m = pl.program_id(0)

    @pl.when(m < head_tiles)
    def _():
        o_refs[0][...] = x

    @pl.when(m >= head_tiles)
    def _():
        o_refs[1][...] = x


def _ple(x, y3, slot_rows, wts, p, g, gd, gu, win, gf, final, head_rows):
    R, D = x.shape
    Pd = p.shape[1]
    L = wts.shape[1]
    nc = y3.shape[1]
    tm = _pick(R, (128, 64, 32, 16, 8))
    nt = R // tm
    row = pl.BlockSpec((tm, D), lambda m, sl: (m, 0))
    fixed = lambda shape: pl.BlockSpec(shape, lambda m, sl: (0, 0))
    assert head_rows % tm == 0
    ht = head_rows // tm
    if final:
        out_specs = [pl.BlockSpec((tm, D), lambda m, sl: (jnp.minimum(m, ht - 1), 0)),
                     pl.BlockSpec((tm, D), lambda m, sl: (jnp.maximum(m - ht, 0), 0))]
        out_shape = [jax.ShapeDtypeStruct((head_rows, D), F32), jax.ShapeDtypeStruct((R - head_rows, D), F32)]
    else:
        out_specs = row
        out_shape = jax.ShapeDtypeStruct((R, D), F32)
    stage = pltpu.VMEM((TOP_K * tm * _pitch(nc), LANES), F32)
    return pl.pallas_call(
        functools.partial(_ple_kernel, final=final, head_tiles=ht, n_tok=R),
        grid_spec=pltpu.PrefetchScalarGridSpec(
            num_scalar_prefetch=1,
            grid=(nt,),
            in_specs=[
                row,
                pl.BlockSpec(memory_space=pl.ANY),
                pl.BlockSpec((tm, L), lambda m, sl: (m, 0)),
                pl.BlockSpec((tm, Pd), lambda m, sl: (m, 0)),
                fixed((1, D)), fixed((D, Pd)), fixed((Pd, D)), fixed((Pd, D)), fixed((1, D)),
            ],
            out_specs=out_specs,
            scratch_shapes=[stage, stage, pltpu.VMEM((TOP_K, tm, D), F32), pltpu.SemaphoreType.DMA((2,))],
        ),
        out_shape=out_shape,
        compiler_params=_cparams(("arbitrary",)),
        name="moe_combine_ple",
    )(slot_rows, x, y3, wts, p, g, gd, gu, win, gf)


def _moe_ple(x, p, layer, prm, final, head_rows):
    R, D = x.shape
    assert (R * TOP_K) % MOE_ROWS == 0 and D % LANES == 0
    n_grp = prm["moe_w_grp"].shape[2]
    n_exp = prm["moe_w_exp"].shape[2]
    L = LANES
    w_cat = jnp.concatenate([prm["moe_w_grp"][layer], prm["moe_w_exp"][layer],
                             jnp.zeros((D, L - n_grp - n_exp), F32)], axis=1)
    b_cat = jnp.concatenate([prm["moe_b_grp"][layer], prm["moe_b_exp"][layer],
                             jnp.zeros((L - n_grp - n_exp,), F32)])[None, :]
    hn, wts, route, cnt = _router(x, prm["norm_ffn"][layer][None, :], w_cat, b_cat, n_grp, n_exp // n_grp)
    dest, buf_tok, plan = _dispatch_plan(
        route[:, :TOP_K], route[:, TOP_K:2 * TOP_K], cnt[0, :n_exp].astype(jnp.int32))
    xg = _gather_rows(hn, buf_tok, plan[1])
    hmid = _moe_up(xg, prm["moe_w_gate"], prm["moe_w_up"], layer, plan)
    y3 = _moe_down(hmid, prm["moe_w_down"], layer, plan)
    slot_rows = dest.reshape(R, TOP_K).T.reshape(-1)
    return _ple(x, y3, slot_rows, wts, p, prm["norm_ple"][layer][None, :],
                prm["ple_gate_down"][layer].astype(BF16), prm["ple_gate_up"][layer].astype(BF16),
                prm["ple_w_in"][layer].astype(BF16), prm["norm_final"][None, :], final, head_rows)


def kernel(x_prompt, x_sample, state_rwkv_shift, state_rwkv_wkv, state_pool, p_prompt, p_sample,
           rwkv_mu, rwkv_w_rkv, rwkv_w_o, rwkv_w0, rwkv_w1, rwkv_w2, rwkv_a0, rwkv_a1, rwkv_a2,
           rwkv_g1, rwkv_g2, rwkv_k_k, rwkv_k_a, rwkv_r_k, rwkv_lnx_w, rwkv_lnx_b,
           pool_w, pool_scale, norm_mix, norm_ffn, norm_ple, norm_final,
           moe_w_grp, moe_b_grp, moe_w_exp, moe_b_exp, moe_w_gate, moe_w_up, moe_w_down,
           ple_w_in, ple_gate_down, ple_gate_up):
    Bp, Tp, D = x_prompt.shape
    Bs, Ts, _ = x_sample.shape
    depth = norm_mix.shape[0]
    H, hs = rwkv_r_k.shape[1], rwkv_r_k.shape[2]
    W = QUAD * hs
    assert D % W == 0 and Ts <= CHUNK and Tp % CHUNK == 0
    lay = _SeqLayout(Bp, Tp, Bs, Ts)
    Rp = Bp * Tp
    prm = dict(moe_w_grp=moe_w_grp, moe_b_grp=moe_b_grp, moe_w_exp=moe_w_exp, moe_b_exp=moe_b_exp,
               moe_w_gate=moe_w_gate, moe_w_up=moe_w_up, moe_w_down=moe_w_down, norm_ffn=norm_ffn,
               norm_ple=norm_ple, norm_final=norm_final, ple_w_in=ple_w_in, ple_gate_down=ple_gate_down,
               ple_gate_up=ple_gate_up)

    x = jnp.concatenate([x_prompt.reshape(Rp, D), x_sample.reshape(Bs * Ts, D)], axis=0)
    Pd = p_prompt.shape[-1]
    p_all = jnp.concatenate([p_prompt.reshape(depth, Rp, Pd), p_sample.reshape(depth, Bs * Ts, Pd)],
                            axis=1).astype(BF16)

    def slabs(vec):
        return vec.reshape(D // W, 1, W)

    shift_p, wkv_p, pool_p, shift_s, wkv_s, pool_s = [], [], [], [], [], []
    for i in range(depth):
        j = i // 2
        g_mix = norm_mix[i][None, :]
        if i % 2 == 0:
            shift_all = jnp.concatenate([jnp.zeros((Bp, D), F32), state_rwkv_shift[j]], axis=0)[:, None, :]
            mixed, h_last = _norm_mix(x, shift_all, g_mix, rwkv_mu[j], lay)
            xr, xw, xk, xv, xa, xg = mixed
            w_rkv = rwkv_w_rkv[j].astype(BF16)
            r = _mm(xr, w_rkv, w_group=0, slab_out=W, name="proj_r")
            k = _mm(xk, w_rkv, w_group=1, slab_out=W, name="proj_k")
            v = _mm(xv, w_rkv, w_group=2, slab_out=W, name="proj_v")
            lw = _lora(xw, rwkv_w1[j].astype(BF16), rwkv_w2[j].astype(BF16), rwkv_w0[j][None, :], "decay", W)
            a = _lora(xa, rwkv_a1[j].astype(BF16), rwkv_a2[j].astype(BF16), rwkv_a0[j][None, :], "aaa", W)
            gl = rwkv_g1.shape[2]
            glp = -(-gl // LANES) * LANES
            g1 = jnp.pad(rwkv_g1[j], ((0, 0), (0, glp - gl))).astype(BF16)
            g2 = jnp.pad(rwkv_g2[j], ((0, glp - gl), (0, 0))).astype(BF16)
            g = _lora(xg, g1, g2, jnp.zeros((1, D), F32), "gate", W)
            sprm = (slabs(rwkv_k_k[j]), slabs(rwkv_k_a[j]), slabs(rwkv_r_k[j].reshape(D)),
                    slabs(rwkv_lnx_w[j]), slabs(rwkv_lnx_b[j]))
            o_p, sp = _scan(r, k, v, lw, a, g, sprm, None, row0=0, n_seq=Bp, t_len=Tp, hs=hs)
            o_s, ss = _scan(r, k, v, lw, a, g, sprm, state_rwkv_wkv[j], row0=Rp, n_seq=Bs, t_len=Ts, hs=hs)
            o = jnp.concatenate([o_p, o_s], axis=1)
            x = _mm(o, rwkv_w_o[j].astype(BF16)[None], res=x, slab_in=True, name="proj_o")
            shift_p.append(h_last[:Bp, -1])
            shift_s.append(h_last[Bp:, -1])
            wkv_p.append(sp)
            wkv_s.append(ss)
        else:
            hist = jnp.concatenate([jnp.zeros((Bp, 16, D), F32),
                                    jnp.pad(state_pool[j], ((0, 0), (1, 0), (0, 0)))], axis=0)
            d, h_last = _pool(x, hist, g_mix, lay)
            x = _mm(d, pool_w[j].astype(BF16), scale=pool_scale[j][None, :], res=x, name="pool_proj")
            nh = state_pool.shape[2]
            pool_p.append(h_last[:Bp, SEQ_TILE - nh:])
            pool_s.append(h_last[Bp:, SEQ_TILE - nh:])
        x = _moe_ple(x, p_all[i], i, prm, final=(i == depth - 1), head_rows=Rp)

    y_prompt = x[0].reshape(Bp, Tp, D)
    y_sample = x[1].reshape(Bs, Ts, D)
    return (y_prompt, y_sample, jnp.stack(shift_p), jnp.stack(wkv_p), jnp.stack(pool_p),
            jnp.stack(shift_s), jnp.stack(wkv_s), jnp.stack(pool_s))
```

```python
import functools

import jax
import jax.numpy as jnp
from jax import lax
from jax.experimental import pallas as pl
from jax.experimental.pallas import tpu as pltpu

F32 = jnp.float32
BF16 = jnp.bfloat16

NORM_EPS = 1e-6
GN_EPS = 64e-5
PAST_LEN = 4096
POOL_WINDOWS = (2, 4, 8, 16)
TOP_K = 2
LANES = 128
SEQ_TILE = 32
CHUNK = 64
QUAD = 4
SCAN_GROUP = 4
MOE_ROWS = 256
VMEM_LIMIT = 56 * 1024 * 1024


def _cparams(sem):
    return pltpu.CompilerParams(dimension_semantics=sem, vmem_limit_bytes=VMEM_LIMIT)


def _pick(n, cands):
    for c in cands:
        if n % c == 0:
            return c
    return n


def _rms(x, g):
    return x * lax.rsqrt(jnp.mean(x * x, axis=-1, keepdims=True) + NORM_EPS) * g


_NN = (((1,), (0,)), ((), ()))
_NT = (((1,), (1,)), ((), ()))
_TN = (((0,), (0,)), ((), ()))


def _dot(a, b, dims=_NN, prec=None):
    return lax.dot_general(a, b, dims, precision=prec, preferred_element_type=F32)


def _mm_kernel(*refs, has_res, has_scale, a_slabs, o_slabs):
    a_ref, w_ref = refs[0], refs[1]
    k = 2
    if a_slabs:
        a = jnp.concatenate([a_ref[q] for q in range(a_slabs)], axis=1)
    else:
        a = a_ref[...]
    acc = _dot(a, w_ref[...])
    if has_scale:
        acc = acc * refs[k][...]
        k += 1
    if has_res:
        acc = refs[k][...] + acc
        k += 1
    o_ref = refs[k]
    if o_slabs:
        sw = o_ref.shape[2]
        for q in range(o_slabs):
            o_ref[q] = acc[:, q * sw:(q + 1) * sw].astype(o_ref.dtype)
    else:
        o_ref[...] = acc.astype(o_ref.dtype)


def _mm(a, w, *, res=None, scale=None, out_dtype=F32, slab_in=False, slab_out=0, w_group=None, name="mm"):
    M = a.shape[1] if slab_in else a.shape[0]
    G, Kg, Ng = w.shape
    w0 = 0
    if w_group is not None:
        G, w0 = 1, w_group
    tm = _pick(M, (512, 256, 128, 64, 32, 16, 8))
    tn = _pick(Ng, (1024, 512, 256, 128))
    nn = Ng // tn
    if slab_in:
        assert G == 1
        a_spec = pl.BlockSpec((a.shape[0], tm, a.shape[2]), lambda g, n, m: (0, m, 0))
    else:
        a_spec = pl.BlockSpec((tm, Kg), lambda g, n, m: (m, g))
    in_specs = [a_spec, pl.BlockSpec((None, Kg, tn), lambda g, n, m: (w0 + g, 0, n))]
    args = [a, w]
    if scale is not None:
        in_specs.append(pl.BlockSpec((1, tn), lambda g, n, m: (0, g * nn + n)))
        args.append(scale)
    if res is not None:
        in_specs.append(pl.BlockSpec((tm, tn), lambda g, n, m: (m, g * nn + n)))
        args.append(res)
    if slab_out:
        assert tn % slab_out == 0
        per = tn // slab_out
        out_spec = pl.BlockSpec((per, tm, slab_out), lambda g, n, m: (g * nn + n, m, 0))
        out_shape = jax.ShapeDtypeStruct((G * Ng // slab_out, M, slab_out), out_dtype)
    else:
        per = 0
        out_spec = pl.BlockSpec((tm, tn), lambda g, n, m: (m, g * nn + n))
        out_shape = jax.ShapeDtypeStruct((M, G * Ng), out_dtype)
    return pl.pallas_call(
        functools.partial(_mm_kernel, has_res=res is not None, has_scale=scale is not None,
                          a_slabs=a.shape[0] if slab_in else 0, o_slabs=per),
        grid=(G, nn, M // tm),
        in_specs=in_specs,
        out_specs=out_spec,
        out_shape=out_shape,
        compiler_params=_cparams(("arbitrary", "arbitrary", "arbitrary")),
        name=name,
    )(*args)


def _lora_kernel(a_ref, w1_ref, w2_ref, b_ref, o_ref, *, mode):
    t = _dot(a_ref[...], w1_ref[...])
    if mode == "decay":
        t = jnp.tanh(t)
    elif mode == "gate":
        t = jax.nn.sigmoid(t)
    z = _dot(t.astype(BF16), w2_ref[...]) + b_ref[...]
    if mode == "decay":
        z = -jnp.exp(-jax.nn.softplus(-z) - 0.5)
    elif mode == "aaa":
        z = jax.nn.sigmoid(z)
    sw = o_ref.shape[2]
    for q in range(o_ref.shape[0]):
        o_ref[q] = z[:, q * sw:(q + 1) * sw]


def _lora(a, w1, w2, bias, mode, sw):
    M, K = a.shape
    L = w1.shape[1]
    D = w2.shape[1]
    tm = _pick(M, (256, 128, 64, 32, 16, 8))
    return pl.pallas_call(
        functools.partial(_lora_kernel, mode=mode),
        grid=(M // tm,),
        in_specs=[
            pl.BlockSpec((tm, K), lambda m: (m, 0)),
            pl.BlockSpec((K, L), lambda m: (0, 0)),
            pl.BlockSpec((L, D), lambda m: (0, 0)),
            pl.BlockSpec((1, D), lambda m: (0, 0)),
        ],
        out_specs=pl.BlockSpec((D // sw, tm, sw), lambda m: (0, m, 0)),
        out_shape=jax.ShapeDtypeStruct((D // sw, M, sw), F32),
        compiler_params=_cparams(("arbitrary",)),
        name="lora_" + mode,
    )(a, w1, w2, bias)


class _SeqLayout:
    def __init__(self, bp, tp, bs, ts):
        assert tp % SEQ_TILE == 0 and ts % SEQ_TILE == 0
        self.bp, self.tp, self.bs, self.ts = bp, tp, bs, ts
        self.tiles_p = tp // SEQ_TILE
        self.tiles_s = ts // SEQ_TILE
        self.n_tiles_p = bp * self.tiles_p
        self.n_tiles = self.n_tiles_p + bs * self.tiles_s
        self.n_seq = bp + bs
        self.rows = bp * tp + bs * ts

    def seq_of_tile(self, i):
        return jnp.where(i < self.n_tiles_p, i // self.tiles_p,
                         self.bp + (i - self.n_tiles_p) // self.tiles_s)

    def tile_in_seq(self, i):
        return jnp.where(i < self.n_tiles_p, i % self.tiles_p, (i - self.n_tiles_p) % self.tiles_s)


def _norm_mix_kernel(x_ref, xp_ref, sh_ref, g_ref, mu_ref, *outs, lay):
    i = pl.program_id(0)
    g = g_ref[...]
    h = _rms(x_ref[...], g)
    h_before = _rms(xp_ref[7:8, :], g)
    first = lay.tile_in_seq(i) == 0
    prev_row = jnp.where(first, sh_ref[...], h_before)
    rows = lax.broadcasted_iota(jnp.int32, h.shape, 0)
    h_prev = jnp.where(rows == 0, prev_row, pltpu.roll(h, 1, axis=0))
    xx = h_prev - h
    for n in range(6):
        outs[n][...] = (h + xx * mu_ref[n:n + 1, :]).astype(BF16)
    outs[6][...] = h


def _norm_mix(x, shift_all, g, mu, lay):
    R, D = x.shape
    tpb = SEQ_TILE // 8
    row_spec = pl.BlockSpec((SEQ_TILE, D), lambda i: (i, 0))
    outs = pl.pallas_call(
        functools.partial(_norm_mix_kernel, lay=lay),
        grid=(lay.n_tiles,),
        in_specs=[
            row_spec,
            pl.BlockSpec((8, D), lambda i: (jnp.maximum(i * tpb - 1, 0), 0)),
            pl.BlockSpec((None, 1, D), lambda i: (lay.seq_of_tile(i), 0, 0)),
            pl.BlockSpec((1, D), lambda i: (0, 0)),
            pl.BlockSpec((6, D), lambda i: (0, 0)),
        ],
        out_specs=[row_spec] * 6 + [pl.BlockSpec((None, SEQ_TILE, D), lambda i: (lay.seq_of_tile(i), 0, 0))],
        out_shape=[jax.ShapeDtypeStruct((R, D), BF16)] * 6
        + [jax.ShapeDtypeStruct((lay.n_seq, SEQ_TILE, D), F32)],
        compiler_params=_cparams(("arbitrary",)),
        name="norm_mix",
    )(x, x, shift_all, g, mu)
    return outs[:6], outs[6]


def _scan_kernel(r_ref, k_ref, v_ref, lw_ref, a_ref, g_ref, kk_ref, ka_ref, rk_ref, lnw_ref, lnb_ref,
                 *rest, t_blk, hs, group, n_chunks):
    s0_ref = rest[0] if len(rest) == 4 else None
    o_ref, sout_ref, s_ref = rest[-3:]
    C = t_blk
    nq, _, W = r_ref.shape
    c = pl.program_id(1)

    def head_block(q, j):
        return (q, slice(j * hs, (j + 1) * hs), slice(j * hs, (j + 1) * hs))

    @pl.when(c == 0)
    def _():
        s_ref[...] = jnp.zeros(s_ref.shape, F32)
        if s0_ref is not None:
            for q in range(nq):
                for j in range(QUAD):
                    s_ref[head_block(q, j)] = s0_ref[q * QUAD + j]

    ri = lax.broadcasted_iota(jnp.int32, (W, W), 0)
    ci = lax.broadcasted_iota(jnp.int32, (W, W), 1)
    head_blk = (ri // hs) == (ci // hs)
    ones_blk = jnp.where(head_blk, 1.0, 0.0).astype(F32)
    ti = lax.broadcasted_iota(jnp.int32, (C, C), 0)
    si = lax.broadcasted_iota(jnp.int32, (C, C), 1)
    tri_incl = jnp.where(ti >= si, 1.0, 0.0).astype(F32)
    lane_head = lax.broadcasted_iota(jnp.int32, (C, W), 1) // hs
    rj = lax.broadcasted_iota(jnp.int32, (QUAD * C, QUAD * C), 0)
    cj = lax.broadcasted_iota(jnp.int32, (QUAD * C, QUAD * C), 1)
    t_in = rj % C
    s_in = cj % C
    strict = jnp.where(t_in > s_in, 1.0, 0.0).astype(F32)
    incl = jnp.where(t_in >= s_in, 1.0, 0.0).astype(F32)
    eye = jnp.where(rj == cj, 1.0, 0.0).astype(F32)

    ones_bf = ones_blk.astype(BF16)
    head_masks = [jnp.where(lane_head == h, 1.0, 0.0).astype(BF16) for h in range(QUAD)]

    def fold(x):
        y = x[0:C]
        for h in range(1, QUAD):
            y = y + x[h * C:(h + 1) * C]
        return y

    def group_body(it, carry, *, passes, group):
        qs = [it * group + j for j in range(group)]
        each = lambda f, *xs: [f(*t) for t in zip(*xs)]

        full = passes == 6

        def parts(x):
            if full:
                return (x,)
            hi = x.astype(BF16)
            if passes == 1:
                return (hi,)
            return (hi, (x - hi.astype(F32)).astype(BF16))

        def bdot(x, y, dims=_NN):
            if full:
                return _dot(x[0], y[0], dims, lax.Precision.HIGHEST)
            out = _dot(x[0], y[0], dims)
            if passes > 1:
                out = out + (_dot(x[0], y[1], dims) + _dot(x[1], y[0], dims))
            return out

        def head_sum(x):
            if full:
                return _dot(x, ones_blk, _NN, lax.Precision.HIGHEST)
            out = _dot(parts(x)[0], ones_bf)
            if passes > 1:
                out = out + _dot(parts(x)[1], ones_bf)
            return out

        def stack(x):
            return tuple(jnp.concatenate([xb * m for m in head_masks], axis=0) for xb in parts(x))

        def load(ref):
            return [ref[q] for q in qs]

        r, k, v, lw, a = load(r_ref), load(k_ref), load(v_ref), load(lw_ref), load(a_ref)
        kk = each(lambda k_, q: k_ * kk_ref[q], k, qs)
        kk = each(lambda x: x * lax.rsqrt(jnp.maximum(head_sum(x * x), 1e-24)), kk)
        kf = each(lambda k_, a_, q: k_ * (1.0 + (a_ - 1.0) * ka_ref[q]), k, a, qs)
        bv = each(lambda x, a_: x * a_, kk, a)

        cum = each(lambda x: _dot(tri_incl, x, _NN, lax.Precision.HIGHEST), lw)
        e_neg = each(lambda x: jnp.exp(-x), cum)
        rt = each(lambda r_, x: r_ * jnp.exp(x), r, cum)
        at = each(lambda x, c_, l_: -x * jnp.exp(c_ - l_), kk, cum, lw)
        kt = each(lambda x, e: x * e, kf, e_neg)
        bt = each(lambda x, e: x * e, bv, e_neg)
        p_c = each(lambda x: jnp.exp(x[C - 1:C, :]), cum)

        a_st, r_st, b_st, k_st, v_st = (each(stack, x) for x in (at, rt, bt, kt, v))
        a_ab = each(lambda x, y: bdot(x, y, _NT) * strict, a_st, b_st)
        a_ak = each(lambda x, y: parts(bdot(x, y, _NT) * strict), a_st, k_st)
        a_rb = each(lambda x, y: parts(bdot(x, y, _NT) * incl), r_st, b_st)
        a_rk = each(lambda x, y: parts(bdot(x, y, _NT) * incl), r_st, k_st)

        t_inv = each(lambda x: eye + x, a_ab)
        n_pow = each(parts, a_ab)
        p = 1
        while 2 * p < C:
            n_pow = each(lambda x: parts(bdot(x, x)), n_pow)
            t_inv = each(lambda t, n: t + bdot(parts(t), n), t_inv, n_pow)
            p *= 2
        t_bf = each(parts, t_inv)

        s = [s_ref[q] for q in qs]
        s_bf = each(parts, s)
        u_in = each(lambda a_, s_, ak, v_: parts(bdot(a_, s_, _NT) + bdot(ak, v_)), a_st, s_bf, a_ak, v_st)
        u_st = each(bdot, t_bf, u_in)
        o_st = each(lambda r_, s_, rb, u_, rk, v_: bdot(r_, s_, _NT) + bdot(rb, parts(u_)) + bdot(rk, v_),
                    r_st, s_bf, a_rb, u_st, a_rk, v_st)
        o = each(fold, o_st)
        u = each(fold, u_st)
        upd = each(lambda u_, v_, b_, k_, pc: bdot(parts(jnp.concatenate([u_, v_], axis=0)),
                                                   parts(jnp.concatenate([b_ * pc, k_ * pc], axis=0)), _TN),
                   u, v, bt, kt, p_c)
        for q, s_, pc, up in zip(qs, s, p_c, upd):
            s_ref[q] = s_ * pc + jnp.where(head_blk, up, 0.0)

        inv_n = 1.0 / hs
        o_c = each(lambda x: x - head_sum(x) * inv_n, o)
        o_n = each(lambda x: x * lax.rsqrt(head_sum(x * x) * inv_n + GN_EPS), o_c)
        bonus = each(lambda r_, k_, q: head_sum(r_ * k_ * rk_ref[q]), r, kf, qs)
        for q, x, bo, v_ in zip(qs, o_n, bonus, v):
            y = x * lnw_ref[q] + lnb_ref[q] + bo * v_
            o_ref[q] = (y * g_ref[q]).astype(o_ref.dtype)
        return carry

    def run(passes, grp):
        lax.fori_loop(0, nq // grp, functools.partial(group_body, passes=passes, group=grp), 0)

    if n_chunks > 1:
        last = n_chunks - 1

        @pl.when(c < last)
        def _():
            run(1, group)

        @pl.when(c == last)
        def _():
            run(6, _pick(nq, (2, 1)))
    else:
        run(1, group)

    @pl.when(c == n_chunks - 1)
    def _():
        for q in range(nq):
            for j in range(QUAD):
                sout_ref[q * QUAD + j] = s_ref[head_block(q, j)]


def _scan(r, k, v, lw, a, g, prm, s0, *, row0, n_seq, t_len, hs):
    nq, _, W = r.shape
    H = nq * QUAD
    t_blk = min(CHUNK, t_len)
    n_chunks = t_len // t_blk
    blk0 = row0 // t_blk
    row_spec = pl.BlockSpec((nq, t_blk, W), lambda b, c: (0, blk0 + b * n_chunks + c, 0))
    prm_spec = pl.BlockSpec((nq, 1, W), lambda b, c: (0, 0, 0))
    st_spec = pl.BlockSpec((None, H, hs, hs), lambda b, c: (b, 0, 0, 0))
    return pl.pallas_call(
        functools.partial(_scan_kernel, t_blk=t_blk, hs=hs, group=_pick(nq, (SCAN_GROUP, 2, 1)),
                          n_chunks=n_chunks),
        grid=(n_seq, n_chunks),
        in_specs=[row_spec] * 6 + [prm_spec] * 5 + ([] if s0 is None else [st_spec]),
        out_specs=[pl.BlockSpec((nq, t_blk, W), lambda b, c: (0, b * n_chunks + c, 0)), st_spec],
        out_shape=[jax.ShapeDtypeStruct((nq, n_seq * t_len, W), BF16),
                   jax.ShapeDtypeStruct((n_seq, H, hs, hs), F32)],
        scratch_shapes=[pltpu.VMEM((nq, W, W), F32)],
        compiler_params=_cparams(("arbitrary", "arbitrary")),
        name="rwkv_scan",
    )(r, k, v, lw, a, g, *prm, *(() if s0 is None else (s0,)))


def _pool_kernel(x_ref, xp_ref, hist_ref, g_ref, d_ref, hl_ref, *, lay):
    i = pl.program_id(0)
    g = g_ref[...]
    h = _rms(x_ref[...], g)
    tis = lay.tile_in_seq(i)
    prev = jnp.where(tis == 0, hist_ref[...], _rms(xp_ref[...], g))
    ext = jnp.concatenate([prev, h], axis=0)
    D = h.shape[1]
    gw = D // len(POOL_WINDOWS)
    pos0 = jnp.where(i < lay.n_tiles_p, 0, PAST_LEN) + tis * SEQ_TILE
    pos = pos0 + lax.broadcasted_iota(jnp.int32, (SEQ_TILE, 1), 0)
    for gi, w in enumerate(POOL_WINDOWS):
        e = ext[:, gi * gw:(gi + 1) * gw]
        s = e
        step = 1
        while step < w:
            s = s + pltpu.roll(s, step, axis=0)
            step *= 2
        cnt = jnp.minimum(pos + 1, w).astype(F32)
        mean = s[16:] / cnt
        d_ref[:, gi * gw:(gi + 1) * gw] = (mean - h[:, gi * gw:(gi + 1) * gw]).astype(BF16)
    hl_ref[...] = h


def _pool(x, hist16, g, lay):
    R, D = x.shape
    tp16 = SEQ_TILE // 16
    return pl.pallas_call(
        functools.partial(_pool_kernel, lay=lay),
        grid=(lay.n_tiles,),
        in_specs=[
            pl.BlockSpec((SEQ_TILE, D), lambda i: (i, 0)),
            pl.BlockSpec((16, D), lambda i: (jnp.maximum(i * tp16 - 1, 0), 0)),
            pl.BlockSpec((None, 16, D), lambda i: (lay.seq_of_tile(i), 0, 0)),
            pl.BlockSpec((1, D), lambda i: (0, 0)),
        ],
        out_specs=[pl.BlockSpec((SEQ_TILE, D), lambda i: (i, 0)),
                   pl.BlockSpec((None, SEQ_TILE, D), lambda i: (lay.seq_of_tile(i), 0, 0))],
        out_shape=[jax.ShapeDtypeStruct((R, D), BF16),
                   jax.ShapeDtypeStruct((lay.n_seq, SEQ_TILE, D), F32)],
        compiler_params=_cparams(("arbitrary",)),
        name="pool_mix",
    )(x, x, hist16, g)


def _router_kernel(x_ref, g_ref, w_ref, b_ref, hn_ref, wts_ref, eid_ref, cnt_ref, *, n_grp, per_grp):
    hn = _rms(x_ref[...], g_ref[...])
    hb = hn.astype(BF16)
    bits = lax.bitcast_convert_type(hb.astype(F32), jnp.uint32)
    for c in range(hn_ref.shape[1]):
        lo = bits[:, (2 * c) * LANES:(2 * c + 1) * LANES] >> 16
        hi = bits[:, (2 * c + 1) * LANES:(2 * c + 2) * LANES] & jnp.uint32(0xFFFF0000)
        hn_ref[:, c, :] = lo | hi
    logits = _dot(hn, w_ref[...], prec=lax.Precision.HIGHEST) + b_ref[...]
    lane = lax.broadcasted_iota(jnp.int32, logits.shape, 1).astype(F32)
    neg = -jnp.inf
    big = float(logits.shape[1])
    gl = jnp.where(lane < n_grp, logits, neg)
    gmax = jnp.max(gl, axis=-1, keepdims=True)
    gsum = jnp.sum(jnp.exp(gl - gmax), axis=-1, keepdims=True)
    g_idx = jnp.min(jnp.where(gl == gmax, lane, big), axis=-1, keepdims=True)
    g_gate = 1.0 / gsum
    lo = n_grp + g_idx * per_grp
    el = jnp.where((lane >= lo) & (lane < lo + per_grp), logits, neg)
    e1 = jnp.max(el, axis=-1, keepdims=True)
    i1 = jnp.min(jnp.where(el == e1, lane, big), axis=-1, keepdims=True)
    el2 = jnp.where(lane == i1, neg, el)
    e2 = jnp.max(el2, axis=-1, keepdims=True)
    i2 = jnp.min(jnp.where(el2 == e2, lane, big), axis=-1, keepdims=True)
    esum = jnp.sum(jnp.exp(el - e1), axis=-1, keepdims=True)
    p1 = 1.0 / esum
    p2 = jnp.exp(e2 - e1) / esum
    w1 = g_gate * p1 / (p1 + p2)
    w2 = g_gate * p2 / (p1 + p2)
    wts_ref[...] = jnp.where(lane == 0, w1, jnp.where(lane == 1, w2, 0.0))

    @pl.when(pl.program_id(0) == 0)
    def _():
        cnt_ref[...] = jnp.zeros(cnt_ref.shape, F32)

    e1 = i1 - n_grp
    e2 = i2 - n_grp
    oh1 = jnp.where(lane == e1, 1.0, 0.0)
    oh2 = jnp.where(lane == e2, 1.0, 0.0)
    both = oh1 + oh2
    tm = both.shape[0]
    before = (lax.broadcasted_iota(jnp.int32, (tm, tm), 1) < lax.broadcasted_iota(jnp.int32, (tm, tm), 0))
    seen = _dot(jnp.where(before, 1.0, 0.0).astype(BF16), both.astype(BF16)) + cnt_ref[...]
    rank1 = jnp.sum(oh1 * seen, axis=-1, keepdims=True)
    rank2 = jnp.sum(oh2 * seen, axis=-1, keepdims=True)
    cnt_ref[...] += jnp.sum(both, axis=0, keepdims=True)
    eid_ref[...] = jnp.where(lane == 0, e1, jnp.where(lane == 1, e2, jnp.where(
        lane == 2, rank1, jnp.where(lane == 3, rank2, 0.0)))).astype(jnp.int32)


def _router(x, g, w_cat, b_cat, n_grp, per_grp):
    R, D = x.shape
    tm = _pick(R, (256, 128, 64, 32, 16, 8))
    L = w_cat.shape[1]
    return pl.pallas_call(
        functools.partial(_router_kernel, n_grp=n_grp, per_grp=per_grp),
        grid=(R // tm,),
        in_specs=[
            pl.BlockSpec((tm, D), lambda m: (m, 0)),
            pl.BlockSpec((1, D), lambda m: (0, 0)),
            pl.BlockSpec((D, L), lambda m: (0, 0)),
            pl.BlockSpec((1, L), lambda m: (0, 0)),
        ],
        out_specs=[pl.BlockSpec((tm, D // (2 * LANES), LANES), lambda m: (m, 0, 0)),
                   pl.BlockSpec((tm, L), lambda m: (m, 0)),
                   pl.BlockSpec((tm, L), lambda m: (m, 0)),
                   pl.BlockSpec((1, L), lambda m: (0, 0))],
        out_shape=[jax.ShapeDtypeStruct((R, D // (2 * LANES), LANES), jnp.uint32),
                   jax.ShapeDtypeStruct((R, L), F32),
                   jax.ShapeDtypeStruct((R, L), jnp.int32),
                   jax.ShapeDtypeStruct((1, L), F32)],
        compiler_params=_cparams(("arbitrary",)),
        name="moe_router",
    )(x, g, w_cat, b_cat)


def _pitch(nc):
    return nc + 4


def _issue_rows(src_ref, row_of, n_rows, buf, sem, holes=False):
    nc = src_ref.shape[1]

    def issue(j, carry):
        for par in range(2):
            r = 2 * j + par
            row = row_of(r)

            def start(r=r, row=row, par=par):
                pltpu.make_async_copy(src_ref.at[row], buf.at[pl.ds(r * _pitch(nc), nc)], sem).start(priority=par)

            if holes:
                pl.when(row >= 0)(start)
            else:
                start()
        return carry

    lax.fori_loop(0, n_rows // 2, issue, 0)


def _wait_rows(src_ref, row_of, n_rows, buf, sem, holes=False):
    nc = src_ref.shape[1]

    def drain(r, carry):
        def wait():
            pltpu.make_async_copy(src_ref.at[0], buf.at[pl.ds(r * _pitch(nc), nc)], sem).wait()

        if holes:
            pl.when(row_of(r) >= 0)(wait)
        else:
            wait()
        return carry

    lax.fori_loop(0, n_rows, drain, 0)


def _gather_kernel(idx_ref, nblk_ref, src_ref, o_ref, buf0, buf1, sem):
    G = o_ref.shape[0]
    nc = src_ref.shape[1]
    b = pl.program_id(0)
    nblk = nblk_ref[0]
    bufs = (buf0, buf1)

    def rows_of(blk):
        return lambda r: idx_ref[blk * G + r]

    def fetch(blk, slot):
        _issue_rows(src_ref, rows_of(blk), G, bufs[slot], sem.at[slot], holes=True)

    @pl.when(b == 0)
    def _():
        for buf in bufs:
            buf[...] = jnp.zeros(buf.shape, buf.dtype)

    @pl.when((b == 0) & (nblk > 0))
    def _():
        fetch(0, 0)

    for slot in range(2):
        @pl.when((b < nblk) & (b % 2 == slot))
        def _(slot=slot):
            @pl.when(b + 1 < nblk)
            def _():
                fetch(b + 1, 1 - slot)

            _wait_rows(src_ref, rows_of(b), G, bufs[slot], sem.at[slot], holes=True)
            for cc in range(nc):
                w = bufs[slot][pl.ds(cc, G, stride=_pitch(nc)), :]
                lo = lax.bitcast_convert_type(w << 16, F32)
                hi = lax.bitcast_convert_type(w & jnp.uint32(0xFFFF0000), F32)
                o_ref[:, (2 * cc) * LANES:(2 * cc + 1) * LANES] = lo.astype(BF16)
                o_ref[:, (2 * cc + 1) * LANES:(2 * cc + 2) * LANES] = hi.astype(BF16)

    @pl.when(b >= nblk_ref[0])
    def _():
        o_ref[...] = jnp.zeros(o_ref.shape, o_ref.dtype)


def _gather_rows(src3, idx, nblk):
    n = idx.shape[0]
    _, nc, lanes = src3.shape
    G = MOE_ROWS
    assert n % G == 0 and lanes == LANES and src3.dtype == jnp.uint32
    return pl.pallas_call(
        _gather_kernel,
        grid_spec=pltpu.PrefetchScalarGridSpec(
            num_scalar_prefetch=2,
            grid=(n // G,),
            in_specs=[pl.BlockSpec(memory_space=pl.ANY)],
            out_specs=pl.BlockSpec((G, 2 * nc * LANES), lambda b, idx_r, nb_r: (b, 0)),
            scratch_shapes=[pltpu.VMEM((G * _pitch(nc), LANES), jnp.uint32),
                            pltpu.VMEM((G * _pitch(nc), LANES), jnp.uint32),
                            pltpu.SemaphoreType.DMA((2,))],
        ),
        out_shape=jax.ShapeDtypeStruct((n, 2 * nc * LANES), BF16),
        compiler_params=_cparams(("arbitrary",)),
        name="row_gather",
    )(idx, nblk, src3)


def _new_expert(blk_e_ref, b):
    prev = blk_e_ref[jnp.maximum(b - 1, 0)]
    return (b == 0) | (blk_e_ref[b] != prev)


def _next_weights(plan, w_hbm, w_f32, w_bf, sem, layer):
    blk_e_ref, _, run_ref, next_ref, nruns_ref = plan
    s, b = pl.program_id(0), pl.program_id(1)
    e = blk_e_ref[b]
    ts = w_bf[0].shape[1]

    def copies(ee, ss, slot):
        return [pltpu.make_async_copy(src.at[layer, ee, :, pl.ds(ss * ts, ts)], dst.at[slot], sem.at[i, slot])
                for i, (src, dst) in enumerate(zip(w_hbm, w_f32))]

    slot = (run_ref[e] + s * nruns_ref[0]) % 2

    @pl.when((s == 0) & (b == 0))
    def _():
        for cp in copies(e, s, slot):
            cp.start()

    e_next = next_ref[e]
    s_next = s + (e_next <= e).astype(jnp.int32)

    @pl.when(s_next < pl.num_programs(0))
    def _():
        for cp in copies(e_next, s_next, 1 - slot):
            cp.start()

    for cp in copies(e, s, slot):
        cp.wait()
    for dst, src in zip(w_bf, w_f32):
        dst[...] = src[slot].astype(BF16)


def _moe_up_kernel(*refs, layer):
    plan, (x_ref, wg_hbm, wu_hbm, h_ref, wg_f32, wu_f32, wg_bf, wu_bf, sem) = refs[:5], refs[5:]
    blk_e_ref, nblk_ref = plan[0], plan[1]
    b = pl.program_id(1)

    @pl.when(b < nblk_ref[0])
    def _():
        @pl.when(_new_expert(blk_e_ref, b))
        def _():
            _next_weights(plan, (wg_hbm, wu_hbm), (wg_f32, wu_f32), (wg_bf, wu_bf), sem, layer)

        x = x_ref[...]
        gate = _dot(x, wg_bf[...])
        up = _dot(x, wu_bf[...])
        h_ref[...] = (jax.nn.silu(gate) * up).astype(BF16)

    @pl.when(b >= nblk_ref[0])
    def _():
        h_ref[...] = jnp.zeros(h_ref.shape, BF16)


def _moe_up(xg, w_gate, w_up, layer, plan):
    P, D = xg.shape
    De = w_gate.shape[3]
    tj = _pick(De, (512, 256, 128))
    nb = P // MOE_ROWS
    any_spec = pl.BlockSpec(memory_space=pl.ANY)
    return pl.pallas_call(
        functools.partial(_moe_up_kernel, layer=layer),
        grid_spec=pltpu.PrefetchScalarGridSpec(
            num_scalar_prefetch=len(plan),
            grid=(De // tj, nb),
            in_specs=[pl.BlockSpec((MOE_ROWS, D), lambda j, b, *_: (b, 0)), any_spec, any_spec],
            out_specs=pl.BlockSpec((MOE_ROWS, tj), lambda j, b, *_: (b, j)),
            scratch_shapes=[pltpu.VMEM((2, D, tj), F32), pltpu.VMEM((2, D, tj), F32),
                            pltpu.VMEM((D, tj), BF16), pltpu.VMEM((D, tj), BF16),
                            pltpu.SemaphoreType.DMA((2, 2))],
        ),
        out_shape=jax.ShapeDtypeStruct((P, De), BF16),
        compiler_params=_cparams(("arbitrary", "arbitrary")),
        name="moe_up",
    )(*plan, xg, w_gate, w_up)


def _moe_down_kernel(*refs, layer):
    plan, (h_ref, wd_hbm, y_ref, wd_f32, wd_bf, sem) = refs[:5], refs[5:]
    blk_e_ref, nblk_ref = plan[0], plan[1]
    b = pl.program_id(1)

    @pl.when(b < nblk_ref[0])
    def _():
        @pl.when(_new_expert(blk_e_ref, b))
        def _():
            _next_weights(plan, (wd_hbm,), (wd_f32,), (wd_bf,), sem, layer)

        y = _dot(h_ref[...], wd_bf[...])
        for cc in range(y_ref.shape[1]):
            y_ref[:, cc, :] = y[:, cc * LANES:(cc + 1) * LANES]

    @pl.when(b >= nblk_ref[0])
    def _():
        y_ref[...] = jnp.zeros(y_ref.shape, F32)


def _moe_down(h, w_down, layer, plan):
    P, De = h.shape
    D = w_down.shape[3]
    tn = _pick(D, (2048, 1024, 512, 256, 128))
    nb = P // MOE_ROWS
    return pl.pallas_call(
        functools.partial(_moe_down_kernel, layer=layer),
        grid_spec=pltpu.PrefetchScalarGridSpec(
            num_scalar_prefetch=len(plan),
            grid=(D // tn, nb),
            in_specs=[pl.BlockSpec((MOE_ROWS, De), lambda n, b, *_: (b, 0)),
                      pl.BlockSpec(memory_space=pl.ANY)],
            out_specs=pl.BlockSpec((MOE_ROWS, tn // LANES, LANES), lambda n, b, *_: (b, n, 0)),
            scratch_shapes=[pltpu.VMEM((2, De, tn), F32), pltpu.VMEM((De, tn), BF16),
                            pltpu.SemaphoreType.DMA((1, 2))],
        ),
        out_shape=jax.ShapeDtypeStruct((P, D // LANES, LANES), F32),
        compiler_params=_cparams(("arbitrary", "arbitrary")),
        name="moe_down",
    )(*plan, h, w_down)


def _dispatch_plan(eid, rank, counts):
    R = eid.shape[0]
    S = R * TOP_K
    n_experts = counts.shape[0]
    e_flat = eid.reshape(-1)
    padded = (counts + MOE_ROWS - 1) // MOE_ROWS * MOE_ROWS
    pad_end = jnp.cumsum(padded)
    dest = ((pad_end - padded)[e_flat] + rank.reshape(-1)).astype(jnp.int32)
    nb = -(-S // MOE_ROWS) + n_experts
    buf_tok = jnp.full((nb * MOE_ROWS,), -1, jnp.int32).at[dest].set(jnp.arange(S, dtype=jnp.int32) // TOP_K)
    blk_e = jnp.minimum(jnp.searchsorted(pad_end, jnp.arange(nb, dtype=jnp.int32) * MOE_ROWS, side='right'),
                        n_experts - 1).astype(jnp.int32)
    nblk = (pad_end[-1:] // MOE_ROWS).astype(jnp.int32)
    ids = jnp.arange(n_experts, dtype=jnp.int32)
    live = counts > 0
    later = jnp.where(live[None, :] & (ids[None, :] > ids[:, None]), ids[None, :], n_experts)
    nxt = jnp.min(later, axis=1)
    next_e = jnp.where(nxt == n_experts, jnp.argmax(live), nxt).astype(jnp.int32)
    run_idx = (jnp.cumsum(live) - 1).astype(jnp.int32)
    n_runs = jnp.sum(live).astype(jnp.int32)[None]
    return dest, buf_tok, (blk_e, nblk, run_idx, next_e, n_runs)


def _ple_kernel(slot_ref, x_ref, y_hbm, wts_ref, p_ref, g_ref, gd_ref, gu_ref, win_ref, gf_ref, *rest,
                final, head_tiles, n_tok):
    n_out = 2 if final else 1
    o_refs = rest[:n_out]
    buf0, buf1, ysel, sem = rest[n_out:]
    bufs = (buf0, buf1)
    m = pl.program_id(0)
    tm = x_ref.shape[0]
    nc = y_hbm.shape[1]

    def fetch(tile, slot):
        _issue_rows(y_hbm, lambda r: slot_ref[(r // tm) * n_tok + tile * tm + r % tm], TOP_K * tm,
                    bufs[slot], sem.at[slot])

    @pl.when(m == 0)
    def _():
        fetch(0, 0)

    for slot in range(2):
        @pl.when(m % 2 == slot)
        def _(slot=slot):
            @pl.when(m + 1 < pl.num_programs(0))
            def _():
                fetch(m + 1, 1 - slot)

            _wait_rows(y_hbm, None, TOP_K * tm, bufs[slot], sem.at[slot])
            for k in range(TOP_K):
                for cc in range(nc):
                    ysel[k, :, cc * LANES:(cc + 1) * LANES] = (
                        bufs[slot][pl.ds(k * tm * _pitch(nc) + cc, tm, stride=_pitch(nc)), :])

    wts = wts_ref[...]
    x = x_ref[...]
    for k in range(TOP_K):
        x = x + ysel[k] * wts[:, k:k + 1]
    hn = _rms(x, g_ref[...]).astype(BF16)
    t = _dot(hn, gd_ref[...]).astype(BF16)
    gate = jax.nn.sigmoid(_dot(t, gu_ref[...]))
    pe = _dot(p_ref[...], win_ref[...])
    x = x + pe * gate
    if not final:
        o_refs[0][...] = x
        return
    x = _rms(x, gf_ref[...])
    m = pl.program_id(0)

    @pl.when(m < head_tiles)
    def _():
        o_refs[0][...] = x

    @pl.when(m >= head_tiles)
    def _():
        o_refs[1][...] = x


def _ple(x, y3, slot_rows, wts, p, g, gd, gu, win, gf, final, head_rows):
    R, D = x.shape
    Pd = p.shape[1]
    L = wts.shape[1]
    nc = y3.shape[1]
    tm = _pick(R, (128, 64, 32, 16, 8))
    nt = R // tm
    row = pl.BlockSpec((tm, D), lambda m, sl: (m, 0))
    fixed = lambda shape: pl.BlockSpec(shape, lambda m, sl: (0, 0))
    assert head_rows % tm == 0
    ht = head_rows // tm
    if final:
        out_specs = [pl.BlockSpec((tm, D), lambda m, sl: (jnp.minimum(m, ht - 1), 0)),
                     pl.BlockSpec((tm, D), lambda m, sl: (jnp.maximum(m - ht, 0), 0))]
        out_shape = [jax.ShapeDtypeStruct((head_rows, D), F32), jax.ShapeDtypeStruct((R - head_rows, D), F32)]
    else:
        out_specs = row
        out_shape = jax.ShapeDtypeStruct((R, D), F32)
    stage = pltpu.VMEM((TOP_K * tm * _pitch(nc), LANES), F32)
    return pl.pallas_call(
        functools.partial(_ple_kernel, final=final, head_tiles=ht, n_tok=R),
        grid_spec=pltpu.PrefetchScalarGridSpec(
            num_scalar_prefetch=1,
            grid=(nt,),
            in_specs=[
                row,
                pl.BlockSpec(memory_space=pl.ANY),
                pl.BlockSpec((tm, L), lambda m, sl: (m, 0)),
                pl.BlockSpec((tm, Pd), lambda m, sl: (m, 0)),
                fixed((1, D)), fixed((D, Pd)), fixed((Pd, D)), fixed((Pd, D)), fixed((1, D)),
            ],
            out_specs=out_specs,
            scratch_shapes=[stage, stage, pltpu.VMEM((TOP_K, tm, D), F32), pltpu.SemaphoreType.DMA((2,))],
        ),
        out_shape=out_shape,
        compiler_params=_cparams(("arbitrary",)),
        name="moe_combine_ple",
    )(slot_rows, x, y3, wts, p, g, gd, gu, win, gf)


def _moe_ple(x, p, layer, prm, final, head_rows):
    R, D = x.shape
    assert (R * TOP_K) % MOE_ROWS == 0 and D % LANES == 0
    n_grp = prm["moe_w_grp"].shape[2]
    n_exp = prm["moe_w_exp"].shape[2]
    L = LANES
    w_cat = jnp.concatenate([prm["moe_w_grp"][layer], prm["moe_w_exp"][layer],
                             jnp.zeros((D, L - n_grp - n_exp), F32)], axis=1)
    b_cat = jnp.concatenate([prm["moe_b_grp"][layer], prm["moe_b_exp"][layer],
                             jnp.zeros((L - n_grp - n_exp,), F32)])[None, :]
    hn, wts, route, cnt = _router(x, prm["norm_ffn"][layer][None, :], w_cat, b_cat, n_grp, n_exp // n_grp)
    dest, buf_tok, plan = _dispatch_plan(
        route[:, :TOP_K], route[:, TOP_K:2 * TOP_K], cnt[0, :n_exp].astype(jnp.int32))
    xg = _gather_rows(hn, buf_tok, plan[1])
    hmid = _moe_up(xg, prm["moe_w_gate"], prm["moe_w_up"], layer, plan)
    y3 = _moe_down(hmid, prm["moe_w_down"], layer, plan)
    slot_rows = dest.reshape(R, TOP_K).T.reshape(-1)
    return _ple(x, y3, slot_rows, wts, p, prm["norm_ple"][layer][None, :],
                prm["ple_gate_down"][layer].astype(BF16), prm["ple_gate_up"][layer].astype(BF16),
                prm["ple_w_in"][layer].astype(BF16), prm["norm_final"][None, :], final, head_rows)


def kernel(x_prompt, x_sample, state_rwkv_shift, state_rwkv_wkv, state_pool, p_prompt, p_sample,
           rwkv_mu, rwkv_w_rkv, rwkv_w_o, rwkv_w0, rwkv_w1, rwkv_w2, rwkv_a0, rwkv_a1, rwkv_a2,
           rwkv_g1, rwkv_g2, rwkv_k_k, rwkv_k_a, rwkv_r_k, rwkv_lnx_w, rwkv_lnx_b,
           pool_w, pool_scale, norm_mix, norm_ffn, norm_ple, norm_final,
           moe_w_grp, moe_b_grp, moe_w_exp, moe_b_exp, moe_w_gate, moe_w_up, moe_w_down,
           ple_w_in, ple_gate_down, ple_gate_up):
    Bp, Tp, D = x_prompt.shape
    Bs, Ts, _ = x_sample.shape
    depth = norm_mix.shape[0]
    H, hs = rwkv_r_k.shape[1], rwkv_r_k.shape[2]
    W = QUAD * hs
    assert D % W == 0 and Ts <= CHUNK and Tp % CHUNK == 0
    lay = _SeqLayout(Bp, Tp, Bs, Ts)
    Rp = Bp * Tp
    prm = dict(moe_w_grp=moe_w_grp, moe_b_grp=moe_b_grp, moe_w_exp=moe_w_exp, moe_b_exp=moe_b_exp,
               moe_w_gate=moe_w_gate, moe_w_up=moe_w_up, moe_w_down=moe_w_down, norm_ffn=norm_ffn,
               norm_ple=norm_ple, norm_final=norm_final, ple_w_in=ple_w_in, ple_gate_down=ple_gate_down,
               ple_gate_up=ple_gate_up)

    x = jnp.concatenate([x_prompt.reshape(Rp, D), x_sample.reshape(Bs * Ts, D)], axis=0)
    Pd = p_prompt.shape[-1]
    p_all = jnp.concatenate([p_prompt.reshape(depth, Rp, Pd), p_sample.reshape(depth, Bs * Ts, Pd)],
                            axis=1).astype(BF16)

    def slabs(vec):
        return vec.reshape(D // W, 1, W)

    shift_p, wkv_p, pool_p, shift_s, wkv_s, pool_s = [], [], [], [], [], []
    for i in range(depth):
        j = i // 2
        g_mix = norm_mix[i][None, :]
        if i % 2 == 0:
            shift_all = jnp.concatenate([jnp.zeros((Bp, D), F32), state_rwkv_shift[j]], axis=0)[:, None, :]
            mixed, h_last = _norm_mix(x, shift_all, g_mix, rwkv_mu[j], lay)
            xr, xw, xk, xv, xa, xg = mixed
            w_rkv = rwkv_w_rkv[j].astype(BF16)
            r = _mm(xr, w_rkv, w_group=0, slab_out=W, name="proj_r")
            k = _mm(xk, w_rkv, w_group=1, slab_out=W, name="proj_k")
            v = _mm(xv, w_rkv, w_group=2, slab_out=W, name="proj_v")
            lw = _lora(xw, rwkv_w1[j].astype(BF16), rwkv_w2[j].astype(BF16), rwkv_w0[j][None, :], "decay", W)
            a = _lora(xa, rwkv_a1[j].astype(BF16), rwkv_a2[j].astype(BF16), rwkv_a0[j][None, :], "aaa", W)
            gl = rwkv_g1.shape[2]
            glp = -(-gl // LANES) * LANES
            g1 = jnp.pad(rwkv_g1[j], ((0, 0), (0, glp - gl))).astype(BF16)
            g2 = jnp.pad(rwkv_g2[j], ((0, glp - gl), (0, 0))).astype(BF16)
            g = _lora(xg, g1, g2, jnp.zeros((1, D), F32), "gate", W)
            sprm = (slabs(rwkv_k_k[j]), slabs(rwkv_k_a[j]), slabs(rwkv_r_k[j].reshape(D)),
                    slabs(rwkv_lnx_w[j]), slabs(rwkv_lnx_b[j]))
            o_p, sp = _scan(r, k, v, lw, a, g, sprm, None, row0=0, n_seq=Bp, t_len=Tp, hs=hs)
            o_s, ss = _scan(r, k, v, lw, a, g, sprm, state_rwkv_wkv[j], row0=Rp, n_seq=Bs, t_len=Ts, hs=hs)
            o = jnp.concatenate([o_p, o_s], axis=1)
            x = _mm(o, rwkv_w_o[j].astype(BF16)[None], res=x, slab_in=True, name="proj_o")
            shift_p.append(h_last[:Bp, -1])
            shift_s.append(h_last[Bp:, -1])
            wkv_p.append(sp)
            wkv_s.append(ss)
        else:
            hist = jnp.concatenate([jnp.zeros((Bp, 16, D), F32),
                                    jnp.pad(state_pool[j], ((0, 0), (1, 0), (0, 0)))], axis=0)
            d, h_last = _pool(x, hist, g_mix, lay)
            x = _mm(d, pool_w[j].astype(BF16), scale=pool_scale[j][None, :], res=x, name="pool_proj")
            nh = state_pool.shape[2]
            pool_p.append(h_last[:Bp, SEQ_TILE - nh:])
            pool_s.append(h_last[Bp:, SEQ_TILE - nh:])
        x = _moe_ple(x, p_all[i], i, prm, final=(i == depth - 1), head_rows=Rp)

    y_prompt = x[0].reshape(Bp, Tp, D)
    y_sample = x[1].reshape(Bs, Ts, D)
    return (y_prompt, y_sample, jnp.stack(shift_p), jnp.stack(wkv_p), jnp.stack(pool_p),
            jnp.stack(shift_s), jnp.stack(wkv_s), jnp.stack(pool_s))
```

```python
import functools

import jax
import jax.numpy as jnp
from jax import lax
from jax.experimental import pallas as pl
from jax.experimental.pallas import tpu as pltpu

F32 = jnp.float32
BF16 = jnp.bfloat16

NORM_EPS = 1e-6
GN_EPS = 64e-5
PAST_LEN = 4096
POOL_WINDOWS = (2, 4, 8, 16)
TOP_K = 2
LANES = 128
SEQ_TILE = 32
CHUNK = 64
QUAD = 4
SCAN_GROUP = 8
MOE_ROWS = 256
ISSUE_UNROLL = 8
VMEM_LIMIT = 56 * 1024 * 1024


def _cparams(sem):
    return pltpu.CompilerParams(dimension_semantics=sem, vmem_limit_bytes=VMEM_LIMIT)


def _pick(n, cands):
    for c in cands:
        if n % c == 0:
            return c
    return n


def _rms(x, g):
    return x * lax.rsqrt(jnp.mean(x * x, axis=-1, keepdims=True) + NORM_EPS) * g


_NN = (((1,), (0,)), ((), ()))
_NT = (((1,), (1,)), ((), ()))
_TN = (((0,), (0,)), ((), ()))


def _dot(a, b, dims=_NN, prec=None):
    return lax.dot_general(a, b, dims, precision=prec, preferred_element_type=F32)


def _mm_kernel(*refs, has_res, has_scale, a_slabs, o_slabs):
    a_ref, w_ref = refs[0], refs[1]
    k = 2
    if a_slabs:
        a = jnp.concatenate([a_ref[q] for q in range(a_slabs)], axis=1)
    else:
        a = a_ref[...]
    acc = _dot(a, w_ref[...])
    if has_scale:
        acc = acc * refs[k][...]
        k += 1
    if has_res:
        acc = refs[k][...] + acc
        k += 1
    o_ref = refs[k]
    if o_slabs:
        sw = o_ref.shape[2]
        for q in range(o_slabs):
            o_ref[q] = acc[:, q * sw:(q + 1) * sw].astype(o_ref.dtype)
    else:
        o_ref[...] = acc.astype(o_ref.dtype)


def _mm(a, w, *, res=None, scale=None, out_dtype=F32, slab_in=False, slab_out=0, w_group=None, name="mm"):
    M = a.shape[1] if slab_in else a.shape[0]
    G, Kg, Ng = w.shape
    w0 = 0
    if w_group is not None:
        G, w0 = 1, w_group
    tm = _pick(M, (512, 256, 128, 64, 32, 16, 8))
    tn = _pick(Ng, (1024, 512, 256, 128))
    nn = Ng // tn
    if slab_in:
        assert G == 1
        a_spec = pl.BlockSpec((a.shape[0], tm, a.shape[2]), lambda g, n, m: (0, m, 0))
    else:
        a_spec = pl.BlockSpec((tm, Kg), lambda g, n, m: (m, g))
    in_specs = [a_spec, pl.BlockSpec((None, Kg, tn), lambda g, n, m: (w0 + g, 0, n))]
    args = [a, w]
    if scale is not None:
        in_specs.append(pl.BlockSpec((1, tn), lambda g, n, m: (0, g * nn + n)))
        args.append(scale)
    if res is not None:
        in_specs.append(pl.BlockSpec((tm, tn), lambda g, n, m: (m, g * nn + n)))
        args.append(res)
    if slab_out:
        assert tn % slab_out == 0
        per = tn // slab_out
        out_spec = pl.BlockSpec((per, tm, slab_out), lambda g, n, m: (g * nn + n, m, 0))
        out_shape = jax.ShapeDtypeStruct((G * Ng // slab_out, M, slab_out), out_dtype)
    else:
        per = 0
        out_spec = pl.BlockSpec((tm, tn), lambda g, n, m: (m, g * nn + n))
        out_shape = jax.ShapeDtypeStruct((M, G * Ng), out_dtype)
    return pl.pallas_call(
        functools.partial(_mm_kernel, has_res=res is not None, has_scale=scale is not None,
                          a_slabs=a.shape[0] if slab_in else 0, o_slabs=per),
        grid=(G, nn, M // tm),
        in_specs=in_specs,
        out_specs=out_spec,
        out_shape=out_shape,
        compiler_params=_cparams(("arbitrary", "arbitrary", "arbitrary")),
        name=name,
    )(*args)


def _lora_kernel(a_ref, w1_ref, w2_ref, b_ref, o_ref, *, mode):
    t = _dot(a_ref[...], w1_ref[...])
    if mode == "decay":
        t = jnp.tanh(t)
    elif mode == "gate":
        t = jax.nn.sigmoid(t)
    z = _dot(t.astype(BF16), w2_ref[...]) + b_ref[...]
    if mode == "decay":
        z = -jnp.exp(-jax.nn.softplus(-z) - 0.5)
    elif mode == "aaa":
        z = jax.nn.sigmoid(z)
    sw = o_ref.shape[2]
    for q in range(o_ref.shape[0]):
        o_ref[q] = z[:, q * sw:(q + 1) * sw]


def _lora(a, w1, w2, bias, mode, sw):
    M, K = a.shape
    L = w1.shape[1]
    D = w2.shape[1]
    tm = _pick(M, (256, 128, 64, 32, 16, 8))
    return pl.pallas_call(
        functools.partial(_lora_kernel, mode=mode),
        grid=(M // tm,),
        in_specs=[
            pl.BlockSpec((tm, K), lambda m: (m, 0)),
            pl.BlockSpec((K, L), lambda m: (0, 0)),
            pl.BlockSpec((L, D), lambda m: (0, 0)),
            pl.BlockSpec((1, D), lambda m: (0, 0)),
        ],
        out_specs=pl.BlockSpec((D // sw, tm, sw), lambda m: (0, m, 0)),
        out_shape=jax.ShapeDtypeStruct((D // sw, M, sw), F32),
        compiler_params=_cparams(("arbitrary",)),
        name="lora_" + mode,
    )(a, w1, w2, bias)


class _SeqLayout:
    def __init__(self, bp, tp, bs, ts):
        assert tp % SEQ_TILE == 0 and ts % SEQ_TILE == 0
        self.bp, self.tp, self.bs, self.ts = bp, tp, bs, ts
        self.tiles_p = tp // SEQ_TILE
        self.tiles_s = ts // SEQ_TILE
        self.n_tiles_p = bp * self.tiles_p
        self.n_tiles = self.n_tiles_p + bs * self.tiles_s
        self.n_seq = bp + bs
        self.rows = bp * tp + bs * ts

    def seq_of_tile(self, i):
        return jnp.where(i < self.n_tiles_p, i // self.tiles_p,
                         self.bp + (i - self.n_tiles_p) // self.tiles_s)

    def tile_in_seq(self, i):
        return jnp.where(i < self.n_tiles_p, i % self.tiles_p, (i - self.n_tiles_p) % self.tiles_s)


def _norm_mix_kernel(x_ref, xp_ref, sh_ref, g_ref, mu_ref, *outs, lay):
    i = pl.program_id(0)
    g = g_ref[...]
    h = _rms(x_ref[...], g)
    h_before = _rms(xp_ref[7:8, :], g)
    first = lay.tile_in_seq(i) == 0
    prev_row = jnp.where(first, sh_ref[...], h_before)
    rows = lax.broadcasted_iota(jnp.int32, h.shape, 0)
    h_prev = jnp.where(rows == 0, prev_row, pltpu.roll(h, 1, axis=0))
    xx = h_prev - h
    for n in range(6):
        outs[n][...] = (h + xx * mu_ref[n:n + 1, :]).astype(BF16)
    outs[6][...] = h


def _norm_mix(x, shift_all, g, mu, lay):
    R, D = x.shape
    tpb = SEQ_TILE // 8
    row_spec = pl.BlockSpec((SEQ_TILE, D), lambda i: (i, 0))
    outs = pl.pallas_call(
        functools.partial(_norm_mix_kernel, lay=lay),
        grid=(lay.n_tiles,),
        in_specs=[
            row_spec,
            pl.BlockSpec((8, D), lambda i: (jnp.maximum(i * tpb - 1, 0), 0)),
            pl.BlockSpec((None, 1, D), lambda i: (lay.seq_of_tile(i), 0, 0)),
            pl.BlockSpec((1, D), lambda i: (0, 0)),
            pl.BlockSpec((6, D), lambda i: (0, 0)),
        ],
        out_specs=[row_spec] * 6 + [pl.BlockSpec((None, SEQ_TILE, D), lambda i: (lay.seq_of_tile(i), 0, 0))],
        out_shape=[jax.ShapeDtypeStruct((R, D), BF16)] * 6
        + [jax.ShapeDtypeStruct((lay.n_seq, SEQ_TILE, D), F32)],
        compiler_params=_cparams(("arbitrary",)),
        name="norm_mix",
    )(x, x, shift_all, g, mu)
    return outs[:6], outs[6]


def _scan_kernel(r_ref, k_ref, v_ref, lw_ref, a_ref, g_ref, kk_ref, ka_ref, rk_ref, lnw_ref, lnb_ref,
                 *rest, t_blk, hs, group, n_chunks):
    s0_ref = rest[0] if len(rest) == 4 else None
    o_ref, sout_ref, s_ref = rest[-3:]
    C = t_blk
    nq, _, W = r_ref.shape
    c = pl.program_id(1)

    def head_block(q, j):
        return (q, slice(j * hs, (j + 1) * hs), slice(j * hs, (j + 1) * hs))

    @pl.when(c == 0)
    def _():
        s_ref[...] = jnp.zeros(s_ref.shape, F32)
        if s0_ref is not None:
            for q in range(nq):
                for j in range(QUAD):
                    s_ref[head_block(q, j)] = s0_ref[q * QUAD + j]

    ri = lax.broadcasted_iota(jnp.int32, (W, W), 0)
    ci = lax.broadcasted_iota(jnp.int32, (W, W), 1)
    head_blk = (ri // hs) == (ci // hs)
    ones_blk = jnp.where(head_blk, 1.0, 0.0).astype(F32)
    ti = lax.broadcasted_iota(jnp.int32, (C, C), 0)
    si = lax.broadcasted_iota(jnp.int32, (C, C), 1)
    tri_incl = jnp.where(ti >= si, 1.0, 0.0).astype(F32)
    lane_head = lax.broadcasted_iota(jnp.int32, (C, W), 1) // hs
    rj = lax.broadcasted_iota(jnp.int32, (QUAD * C, QUAD * C), 0)
    cj = lax.broadcasted_iota(jnp.int32, (QUAD * C, QUAD * C), 1)
    t_in = rj % C
    s_in = cj % C
    strict = jnp.where(t_in > s_in, 1.0, 0.0).astype(F32)
    incl = jnp.where(t_in >= s_in, 1.0, 0.0).astype(F32)
    eye = jnp.where(rj == cj, 1.0, 0.0).astype(F32)

    ones_bf = ones_blk.astype(BF16)
    head_masks = [jnp.where(lane_head == h, 1.0, 0.0).astype(BF16) for h in range(QUAD)]

    def fold(x):
        y = x[0:C]
        for h in range(1, QUAD):
            y = y + x[h * C:(h + 1) * C]
        return y

    def group_body(it, carry, *, passes, group):
        qs = [it * group + j for j in range(group)]
        each = lambda f, *xs: [f(*t) for t in zip(*xs)]

        full = passes == 6

        def parts(x):
            if full:
                return (x,)
            hi = x.astype(BF16)
            if passes == 1:
                return (hi,)
            return (hi, (x - hi.astype(F32)).astype(BF16))

        def bdot(x, y, dims=_NN):
            if full:
                return _dot(x[0], y[0], dims, lax.Precision.HIGHEST)
            out = _dot(x[0], y[0], dims)
            if passes > 1:
                out = out + (_dot(x[0], y[1], dims) + _dot(x[1], y[0], dims))
            return out

        def head_sum(x):
            if full:
                return _dot(x, ones_blk, _NN, lax.Precision.HIGHEST)
            out = _dot(parts(x)[0], ones_bf)
            if passes > 1:
                out = out + _dot(parts(x)[1], ones_bf)
            return out

        def stack(x):
            return tuple(jnp.concatenate([xb * m for m in head_masks], axis=0) for xb in parts(x))

        def load(ref):
            return [ref[q] for q in qs]

        r, k, v, lw, a = load(r_ref), load(k_ref), load(v_ref), load(lw_ref), load(a_ref)
        kk = each(lambda k_, q: k_ * kk_ref[q], k, qs)
        kk = each(lambda x: x * lax.rsqrt(jnp.maximum(head_sum(x * x), 1e-24)), kk)
        kf = each(lambda k_, a_, q: k_ * (1.0 + (a_ - 1.0) * ka_ref[q]), k, a, qs)
        bv = each(lambda x, a_: x * a_, kk, a)

        cum = each(lambda x: _dot(tri_incl, x, _NN, lax.Precision.HIGHEST), lw)
        e_neg = each(lambda x: jnp.exp(-x), cum)
        rt = each(lambda r_, x: r_ * jnp.exp(x), r, cum)
        at = each(lambda x, c_, l_: -x * jnp.exp(c_ - l_), kk, cum, lw)
        kt = each(lambda x, e: x * e, kf, e_neg)
        bt = each(lambda x, e: x * e, bv, e_neg)
        p_c = each(lambda x: jnp.exp(x[C - 1:C, :]), cum)

        a_st, r_st, b_st, k_st, v_st = (each(stack, x) for x in (at, rt, bt, kt, v))
        a_ab = each(lambda x, y: bdot(x, y, _NT) * strict, a_st, b_st)
        a_ak = each(lambda x, y: parts(bdot(x, y, _NT) * strict), a_st, k_st)
        a_rb = each(lambda x, y: parts(bdot(x, y, _NT) * incl), r_st, b_st)
        a_rk = each(lambda x, y: parts(bdot(x, y, _NT) * incl), r_st, k_st)

        t_inv = each(lambda x: eye + x, a_ab)
        n_pow = each(parts, a_ab)
        p = 1
        while 2 * p < C:
            n_pow = each(lambda x: parts(bdot(x, x)), n_pow)
            t_inv = each(lambda t, n: t + bdot(parts(t), n), t_inv, n_pow)
            p *= 2
        t_bf = each(parts, t_inv)

        s = [s_ref[q] for q in qs]
        s_bf = each(parts, s)
        u_in = each(lambda a_, s_, ak, v_: parts(bdot(a_, s_, _NT) + bdot(ak, v_)), a_st, s_bf, a_ak, v_st)
        u_st = each(bdot, t_bf, u_in)
        o_st = each(lambda r_, s_, rb, u_, rk, v_: bdot(r_, s_, _NT) + bdot(rb, parts(u_)) + bdot(rk, v_),
                    r_st, s_bf, a_rb, u_st, a_rk, v_st)
        o = each(fold, o_st)
        u = each(fold, u_st)
        upd = each(lambda u_, v_, b_, k_, pc: bdot(parts(jnp.concatenate([u_, v_], axis=0)),
                                                   parts(jnp.concatenate([b_ * pc, k_ * pc], axis=0)), _TN),
                   u, v, bt, kt, p_c)
        for q, s_, pc, up in zip(qs, s, p_c, upd):
            s_ref[q] = s_ * pc + jnp.where(head_blk, up, 0.0)

        inv_n = 1.0 / hs
        o_c = each(lambda x: x - head_sum(x) * inv_n, o)
        o_n = each(lambda x: x * lax.rsqrt(head_sum(x * x) * inv_n + GN_EPS), o_c)
        bonus = each(lambda r_, k_, q: head_sum(r_ * k_ * rk_ref[q]), r, kf, qs)
        for q, x, bo, v_ in zip(qs, o_n, bonus, v):
            y = x * lnw_ref[q] + lnb_ref[q] + bo * v_
            o_ref[q] = (y * g_ref[q]).astype(o_ref.dtype)
        return carry

    def run(passes, grp):
        lax.fori_loop(0, nq // grp, functools.partial(group_body, passes=passes, group=grp), 0)

    if n_chunks > 1:
        last = n_chunks - 1

        @pl.when(c < last)
        def _():
            run(1, group)

        @pl.when(c == last)
        def _():
            run(6, _pick(nq, (2, 1)))
    else:
        run(1, group)

    @pl.when(c == n_chunks - 1)
    def _():
        for q in range(nq):
            for j in range(QUAD):
                sout_ref[q * QUAD + j] = s_ref[head_block(q, j)]


def _scan(r, k, v, lw, a, g, prm, s0, *, row0, n_seq, t_len, hs):
    nq, _, W = r.shape
    H = nq * QUAD
    t_blk = min(CHUNK, t_len)
    n_chunks = t_len // t_blk
    blk0 = row0 // t_blk
    row_spec = pl.BlockSpec((nq, t_blk, W), lambda b, c: (0, blk0 + b * n_chunks + c, 0))
    prm_spec = pl.BlockSpec((nq, 1, W), lambda b, c: (0, 0, 0))
    st_spec = pl.BlockSpec((None, H, hs, hs), lambda b, c: (b, 0, 0, 0))
    return pl.pallas_call(
        functools.partial(_scan_kernel, t_blk=t_blk, hs=hs, group=_pick(nq, (SCAN_GROUP, 2, 1)),
                          n_chunks=n_chunks),
        grid=(n_seq, n_chunks),
        in_specs=[row_spec] * 6 + [prm_spec] * 5 + ([] if s0 is None else [st_spec]),
        out_specs=[pl.BlockSpec((nq, t_blk, W), lambda b, c: (0, b * n_chunks + c, 0)), st_spec],
        out_shape=[jax.ShapeDtypeStruct((nq, n_seq * t_len, W), BF16),
                   jax.ShapeDtypeStruct((n_seq, H, hs, hs), F32)],
        scratch_shapes=[pltpu.VMEM((nq, W, W), F32)],
        compiler_params=_cparams(("arbitrary", "arbitrary")),
        name="rwkv_scan",
    )(r, k, v, lw, a, g, *prm, *(() if s0 is None else (s0,)))


def _pool_kernel(x_ref, xp_ref, hist_ref, g_ref, d_ref, hl_ref, *, lay):
    i = pl.program_id(0)
    g = g_ref[...]
    h = _rms(x_ref[...], g)
    tis = lay.tile_in_seq(i)
    prev = jnp.where(tis == 0, hist_ref[...], _rms(xp_ref[...], g))
    ext = jnp.concatenate([prev, h], axis=0)
    D = h.shape[1]
    gw = D // len(POOL_WINDOWS)
    pos0 = jnp.where(i < lay.n_tiles_p, 0, PAST_LEN) + tis * SEQ_TILE
    pos = pos0 + lax.broadcasted_iota(jnp.int32, (SEQ_TILE, 1), 0)
    for gi, w in enumerate(POOL_WINDOWS):
        e = ext[:, gi * gw:(gi + 1) * gw]
        s = e
        step = 1
        while step < w:
            s = s + pltpu.roll(s, step, axis=0)
            step *= 2
        cnt = jnp.minimum(pos + 1, w).astype(F32)
        mean = s[16:] / cnt
        d_ref[:, gi * gw:(gi + 1) * gw] = (mean - h[:, gi * gw:(gi + 1) * gw]).astype(BF16)
    hl_ref[...] = h


def _pool(x, hist16, g, lay):
    R, D = x.shape
    tp16 = SEQ_TILE // 16
    return pl.pallas_call(
        functools.partial(_pool_kernel, lay=lay),
        grid=(lay.n_tiles,),
        in_specs=[
            pl.BlockSpec((SEQ_TILE, D), lambda i: (i, 0)),
            pl.BlockSpec((16, D), lambda i: (jnp.maximum(i * tp16 - 1, 0), 0)),
            pl.BlockSpec((None, 16, D), lambda i: (lay.seq_of_tile(i), 0, 0)),
            pl.BlockSpec((1, D), lambda i: (0, 0)),
        ],
        out_specs=[pl.BlockSpec((SEQ_TILE, D), lambda i: (i, 0)),
                   pl.BlockSpec((None, SEQ_TILE, D), lambda i: (lay.seq_of_tile(i), 0, 0))],
        out_shape=[jax.ShapeDtypeStruct((R, D), BF16),
                   jax.ShapeDtypeStruct((lay.n_seq, SEQ_TILE, D), F32)],
        compiler_params=_cparams(("arbitrary",)),
        name="pool_mix",
    )(x, x, hist16, g)


def _router_kernel(x_ref, g_ref, w_ref, b_ref, hn_ref, wts_ref, eid_ref, cnt_ref, *, n_grp, per_grp):
    hn = _rms(x_ref[...], g_ref[...])
    hb = hn.astype(BF16)
    bits = lax.bitcast_convert_type(hb.astype(F32), jnp.uint32)
    for c in range(hn_ref.shape[1]):
        lo = bits[:, (2 * c) * LANES:(2 * c + 1) * LANES] >> 16
        hi = bits[:, (2 * c + 1) * LANES:(2 * c + 2) * LANES] & jnp.uint32(0xFFFF0000)
        hn_ref[:, c, :] = lo | hi
    logits = _dot(hn, w_ref[...], prec=lax.Precision.HIGHEST) + b_ref[...]
    lane = lax.broadcasted_iota(jnp.int32, logits.shape, 1).astype(F32)
    neg = -jnp.inf
    big = float(logits.shape[1])
    gl = jnp.where(lane < n_grp, logits, neg)
    gmax = jnp.max(gl, axis=-1, keepdims=True)
    gsum = jnp.sum(jnp.exp(gl - gmax), axis=-1, keepdims=True)
    g_idx = jnp.min(jnp.where(gl == gmax, lane, big), axis=-1, keepdims=True)
    g_gate = 1.0 / gsum
    lo = n_grp + g_idx * per_grp
    el = jnp.where((lane >= lo) & (lane < lo + per_grp), logits, neg)
    e1 = jnp.max(el, axis=-1, keepdims=True)
    i1 = jnp.min(jnp.where(el == e1, lane, big), axis=-1, keepdims=True)
    el2 = jnp.where(lane == i1, neg, el)
    e2 = jnp.max(el2, axis=-1, keepdims=True)
    i2 = jnp.min(jnp.where(el2 == e2, lane, big), axis=-1, keepdims=True)
    esum = jnp.sum(jnp.exp(el - e1), axis=-1, keepdims=True)
    p1 = 1.0 / esum
    p2 = jnp.exp(e2 - e1) / esum
    w1 = g_gate * p1 / (p1 + p2)
    w2 = g_gate * p2 / (p1 + p2)
    wts_ref[...] = jnp.where(lane == 0, w1, jnp.where(lane == 1, w2, 0.0))

    @pl.when(pl.program_id(0) == 0)
    def _():
        cnt_ref[...] = jnp.zeros(cnt_ref.shape, F32)

    e1 = i1 - n_grp
    e2 = i2 - n_grp
    oh1 = jnp.where(lane == e1, 1.0, 0.0)
    oh2 = jnp.where(lane == e2, 1.0, 0.0)
    both = oh1 + oh2
    tm = both.shape[0]
    before = (lax.broadcasted_iota(jnp.int32, (tm, tm), 1) < lax.broadcasted_iota(jnp.int32, (tm, tm), 0))
    seen = _dot(jnp.where(before, 1.0, 0.0).astype(BF16), both.astype(BF16)) + cnt_ref[...]
    rank1 = jnp.sum(oh1 * seen, axis=-1, keepdims=True)
    rank2 = jnp.sum(oh2 * seen, axis=-1, keepdims=True)
    cnt_ref[...] += jnp.sum(both, axis=0, keepdims=True)
    eid_ref[...] = jnp.where(lane == 0, e1, jnp.where(lane == 1, e2, jnp.where(
        lane == 2, rank1, jnp.where(lane == 3, rank2, 0.0)))).astype(jnp.int32)


def _router(x, g, w_cat, b_cat, n_grp, per_grp):
    R, D = x.shape
    tm = _pick(R, (256, 128, 64, 32, 16, 8))
    L = w_cat.shape[1]
    return pl.pallas_call(
        functools.partial(_router_kernel, n_grp=n_grp, per_grp=per_grp),
        grid=(R // tm,),
        in_specs=[
            pl.BlockSpec((tm, D), lambda m: (m, 0)),
            pl.BlockSpec((1, D), lambda m: (0, 0)),
            pl.BlockSpec((D, L), lambda m: (0, 0)),
            pl.BlockSpec((1, L), lambda m: (0, 0)),
        ],
        out_specs=[pl.BlockSpec((tm, D // (2 * LANES), LANES), lambda m: (m, 0, 0)),
                   pl.BlockSpec((tm, L), lambda m: (m, 0)),
                   pl.BlockSpec((tm, L), lambda m: (m, 0)),
                   pl.BlockSpec((1, L), lambda m: (0, 0))],
        out_shape=[jax.ShapeDtypeStruct((R, D // (2 * LANES), LANES), jnp.uint32),
                   jax.ShapeDtypeStruct((R, L), F32),
                   jax.ShapeDtypeStruct((R, L), jnp.int32),
                   jax.ShapeDtypeStruct((1, L), F32)],
        compiler_params=_cparams(("arbitrary",)),
        name="moe_router",
    )(x, g, w_cat, b_cat)


def _pitch(nc):
    return nc + 4


def _issue_rows(src_ref, row_of, n_rows, buf, sem, holes=False):
    nc = src_ref.shape[1]

    def issue(j, carry):
        for u in range(ISSUE_UNROLL):
            r = ISSUE_UNROLL * j + u
            row = row_of(r)

            def start(r=r, row=row, u=u):
                pltpu.make_async_copy(src_ref.at[row], buf.at[pl.ds(r * _pitch(nc), nc)], sem).start(priority=u % 2)

            if holes:
                pl.when(row >= 0)(start)
            else:
                start()
        return carry

    assert n_rows % ISSUE_UNROLL == 0
    lax.fori_loop(0, n_rows // ISSUE_UNROLL, issue, 0)


def _wait_rows(n_rows, nc, buf, sem):
    pltpu.make_async_copy(buf.at[pl.ds(0, n_rows * nc)], buf.at[pl.ds(0, n_rows * nc)], sem).wait()


def _gather_kernel(idx_ref, nblk_ref, nvalid_ref, src_ref, o_ref, buf0, buf1, sem):
    G = o_ref.shape[0]
    nc = src_ref.shape[1]
    b = pl.program_id(0)
    nblk = nblk_ref[0]
    bufs = (buf0, buf1)

    def rows_of(blk):
        return lambda r: idx_ref[blk * G + r]

    def fetch(blk, slot):
        _issue_rows(src_ref, rows_of(blk), G, bufs[slot], sem.at[slot], holes=True)

    @pl.when(b == 0)
    def _():
        for buf in bufs:
            buf[...] = jnp.zeros(buf.shape, buf.dtype)

    @pl.when((b == 0) & (nblk > 0))
    def _():
        fetch(0, 0)

    for slot in range(2):
        @pl.when((b < nblk) & (b % 2 == slot))
        def _(slot=slot):
            @pl.when(b + 1 < nblk)
            def _():
                fetch(b + 1, 1 - slot)

            _wait_rows(nvalid_ref[b], nc, bufs[slot], sem.at[slot])
            for cc in range(nc):
                w = bufs[slot][pl.ds(cc, G, stride=_pitch(nc)), :]
                lo = lax.bitcast_convert_type(w << 16, F32)
                hi = lax.bitcast_convert_type(w & jnp.uint32(0xFFFF0000), F32)
                o_ref[:, (2 * cc) * LANES:(2 * cc + 1) * LANES] = lo.astype(BF16)
                o_ref[:, (2 * cc + 1) * LANES:(2 * cc + 2) * LANES] = hi.astype(BF16)

    @pl.when(b >= nblk_ref[0])
    def _():
        o_ref[...] = jnp.zeros(o_ref.shape, o_ref.dtype)


def _gather_rows(src3, idx, nblk):
    n = idx.shape[0]
    _, nc, lanes = src3.shape
    G = MOE_ROWS
    assert n % G == 0 and lanes == LANES and src3.dtype == jnp.uint32
    nvalid = jnp.sum((idx >= 0).reshape(n // G, G), axis=1).astype(jnp.int32)
    return pl.pallas_call(
        _gather_kernel,
        grid_spec=pltpu.PrefetchScalarGridSpec(
            num_scalar_prefetch=3,
            grid=(n // G,),
            in_specs=[pl.BlockSpec(memory_space=pl.ANY)],
            out_specs=pl.BlockSpec((G, 2 * nc * LANES), lambda b, *_: (b, 0)),
            scratch_shapes=[pltpu.VMEM((G * _pitch(nc), LANES), jnp.uint32),
                            pltpu.VMEM((G * _pitch(nc), LANES), jnp.uint32),
                            pltpu.SemaphoreType.DMA((2,))],
        ),
        out_shape=jax.ShapeDtypeStruct((n, 2 * nc * LANES), BF16),
        compiler_params=_cparams(("arbitrary",)),
        name="row_gather",
    )(idx, nblk, nvalid, src3)


def _new_expert(blk_e_ref, b):
    prev = blk_e_ref[jnp.maximum(b - 1, 0)]
    return (b == 0) | (blk_e_ref[b] != prev)


def _next_weights(plan, w_hbm, w_f32, w_bf, sem, layer):
    blk_e_ref, _, run_ref, next_ref, nruns_ref = plan
    s, b = pl.program_id(0), pl.program_id(1)
    e = blk_e_ref[b]
    ts = w_bf[0].shape[1]

    def copies(ee, ss, slot):
        return [pltpu.make_async_copy(src.at[layer, ee, :, pl.ds(ss * ts, ts)], dst.at[slot], sem.at[i, slot])
                for i, (src, dst) in enumerate(zip(w_hbm, w_f32))]

    slot = (run_ref[e] + s * nruns_ref[0]) % 2

    @pl.when((s == 0) & (b == 0))
    def _():
        for cp in copies(e, s, slot):
            cp.start()

    e_next = next_ref[e]
    s_next = s + (e_next <= e).astype(jnp.int32)

    @pl.when(s_next < pl.num_programs(0))
    def _():
        for cp in copies(e_next, s_next, 1 - slot):
            cp.start()

    for cp in copies(e, s, slot):
        cp.wait()
    for dst, src in zip(w_bf, w_f32):
        dst[...] = src[slot].astype(BF16)


def _moe_up_kernel(*refs, layer):
    plan, (x_ref, wg_hbm, wu_hbm, h_ref, wg_f32, wu_f32, wg_bf, wu_bf, sem) = refs[:5], refs[5:]
    blk_e_ref, nblk_ref = plan[0], plan[1]
    b = pl.program_id(1)

    @pl.when(b < nblk_ref[0])
    def _():
        @pl.when(_new_expert(blk_e_ref, b))
        def _():
            _next_weights(plan, (wg_hbm, wu_hbm), (wg_f32, wu_f32), (wg_bf, wu_bf), sem, layer)

        x = x_ref[...]
        gate = _dot(x, wg_bf[...])
        up = _dot(x, wu_bf[...])
        h_ref[...] = (jax.nn.silu(gate) * up).astype(BF16)

    @pl.when(b >= nblk_ref[0])
    def _():
        h_ref[...] = jnp.zeros(h_ref.shape, BF16)


def _moe_up(xg, w_gate, w_up, layer, plan):
    P, D = xg.shape
    De = w_gate.shape[3]
    tj = _pick(De, (512, 256, 128))
    nb = P // MOE_ROWS
    any_spec = pl.BlockSpec(memory_space=pl.ANY)
    return pl.pallas_call(
        functools.partial(_moe_up_kernel, layer=layer),
        grid_spec=pltpu.PrefetchScalarGridSpec(
            num_scalar_prefetch=len(plan),
            grid=(De // tj, nb),
            in_specs=[pl.BlockSpec((MOE_ROWS, D), lambda j, b, *_: (b, 0)), any_spec, any_spec],
            out_specs=pl.BlockSpec((MOE_ROWS, tj), lambda j, b, *_: (b, j)),
            scratch_shapes=[pltpu.VMEM((2, D, tj), F32), pltpu.VMEM((2, D, tj), F32),
                            pltpu.VMEM((D, tj), BF16), pltpu.VMEM((D, tj), BF16),
                            pltpu.SemaphoreType.DMA((2, 2))],
        ),
        out_shape=jax.ShapeDtypeStruct((P, De), BF16),
        compiler_params=_cparams(("arbitrary", "arbitrary")),
        name="moe_up",
    )(*plan, xg, w_gate, w_up)


def _moe_down_kernel(*refs, layer):
    plan, (h_ref, wd_hbm, y_ref, wd_f32, wd_bf, sem) = refs[:5], refs[5:]
    blk_e_ref, nblk_ref = plan[0], plan[1]
    b = pl.program_id(1)

    @pl.when(b < nblk_ref[0])
    def _():
        @pl.when(_new_expert(blk_e_ref, b))
        def _():
            _next_weights(plan, (wd_hbm,), (wd_f32,), (wd_bf,), sem, layer)

        y = _dot(h_ref[...], wd_bf[...])
        for cc in range(y_ref.shape[1]):
            y_ref[:, cc, :] = y[:, cc * LANES:(cc + 1) * LANES]

    @pl.when(b >= nblk_ref[0])
    def _():
        y_ref[...] = jnp.zeros(y_ref.shape, F32)


def _moe_down(h, w_down, layer, plan):
    P, De = h.shape
    D = w_down.shape[3]
    tn = _pick(D, (2048, 1024, 512, 256, 128))
    nb = P // MOE_ROWS
    return pl.pallas_call(
        functools.partial(_moe_down_kernel, layer=layer),
        grid_spec=pltpu.PrefetchScalarGridSpec(
            num_scalar_prefetch=len(plan),
            grid=(D // tn, nb),
            in_specs=[pl.BlockSpec((MOE_ROWS, De), lambda n, b, *_: (b, 0)),
                      pl.BlockSpec(memory_space=pl.ANY)],
            out_specs=pl.BlockSpec((MOE_ROWS, tn // LANES, LANES), lambda n, b, *_: (b, n, 0)),
            scratch_shapes=[pltpu.VMEM((2, De, tn), F32), pltpu.VMEM((De, tn), BF16),
                            pltpu.SemaphoreType.DMA((1, 2))],
        ),
        out_shape=jax.ShapeDtypeStruct((P, D // LANES, LANES), F32),
        compiler_params=_cparams(("arbitrary", "arbitrary")),
        name="moe_down",
    )(*plan, h, w_down)


def _dispatch_plan(eid, rank, counts):
    R = eid.shape[0]
    S = R * TOP_K
    n_experts = counts.shape[0]
    e_flat = eid.reshape(-1)
    padded = (counts + MOE_ROWS - 1) // MOE_ROWS * MOE_ROWS
    pad_end = jnp.cumsum(padded)
    dest = ((pad_end - padded)[e_flat] + rank.reshape(-1)).astype(jnp.int32)
    nb = -(-S // MOE_ROWS) + n_experts
    buf_tok = jnp.full((nb * MOE_ROWS,), -1, jnp.int32).at[dest].set(jnp.arange(S, dtype=jnp.int32) // TOP_K)
    blk_e = jnp.minimum(jnp.searchsorted(pad_end, jnp.arange(nb, dtype=jnp.int32) * MOE_ROWS, side='right'),
                        n_experts - 1).astype(jnp.int32)
    nblk = (pad_end[-1:] // MOE_ROWS).astype(jnp.int32)
    ids = jnp.arange(n_experts, dtype=jnp.int32)
    live = counts > 0
    later = jnp.where(live[None, :] & (ids[None, :] > ids[:, None]), ids[None, :], n_experts)
    nxt = jnp.min(later, axis=1)
    next_e = jnp.where(nxt == n_experts, jnp.argmax(live), nxt).astype(jnp.int32)
    run_idx = (jnp.cumsum(live) - 1).astype(jnp.int32)
    n_runs = jnp.sum(live).astype(jnp.int32)[None]
    return dest, buf_tok, (blk_e, nblk, run_idx, next_e, n_runs)


def _ple_kernel(slot_ref, x_ref, y_hbm, wts_ref, p_ref, g_ref, gd_ref, gu_ref, win_ref, gf_ref, *rest,
                final, head_tiles, n_tok):
    n_out = 2 if final else 1
    o_refs = rest[:n_out]
    buf0, buf1, ysel, sem = rest[n_out:]
    bufs = (buf0, buf1)
    m = pl.program_id(0)
    tm = x_ref.shape[0]
    nc = y_hbm.shape[1]

    def fetch(tile, slot):
        _issue_rows(y_hbm, lambda r: slot_ref[(r // tm) * n_tok + tile * tm + r % tm], TOP_K * tm,
                    bufs[slot], sem.at[slot])

    @pl.when(m == 0)
    def _():
        fetch(0, 0)

    for slot in range(2):
        @pl.when(m % 2 == slot)
        def _(slot=slot):
            @pl.when(m + 1 < pl.num_programs(0))
            def _():
                fetch(m + 1, 1 - slot)

            _wait_rows(TOP_K * tm, nc, bufs[slot], sem.at[slot])
            for k in range(TOP_K):
                for cc in range(nc):
                    ysel[k, :, cc * LANES:(cc + 1) * LANES] = (
                        bufs[slot][pl.ds(k * tm * _pitch(nc) + cc, tm, stride=_pitch(nc)), :])

    wts = wts_ref[...]
    x = x_ref[...]
    for k in range(TOP_K):
        x = x + ysel[k] * wts[:, k:k + 1]
    hn = _rms(x, g_ref[...]).astype(BF16)
    t = _dot(hn, gd_ref[...]).astype(BF16)
    gate = jax.nn.sigmoid(_dot(t, gu_ref[...]))
    pe = _dot(p_ref[...], win_ref[...])
    x = x + pe * gate
    if not final:
        o_refs[0][...] = x
        return
    x = _rms(x, gf_ref[...])
    m = pl.program_id(0)

    @pl.when(m < head_tiles)
    def _():
        o_refs[0][...] = x

    @pl.when(m >= head_tiles)
    def _():
        o_refs[1][...] = x


def _ple(x, y3, slot_rows, wts, p, g, gd, gu, win, gf, final, head_rows):
    R, D = x.shape
    Pd = p.shape[1]
    L = wts.shape[1]
    nc = y3.shape[1]
    tm = _pick(R, (128, 64, 32, 16, 8))
    nt = R // tm
    row = pl.BlockSpec((tm, D), lambda m, sl: (m, 0))
    fixed = lambda shape: pl.BlockSpec(shape, lambda m, sl: (0, 0))
    assert head_rows % tm == 0
    ht = head_rows // tm
    if final:
        out_specs = [pl.BlockSpec((tm, D), lambda m, sl: (jnp.minimum(m, ht - 1), 0)),
                     pl.BlockSpec((tm, D), lambda m, sl: (jnp.maximum(m - ht, 0), 0))]
        out_shape = [jax.ShapeDtypeStruct((head_rows, D), F32), jax.ShapeDtypeStruct((R - head_rows, D), F32)]
    else:
        out_specs = row
        out_shape = jax.ShapeDtypeStruct((R, D), F32)
    stage = pltpu.VMEM((TOP_K * tm * _pitch(nc), LANES), F32)
    return pl.pallas_call(
        functools.partial(_ple_kernel, final=final, head_tiles=ht, n_tok=R),
        grid_spec=pltpu.PrefetchScalarGridSpec(
            num_scalar_prefetch=1,
            grid=(nt,),
            in_specs=[
                row,
                pl.BlockSpec(memory_space=pl.ANY),
                pl.BlockSpec((tm, L), lambda m, sl: (m, 0)),
                pl.BlockSpec((tm, Pd), lambda m, sl: (m, 0)),
                fixed((1, D)), fixed((D, Pd)), fixed((Pd, D)), fixed((Pd, D)), fixed((1, D)),
            ],
            out_specs=out_specs,
            scratch_shapes=[stage, stage, pltpu.VMEM((TOP_K, tm, D), F32), pltpu.SemaphoreType.DMA((2,))],
        ),
        out_shape=out_shape,
        compiler_params=_cparams(("arbitrary",)),
        name="moe_combine_ple",
    )(slot_rows, x, y3, wts, p, g, gd, gu, win, gf)


def _moe_ple(x, p, layer, prm, final, head_rows):
    R, D = x.shape
    assert (R * TOP_K) % MOE_ROWS == 0 and D % LANES == 0
    n_grp = prm["moe_w_grp"].shape[2]
    n_exp = prm["moe_w_exp"].shape[2]
    L = LANES
    w_cat = jnp.concatenate([prm["moe_w_grp"][layer], prm["moe_w_exp"][layer],
                             jnp.zeros((D, L - n_grp - n_exp), F32)], axis=1)
    b_cat = jnp.concatenate([prm["moe_b_grp"][layer], prm["moe_b_exp"][layer],
                             jnp.zeros((L - n_grp - n_exp,), F32)])[None, :]
    hn, wts, route, cnt = _router(x, prm["norm_ffn"][layer][None, :], w_cat, b_cat, n_grp, n_exp // n_grp)
    dest, buf_tok, plan = _dispatch_plan(
        route[:, :TOP_K], route[:, TOP_K:2 * TOP_K], cnt[0, :n_exp].astype(jnp.int32))
    xg = _gather_rows(hn, buf_tok, plan[1])
    hmid = _moe_up(xg, prm["moe_w_gate"], prm["moe_w_up"], layer, plan)
    y3 = _moe_down(hmid, prm["moe_w_down"], layer, plan)
    slot_rows = dest.reshape(R, TOP_K).T.reshape(-1)
    return _ple(x, y3, slot_rows, wts, p, prm["norm_ple"][layer][None, :],
                prm["ple_gate_down"][layer].astype(BF16), prm["ple_gate_up"][layer].astype(BF16),
                prm["ple_w_in"][layer].astype(BF16), prm["norm_final"][None, :], final, head_rows)


def kernel(x_prompt, x_sample, state_rwkv_shift, state_rwkv_wkv, state_pool, p_prompt, p_sample,
           rwkv_mu, rwkv_w_rkv, rwkv_w_o, rwkv_w0, rwkv_w1, rwkv_w2, rwkv_a0, rwkv_a1, rwkv_a2,
           rwkv_g1, rwkv_g2, rwkv_k_k, rwkv_k_a, rwkv_r_k, rwkv_lnx_w, rwkv_lnx_b,
           pool_w, pool_scale, norm_mix, norm_ffn, norm_ple, norm_final,
           moe_w_grp, moe_b_grp, moe_w_exp, moe_b_exp, moe_w_gate, moe_w_up, moe_w_down,
           ple_w_in, ple_gate_down, ple_gate_up):
    Bp, Tp, D = x_prompt.shape
    Bs, Ts, _ = x_sample.shape
    depth = norm_mix.shape[0]
    H, hs = rwkv_r_k.shape[1], rwkv_r_k.shape[2]
    W = QUAD * hs
    assert D % W == 0 and Ts <= CHUNK and Tp % CHUNK == 0
    lay = _SeqLayout(Bp, Tp, Bs, Ts)
    Rp = Bp * Tp
    prm = dict(moe_w_grp=moe_w_grp, moe_b_grp=moe_b_grp, moe_w_exp=moe_w_exp, moe_b_exp=moe_b_exp,
               moe_w_gate=moe_w_gate, moe_w_up=moe_w_up, moe_w_down=moe_w_down, norm_ffn=norm_ffn,
               norm_ple=norm_ple, norm_final=norm_final, ple_w_in=ple_w_in, ple_gate_down=ple_gate_down,
               ple_gate_up=ple_gate_up)

    x = jnp.concatenate([x_prompt.reshape(Rp, D), x_sample.reshape(Bs * Ts, D)], axis=0)
    Pd = p_prompt.shape[-1]
    p_all = jnp.concatenate([p_prompt.reshape(depth, Rp, Pd), p_sample.reshape(depth, Bs * Ts, Pd)],
                            axis=1).astype(BF16)

    def slabs(vec):
        return vec.reshape(D // W, 1, W)

    shift_p, wkv_p, pool_p, shift_s, wkv_s, pool_s = [], [], [], [], [], []
    for i in range(depth):
        j = i // 2
        g_mix = norm_mix[i][None, :]
        if i % 2 == 0:
            shift_all = jnp.concatenate([jnp.zeros((Bp, D), F32), state_rwkv_shift[j]], axis=0)[:, None, :]
            mixed, h_last = _norm_mix(x, shift_all, g_mix, rwkv_mu[j], lay)
            xr, xw, xk, xv, xa, xg = mixed
            w_rkv = rwkv_w_rkv[j].astype(BF16)
            r = _mm(xr, w_rkv, w_group=0, slab_out=W, name="proj_r")
            k = _mm(xk, w_rkv, w_group=1, slab_out=W, name="proj_k")
            v = _mm(xv, w_rkv, w_group=2, slab_out=W, name="proj_v")
            lw = _lora(xw, rwkv_w1[j].astype(BF16), rwkv_w2[j].astype(BF16), rwkv_w0[j][None, :], "decay", W)
            a = _lora(xa, rwkv_a1[j].astype(BF16), rwkv_a2[j].astype(BF16), rwkv_a0[j][None, :], "aaa", W)
            gl = rwkv_g1.shape[2]
            glp = -(-gl // LANES) * LANES
            g1 = jnp.pad(rwkv_g1[j], ((0, 0), (0, glp - gl))).astype(BF16)
            g2 = jnp.pad(rwkv_g2[j], ((0, glp - gl), (0, 0))).astype(BF16)
            g = _lora(xg, g1, g2, jnp.zeros((1, D), F32), "gate", W)
            sprm = (slabs(rwkv_k_k[j]), slabs(rwkv_k_a[j]), slabs(rwkv_r_k[j].reshape(D)),
                    slabs(rwkv_lnx_w[j]), slabs(rwkv_lnx_b[j]))
            o_p, sp = _scan(r, k, v, lw, a, g, sprm, None, row0=0, n_seq=Bp, t_len=Tp, hs=hs)
            o_s, ss = _scan(r, k, v, lw, a, g, sprm, state_rwkv_wkv[j], row0=Rp, n_seq=Bs, t_len=Ts, hs=hs)
            o = jnp.concatenate([o_p, o_s], axis=1)
            x = _mm(o, rwkv_w_o[j].astype(BF16)[None], res=x, slab_in=True, name="proj_o")
            shift_p.append(h_last[:Bp, -1])
            shift_s.append(h_last[Bp:, -1])
            wkv_p.append(sp)
            wkv_s.append(ss)
        else:
            hist = jnp.concatenate([jnp.zeros((Bp, 16, D), F32),
                                    jnp.pad(state_pool[j], ((0, 0), (1, 0), (0, 0)))], axis=0)
            d, h_last = _pool(x, hist, g_mix, lay)
            x = _mm(d, pool_w[j].astype(BF16), scale=pool_scale[j][None, :], res=x, name="pool_proj")
            nh = state_pool.shape[2]
            pool_p.append(h_last[:Bp, SEQ_TILE - nh:])
            pool_s.append(h_last[Bp:, SEQ_TILE - nh:])
        x = _moe_ple(x, p_all[i], i, prm, final=(i == depth - 1), head_rows=Rp)

    y_prompt = x[0].reshape(Bp, Tp, D)
    y_sample = x[1].reshape(Bs, Ts, D)
    return (y_prompt, y_sample, jnp.stack(shift_p), jnp.stack(wkv_p), jnp.stack(pool_p),
            jnp.stack(shift_s), jnp.stack(wkv_s), jnp.stack(pool_s))
```

```python
import functools

import jax
import jax.numpy as jnp
from jax import lax
from jax.experimental import pallas as pl
from jax.experimental.pallas import tpu as pltpu

F32 = jnp.float32
BF16 = jnp.bfloat16

NORM_EPS = 1e-6
GN_EPS = 64e-5
PAST_LEN = 4096
POOL_WINDOWS = (2, 4, 8, 16)
TOP_K = 2
LANES = 128
SEQ_TILE = 32
CHUNK = 64
QUAD = 4
SCAN_GROUP = 8
MOE_ROWS = 256
ISSUE_UNROLL = 8
VMEM_LIMIT = 56 * 1024 * 1024


def _cparams(sem):
    return pltpu.CompilerParams(dimension_semantics=sem, vmem_limit_bytes=VMEM_LIMIT)


def _pick(n, cands):
    for c in cands:
        if n % c == 0:
            return c
    return n


def _rms(x, g):
    return x * lax.rsqrt(jnp.mean(x * x, axis=-1, keepdims=True) + NORM_EPS) * g


_NN = (((1,), (0,)), ((), ()))
_NT = (((1,), (1,)), ((), ()))
_TN = (((0,), (0,)), ((), ()))


def _dot(a, b, dims=_NN, prec=None):
    return lax.dot_general(a, b, dims, precision=prec, preferred_element_type=F32)


def _mm_kernel(*refs, has_res, has_scale, a_slabs, o_slabs):
    a_ref, w_ref = refs[0], refs[1]
    k = 2
    if a_slabs:
        a = jnp.concatenate([a_ref[q] for q in range(a_slabs)], axis=1)
    else:
        a = a_ref[...]
    acc = _dot(a, w_ref[...])
    if has_scale:
        acc = acc * refs[k][...]
        k += 1
    if has_res:
        acc = refs[k][...] + acc
        k += 1
    o_ref = refs[k]
    if o_slabs:
        sw = o_ref.shape[2]
        for q in range(o_slabs):
            o_ref[q] = acc[:, q * sw:(q + 1) * sw].astype(o_ref.dtype)
    else:
        o_ref[...] = acc.astype(o_ref.dtype)


def _mm(a, w, *, res=None, scale=None, out_dtype=F32, slab_in=False, slab_out=0, w_group=None, name="mm"):
    M = a.shape[1] if slab_in else a.shape[0]
    G, Kg, Ng = w.shape
    w0 = 0
    if w_group is not None:
        G, w0 = 1, w_group
    tm = _pick(M, (512, 256, 128, 64, 32, 16, 8))
    tn = _pick(Ng, (1024, 512, 256, 128))
    nn = Ng // tn
    if slab_in:
        assert G == 1
        a_spec = pl.BlockSpec((a.shape[0], tm, a.shape[2]), lambda g, n, m: (0, m, 0))
    else:
        a_spec = pl.BlockSpec((tm, Kg), lambda g, n, m: (m, g))
    in_specs = [a_spec, pl.BlockSpec((None, Kg, tn), lambda g, n, m: (w0 + g, 0, n))]
    args = [a, w]
    if scale is not None:
        in_specs.append(pl.BlockSpec((1, tn), lambda g, n, m: (0, g * nn + n)))
        args.append(scale)
    if res is not None:
        in_specs.append(pl.BlockSpec((tm, tn), lambda g, n, m: (m, g * nn + n)))
        args.append(res)
    if slab_out:
        assert tn % slab_out == 0
        per = tn // slab_out
        out_spec = pl.BlockSpec((per, tm, slab_out), lambda g, n, m: (g * nn + n, m, 0))
        out_shape = jax.ShapeDtypeStruct((G * Ng // slab_out, M, slab_out), out_dtype)
    else:
        per = 0
        out_spec = pl.BlockSpec((tm, tn), lambda g, n, m: (m, g * nn + n))
        out_shape = jax.ShapeDtypeStruct((M, G * Ng), out_dtype)
    return pl.pallas_call(
        functools.partial(_mm_kernel, has_res=res is not None, has_scale=scale is not None,
                          a_slabs=a.shape[0] if slab_in else 0, o_slabs=per),
        grid=(G, nn, M // tm),
        in_specs=in_specs,
        out_specs=out_spec,
        out_shape=out_shape,
        compiler_params=_cparams(("arbitrary", "arbitrary", "arbitrary")),
        name=name,
    )(*args)


def _lora_kernel(a_ref, w1_ref, w2_ref, b_ref, o_ref, *, mode):
    t = _dot(a_ref[...], w1_ref[...])
    if mode == "decay":
        t = jnp.tanh(t)
    elif mode == "gate":
        t = jax.nn.sigmoid(t)
    z = _dot(t.astype(BF16), w2_ref[...]) + b_ref[...]
    if mode == "decay":
        z = -jnp.exp(-jax.nn.softplus(-z) - 0.5)
    elif mode == "aaa":
        z = jax.nn.sigmoid(z)
    sw = o_ref.shape[2]
    for q in range(o_ref.shape[0]):
        o_ref[q] = z[:, q * sw:(q + 1) * sw]


def _lora(a, w1, w2, bias, mode, sw):
    M, K = a.shape
    L = w1.shape[1]
    D = w2.shape[1]
    tm = _pick(M, (256, 128, 64, 32, 16, 8))
    return pl.pallas_call(
        functools.partial(_lora_kernel, mode=mode),
        grid=(M // tm,),
        in_specs=[
            pl.BlockSpec((tm, K), lambda m: (m, 0)),
            pl.BlockSpec((K, L), lambda m: (0, 0)),
            pl.BlockSpec((L, D), lambda m: (0, 0)),
            pl.BlockSpec((1, D), lambda m: (0, 0)),
        ],
        out_specs=pl.BlockSpec((D // sw, tm, sw), lambda m: (0, m, 0)),
        out_shape=jax.ShapeDtypeStruct((D // sw, M, sw), F32),
        compiler_params=_cparams(("arbitrary",)),
        name="lora_" + mode,
    )(a, w1, w2, bias)


class _SeqLayout:
    def __init__(self, bp, tp, bs, ts):
        assert tp % SEQ_TILE == 0 and ts % SEQ_TILE == 0
        self.bp, self.tp, self.bs, self.ts = bp, tp, bs, ts
        self.tiles_p = tp // SEQ_TILE
        self.tiles_s = ts // SEQ_TILE
        self.n_tiles_p = bp * self.tiles_p
        self.n_tiles = self.n_tiles_p + bs * self.tiles_s
        self.n_seq = bp + bs
        self.rows = bp * tp + bs * ts

    def seq_of_tile(self, i):
        return jnp.where(i < self.n_tiles_p, i // self.tiles_p,
                         self.bp + (i - self.n_tiles_p) // self.tiles_s)

    def tile_in_seq(self, i):
        return jnp.where(i < self.n_tiles_p, i % self.tiles_p, (i - self.n_tiles_p) % self.tiles_s)


def _norm_mix_kernel(x_ref, xp_ref, sh_ref, g_ref, mu_ref, *outs, lay):
    i = pl.program_id(0)
    g = g_ref[...]
    h = _rms(x_ref[...], g)
    h_before = _rms(xp_ref[7:8, :], g)
    first = lay.tile_in_seq(i) == 0
    prev_row = jnp.where(first, sh_ref[...], h_before)
    rows = lax.broadcasted_iota(jnp.int32, h.shape, 0)
    h_prev = jnp.where(rows == 0, prev_row, pltpu.roll(h, 1, axis=0))
    xx = h_prev - h
    for n in range(6):
        outs[n][...] = (h + xx * mu_ref[n:n + 1, :]).astype(BF16)
    outs[6][...] = h


def _norm_mix(x, shift_all, g, mu, lay):
    R, D = x.shape
    tpb = SEQ_TILE // 8
    row_spec = pl.BlockSpec((SEQ_TILE, D), lambda i: (i, 0))
    outs = pl.pallas_call(
        functools.partial(_norm_mix_kernel, lay=lay),
        grid=(lay.n_tiles,),
        in_specs=[
            row_spec,
            pl.BlockSpec((8, D), lambda i: (jnp.maximum(i * tpb - 1, 0), 0)),
            pl.BlockSpec((None, 1, D), lambda i: (lay.seq_of_tile(i), 0, 0)),
            pl.BlockSpec((1, D), lambda i: (0, 0)),
            pl.BlockSpec((6, D), lambda i: (0, 0)),
        ],
        out_specs=[row_spec] * 6 + [pl.BlockSpec((None, SEQ_TILE, D), lambda i: (lay.seq_of_tile(i), 0, 0))],
        out_shape=[jax.ShapeDtypeStruct((R, D), BF16)] * 6
        + [jax.ShapeDtypeStruct((lay.n_seq, SEQ_TILE, D), F32)],
        compiler_params=_cparams(("arbitrary",)),
        name="norm_mix",
    )(x, x, shift_all, g, mu)
    return outs[:6], outs[6]


def _scan_kernel(r_ref, k_ref, v_ref, lw_ref, a_ref, g_ref, kk_ref, ka_ref, rk_ref, lnw_ref, lnb_ref,
                 *rest, t_blk, hs, group, n_chunks):
    s0_ref = rest[0] if len(rest) == 4 else None
    o_ref, sout_ref, s_ref = rest[-3:]
    C = t_blk
    nq, _, W = r_ref.shape
    c = pl.program_id(1)

    def head_block(q, j):
        return (q, slice(j * hs, (j + 1) * hs), slice(j * hs, (j + 1) * hs))

    @pl.when(c == 0)
    def _():
        s_ref[...] = jnp.zeros(s_ref.shape, F32)
        if s0_ref is not None:
            for q in range(nq):
                for j in range(QUAD):
                    s_ref[head_block(q, j)] = s0_ref[q * QUAD + j]

    ri = lax.broadcasted_iota(jnp.int32, (W, W), 0)
    ci = lax.broadcasted_iota(jnp.int32, (W, W), 1)
    head_blk = (ri // hs) == (ci // hs)
    ones_blk = jnp.where(head_blk, 1.0, 0.0).astype(F32)
    ti = lax.broadcasted_iota(jnp.int32, (C, C), 0)
    si = lax.broadcasted_iota(jnp.int32, (C, C), 1)
    tri_incl = jnp.where(ti >= si, 1.0, 0.0).astype(F32)
    lane_head = lax.broadcasted_iota(jnp.int32, (C, W), 1) // hs
    rj = lax.broadcasted_iota(jnp.int32, (QUAD * C, QUAD * C), 0)
    cj = lax.broadcasted_iota(jnp.int32, (QUAD * C, QUAD * C), 1)
    t_in = rj % C
    s_in = cj % C
    strict = jnp.where(t_in > s_in, 1.0, 0.0).astype(F32)
    incl = jnp.where(t_in >= s_in, 1.0, 0.0).astype(F32)
    eye = jnp.where(rj == cj, 1.0, 0.0).astype(F32)

    ones_bf = ones_blk.astype(BF16)
    head_masks = [jnp.where(lane_head == h, 1.0, 0.0).astype(BF16) for h in range(QUAD)]

    def fold(x):
        y = x[0:C]
        for h in range(1, QUAD):
            y = y + x[h * C:(h + 1) * C]
        return y

    def group_body(it, carry, *, passes, group):
        qs = [it * group + j for j in range(group)]
        each = lambda f, *xs: [f(*t) for t in zip(*xs)]

        def parts(x):
            hi = x.astype(BF16)
            if passes == 1:
                return (hi,)
            return (hi, (x - hi.astype(F32)).astype(BF16))

        def bdot(x, y, dims=_NN):
            out = _dot(x[0], y[0], dims)
            if passes > 1:
                out = out + (_dot(x[0], y[1], dims) + _dot(x[1], y[0], dims))
            return out

        def head_sum(x):
            out = _dot(parts(x)[0], ones_bf)
            if passes > 1:
                out = out + _dot(parts(x)[1], ones_bf)
            return out

        def stack(x):
            return tuple(jnp.concatenate([xb * m for m in head_masks], axis=0) for xb in parts(x))

        def load(ref):
            return [ref[q] for q in qs]

        r, k, v, lw, a = load(r_ref), load(k_ref), load(v_ref), load(lw_ref), load(a_ref)
        kk = each(lambda k_, q: k_ * kk_ref[q], k, qs)
        kk = each(lambda x: x * lax.rsqrt(jnp.maximum(head_sum(x * x), 1e-24)), kk)
        kf = each(lambda k_, a_, q: k_ * (1.0 + (a_ - 1.0) * ka_ref[q]), k, a, qs)
        bv = each(lambda x, a_: x * a_, kk, a)

        cum = each(lambda x: _dot(tri_incl, x, _NN, lax.Precision.HIGHEST), lw)
        e_neg = each(lambda x: jnp.exp(-x), cum)
        rt = each(lambda r_, x: r_ * jnp.exp(x), r, cum)
        at = each(lambda x, c_, l_: -x * jnp.exp(c_ - l_), kk, cum, lw)
        kt = each(lambda x, e: x * e, kf, e_neg)
        bt = each(lambda x, e: x * e, bv, e_neg)
        p_c = each(lambda x: jnp.exp(x[C - 1:C, :]), cum)

        a_st, r_st, b_st, k_st, v_st = (each(stack, x) for x in (at, rt, bt, kt, v))
        a_ab = each(lambda x, y: bdot(x, y, _NT) * strict, a_st, b_st)
        a_ak = each(lambda x, y: parts(bdot(x, y, _NT) * strict), a_st, k_st)
        a_rb = each(lambda x, y: parts(bdot(x, y, _NT) * incl), r_st, b_st)
        a_rk = each(lambda x, y: parts(bdot(x, y, _NT) * incl), r_st, k_st)

        t_inv = each(lambda x: eye + x, a_ab)
        n_pow = each(parts, a_ab)
        p = 1
        while 2 * p < C:
            n_pow = each(lambda x: parts(bdot(x, x)), n_pow)
            t_inv = each(lambda t, n: t + bdot(parts(t), n), t_inv, n_pow)
            p *= 2
        t_bf = each(parts, t_inv)

        s = [s_ref[q] for q in qs]
        s_bf = each(parts, s)
        u_in = each(lambda a_, s_, ak, v_: parts(bdot(a_, s_, _NT) + bdot(ak, v_)), a_st, s_bf, a_ak, v_st)
        u_st = each(bdot, t_bf, u_in)
        o_st = each(lambda r_, s_, rb, u_, rk, v_: bdot(r_, s_, _NT) + bdot(rb, parts(u_)) + bdot(rk, v_),
                    r_st, s_bf, a_rb, u_st, a_rk, v_st)
        o = each(fold, o_st)
        u = each(fold, u_st)
        upd = each(lambda u_, v_, b_, k_, pc: bdot(parts(jnp.concatenate([u_, v_], axis=0)),
                                                   parts(jnp.concatenate([b_ * pc, k_ * pc], axis=0)), _TN),
                   u, v, bt, kt, p_c)
        for q, s_, pc, up in zip(qs, s, p_c, upd):
            s_ref[q] = s_ * pc + jnp.where(head_blk, up, 0.0)

        inv_n = 1.0 / hs
        o_c = each(lambda x: x - head_sum(x) * inv_n, o)
        o_n = each(lambda x: x * lax.rsqrt(head_sum(x * x) * inv_n + GN_EPS), o_c)
        bonus = each(lambda r_, k_, q: head_sum(r_ * k_ * rk_ref[q]), r, kf, qs)
        for q, x, bo, v_ in zip(qs, o_n, bonus, v):
            y = x * lnw_ref[q] + lnb_ref[q] + bo * v_
            o_ref[q] = (y * g_ref[q]).astype(o_ref.dtype)
        return carry

    def run(passes, grp):
        lax.fori_loop(0, nq // grp, functools.partial(group_body, passes=passes, group=grp), 0)

    if n_chunks > 1:
        last = n_chunks - 1

        @pl.when(c < last)
        def _():
            run(1, group)

        @pl.when(c == last)
        def _():
            run(3, _pick(nq, (2, 1)))
    else:
        run(1, group)

    @pl.when(c == n_chunks - 1)
    def _():
        for q in range(nq):
            for j in range(QUAD):
                sout_ref[q * QUAD + j] = s_ref[head_block(q, j)]


def _scan(r, k, v, lw, a, g, prm, s0, *, row0, n_seq, t_len, hs):
    nq, _, W = r.shape
    H = nq * QUAD
    t_blk = min(CHUNK, t_len)
    n_chunks = t_len // t_blk
    blk0 = row0 // t_blk
    row_spec = pl.BlockSpec((nq, t_blk, W), lambda b, c: (0, blk0 + b * n_chunks + c, 0))
    prm_spec = pl.BlockSpec((nq, 1, W), lambda b, c: (0, 0, 0))
    st_spec = pl.BlockSpec((None, H, hs, hs), lambda b, c: (b, 0, 0, 0))
    return pl.pallas_call(
        functools.partial(_scan_kernel, t_blk=t_blk, hs=hs, group=_pick(nq, (SCAN_GROUP, 2, 1)),
                          n_chunks=n_chunks),
        grid=(n_seq, n_chunks),
        in_specs=[row_spec] * 6 + [prm_spec] * 5 + ([] if s0 is None else [st_spec]),
        out_specs=[pl.BlockSpec((nq, t_blk, W), lambda b, c: (0, b * n_chunks + c, 0)), st_spec],
        out_shape=[jax.ShapeDtypeStruct((nq, n_seq * t_len, W), BF16),
                   jax.ShapeDtypeStruct((n_seq, H, hs, hs), F32)],
        scratch_shapes=[pltpu.VMEM((nq, W, W), F32)],
        compiler_params=_cparams(("arbitrary", "arbitrary")),
        name="rwkv_scan",
    )(r, k, v, lw, a, g, *prm, *(() if s0 is None else (s0,)))


def _pool_kernel(x_ref, xp_ref, hist_ref, g_ref, d_ref, hl_ref, *, lay):
    i = pl.program_id(0)
    g = g_ref[...]
    h = _rms(x_ref[...], g)
    tis = lay.tile_in_seq(i)
    prev = jnp.where(tis == 0, hist_ref[...], _rms(xp_ref[...], g))
    ext = jnp.concatenate([prev, h], axis=0)
    D = h.shape[1]
    gw = D // len(POOL_WINDOWS)
    pos0 = jnp.where(i < lay.n_tiles_p, 0, PAST_LEN) + tis * SEQ_TILE
    pos = pos0 + lax.broadcasted_iota(jnp.int32, (SEQ_TILE, 1), 0)
    for gi, w in enumerate(POOL_WINDOWS):
        e = ext[:, gi * gw:(gi + 1) * gw]
        s = e
        step = 1
        while step < w:
            s = s + pltpu.roll(s, step, axis=0)
            step *= 2
        cnt = jnp.minimum(pos + 1, w).astype(F32)
        mean = s[16:] / cnt
        d_ref[:, gi * gw:(gi + 1) * gw] = (mean - h[:, gi * gw:(gi + 1) * gw]).astype(BF16)
    hl_ref[...] = h


def _pool(x, hist16, g, lay):
    R, D = x.shape
    tp16 = SEQ_TILE // 16
    return pl.pallas_call(
        functools.partial(_pool_kernel, lay=lay),
        grid=(lay.n_tiles,),
        in_specs=[
            pl.BlockSpec((SEQ_TILE, D), lambda i: (i, 0)),
            pl.BlockSpec((16, D), lambda i: (jnp.maximum(i * tp16 - 1, 0), 0)),
            pl.BlockSpec((None, 16, D), lambda i: (lay.seq_of_tile(i), 0, 0)),
            pl.BlockSpec((1, D), lambda i: (0, 0)),
        ],
        out_specs=[pl.BlockSpec((SEQ_TILE, D), lambda i: (i, 0)),
                   pl.BlockSpec((None, SEQ_TILE, D), lambda i: (lay.seq_of_tile(i), 0, 0))],
        out_shape=[jax.ShapeDtypeStruct((R, D), BF16),
                   jax.ShapeDtypeStruct((lay.n_seq, SEQ_TILE, D), F32)],
        compiler_params=_cparams(("arbitrary",)),
        name="pool_mix",
    )(x, x, hist16, g)


def _router_kernel(x_ref, g_ref, w_ref, b_ref, hn_ref, wts_ref, eid_ref, cnt_ref, *, n_grp, per_grp):
    hn = _rms(x_ref[...], g_ref[...])
    hb = hn.astype(BF16)
    bits = lax.bitcast_convert_type(hb.astype(F32), jnp.uint32)
    for c in range(hn_ref.shape[1]):
        lo = bits[:, (2 * c) * LANES:(2 * c + 1) * LANES] >> 16
        hi = bits[:, (2 * c + 1) * LANES:(2 * c + 2) * LANES] & jnp.uint32(0xFFFF0000)
        hn_ref[:, c, :] = lo | hi
    logits = _dot(hn, w_ref[...], prec=lax.Precision.HIGHEST) + b_ref[...]
    lane = lax.broadcasted_iota(jnp.int32, logits.shape, 1).astype(F32)
    neg = -jnp.inf
    big = float(logits.shape[1])
    gl = jnp.where(lane < n_grp, logits, neg)
    gmax = jnp.max(gl, axis=-1, keepdims=True)
    gsum = jnp.sum(jnp.exp(gl - gmax), axis=-1, keepdims=True)
    g_idx = jnp.min(jnp.where(gl == gmax, lane, big), axis=-1, keepdims=True)
    g_gate = 1.0 / gsum
    lo = n_grp + g_idx * per_grp
    el = jnp.where((lane >= lo) & (lane < lo + per_grp), logits, neg)
    e1 = jnp.max(el, axis=-1, keepdims=True)
    i1 = jnp.min(jnp.where(el == e1, lane, big), axis=-1, keepdims=True)
    el2 = jnp.where(lane == i1, neg, el)
    e2 = jnp.max(el2, axis=-1, keepdims=True)
    i2 = jnp.min(jnp.where(el2 == e2, lane, big), axis=-1, keepdims=True)
    esum = jnp.sum(jnp.exp(el - e1), axis=-1, keepdims=True)
    p1 = 1.0 / esum
    p2 = jnp.exp(e2 - e1) / esum
    w1 = g_gate * p1 / (p1 + p2)
    w2 = g_gate * p2 / (p1 + p2)
    wts_ref[...] = jnp.where(lane == 0, w1, jnp.where(lane == 1, w2, 0.0))

    @pl.when(pl.program_id(0) == 0)
    def _():
        cnt_ref[...] = jnp.zeros(cnt_ref.shape, F32)

    e1 = i1 - n_grp
    e2 = i2 - n_grp
    oh1 = jnp.where(lane == e1, 1.0, 0.0)
    oh2 = jnp.where(lane == e2, 1.0, 0.0)
    both = oh1 + oh2
    tm = both.shape[0]
    before = (lax.broadcasted_iota(jnp.int32, (tm, tm), 1) < lax.broadcasted_iota(jnp.int32, (tm, tm), 0))
    seen = _dot(jnp.where(before, 1.0, 0.0).astype(BF16), both.astype(BF16)) + cnt_ref[...]
    rank1 = jnp.sum(oh1 * seen, axis=-1, keepdims=True)
    rank2 = jnp.sum(oh2 * seen, axis=-1, keepdims=True)
    cnt_ref[...] += jnp.sum(both, axis=0, keepdims=True)
    eid_ref[...] = jnp.where(lane == 0, e1, jnp.where(lane == 1, e2, jnp.where(
        lane == 2, rank1, jnp.where(lane == 3, rank2, 0.0)))).astype(jnp.int32)


def _router(x, g, w_cat, b_cat, n_grp, per_grp):
    R, D = x.shape
    tm = _pick(R, (256, 128, 64, 32, 16, 8))
    L = w_cat.shape[1]
    return pl.pallas_call(
        functools.partial(_router_kernel, n_grp=n_grp, per_grp=per_grp),
        grid=(R // tm,),
        in_specs=[
            pl.BlockSpec((tm, D), lambda m: (m, 0)),
            pl.BlockSpec((1, D), lambda m: (0, 0)),
            pl.BlockSpec((D, L), lambda m: (0, 0)),
            pl.BlockSpec((1, L), lambda m: (0, 0)),
        ],
        out_specs=[pl.BlockSpec((tm, D // (2 * LANES), LANES), lambda m: (m, 0, 0)),
                   pl.BlockSpec((tm, L), lambda m: (m, 0)),
                   pl.BlockSpec((tm, L), lambda m: (m, 0)),
                   pl.BlockSpec((1, L), lambda m: (0, 0))],
        out_shape=[jax.ShapeDtypeStruct((R, D // (2 * LANES), LANES), jnp.uint32),
                   jax.ShapeDtypeStruct((R, L), F32),
                   jax.ShapeDtypeStruct((R, L), jnp.int32),
                   jax.ShapeDtypeStruct((1, L), F32)],
        compiler_params=_cparams(("arbitrary",)),
        name="moe_router",
    )(x, g, w_cat, b_cat)


def _pitch(nc):
    return nc + 4


def _issue_rows(src_ref, row_of, n_rows, buf, sem, holes=False):
    nc = src_ref.shape[1]

    def issue(j, carry):
        for u in range(ISSUE_UNROLL):
            r = ISSUE_UNROLL * j + u
            row = row_of(r)

            def start(r=r, row=row, u=u):
                pltpu.make_async_copy(src_ref.at[row], buf.at[pl.ds(r * _pitch(nc), nc)], sem).start(priority=u % 2)

            if holes:
                pl.when(row >= 0)(start)
            else:
                start()
        return carry

    assert n_rows % ISSUE_UNROLL == 0
    lax.fori_loop(0, n_rows // ISSUE_UNROLL, issue, 0)


def _wait_rows(n_rows, nc, buf, sem):
    pltpu.make_async_copy(buf.at[pl.ds(0, n_rows * nc)], buf.at[pl.ds(0, n_rows * nc)], sem).wait()


def _gather_kernel(idx_ref, nblk_ref, nvalid_ref, src_ref, o_ref, buf0, buf1, sem):
    G = o_ref.shape[0]
    nc = src_ref.shape[1]
    b = pl.program_id(0)
    nblk = nblk_ref[0]
    bufs = (buf0, buf1)

    def rows_of(blk):
        return lambda r: idx_ref[blk * G + r]

    def fetch(blk, slot):
        _issue_rows(src_ref, rows_of(blk), G, bufs[slot], sem.at[slot], holes=True)

    @pl.when(b == 0)
    def _():
        for buf in bufs:
            buf[...] = jnp.zeros(buf.shape, buf.dtype)

    @pl.when((b == 0) & (nblk > 0))
    def _():
        fetch(0, 0)

    for slot in range(2):
        @pl.when((b < nblk) & (b % 2 == slot))
        def _(slot=slot):
            @pl.when(b + 1 < nblk)
            def _():
                fetch(b + 1, 1 - slot)

            _wait_rows(nvalid_ref[b], nc, bufs[slot], sem.at[slot])
            for cc in range(nc):
                w = bufs[slot][pl.ds(cc, G, stride=_pitch(nc)), :]
                lo = lax.bitcast_convert_type(w << 16, F32)
                hi = lax.bitcast_convert_type(w & jnp.uint32(0xFFFF0000), F32)
                o_ref[:, (2 * cc) * LANES:(2 * cc + 1) * LANES] = lo.astype(BF16)
                o_ref[:, (2 * cc + 1) * LANES:(2 * cc + 2) * LANES] = hi.astype(BF16)

    @pl.when(b >= nblk_ref[0])
    def _():
        o_ref[...] = jnp.zeros(o_ref.shape, o_ref.dtype)


def _gather_rows(src3, idx, nblk):
    n = idx.shape[0]
    _, nc, lanes = src3.shape
    G = MOE_ROWS
    assert n % G == 0 and lanes == LANES and src3.dtype == jnp.uint32
    nvalid = jnp.sum((idx >= 0).reshape(n // G, G), axis=1).astype(jnp.int32)
    return pl.pallas_call(
        _gather_kernel,
        grid_spec=pltpu.PrefetchScalarGridSpec(
            num_scalar_prefetch=3,
            grid=(n // G,),
            in_specs=[pl.BlockSpec(memory_space=pl.ANY)],
            out_specs=pl.BlockSpec((G, 2 * nc * LANES), lambda b, *_: (b, 0)),
            scratch_shapes=[pltpu.VMEM((G * _pitch(nc), LANES), jnp.uint32),
                            pltpu.VMEM((G * _pitch(nc), LANES), jnp.uint32),
                            pltpu.SemaphoreType.DMA((2,))],
        ),
        out_shape=jax.ShapeDtypeStruct((n, 2 * nc * LANES), BF16),
        compiler_params=_cparams(("arbitrary",)),
        name="row_gather",
    )(idx, nblk, nvalid, src3)


def _new_expert(blk_e_ref, b):
    prev = blk_e_ref[jnp.maximum(b - 1, 0)]
    return (b == 0) | (blk_e_ref[b] != prev)


def _next_weights(plan, w_hbm, w_f32, w_bf, sem, layer):
    blk_e_ref, _, run_ref, next_ref, nruns_ref = plan
    s, b = pl.program_id(0), pl.program_id(1)
    e = blk_e_ref[b]
    ts = w_bf[0].shape[1]

    def copies(ee, ss, slot):
        return [pltpu.make_async_copy(src.at[layer, ee, :, pl.ds(ss * ts, ts)], dst.at[slot], sem.at[i, slot])
                for i, (src, dst) in enumerate(zip(w_hbm, w_f32))]

    slot = (run_ref[e] + s * nruns_ref[0]) % 2

    @pl.when((s == 0) & (b == 0))
    def _():
        for cp in copies(e, s, slot):
            cp.start()

    e_next = next_ref[e]
    s_next = s + (e_next <= e).astype(jnp.int32)

    @pl.when(s_next < pl.num_programs(0))
    def _():
        for cp in copies(e_next, s_next, 1 - slot):
            cp.start()

    for cp in copies(e, s, slot):
        cp.wait()
    for dst, src in zip(w_bf, w_f32):
        dst[...] = src[slot].astype(BF16)


def _moe_up_kernel(*refs, layer):
    plan, (x_ref, wg_hbm, wu_hbm, h_ref, wg_f32, wu_f32, wg_bf, wu_bf, sem) = refs[:5], refs[5:]
    blk_e_ref, nblk_ref = plan[0], plan[1]
    b = pl.program_id(1)

    @pl.when(b < nblk_ref[0])
    def _():
        @pl.when(_new_expert(blk_e_ref, b))
        def _():
            _next_weights(plan, (wg_hbm, wu_hbm), (wg_f32, wu_f32), (wg_bf, wu_bf), sem, layer)

        x = x_ref[...]
        gate = _dot(x, wg_bf[...])
        up = _dot(x, wu_bf[...])
        h_ref[...] = (jax.nn.silu(gate) * up).astype(BF16)

    @pl.when(b >= nblk_ref[0])
    def _():
        h_ref[...] = jnp.zeros(h_ref.shape, BF16)


def _moe_up(xg, w_gate, w_up, layer, plan):
    P, D = xg.shape
    De = w_gate.shape[3]
    tj = _pick(De, (512, 256, 128))
    nb = P // MOE_ROWS
    any_spec = pl.BlockSpec(memory_space=pl.ANY)
    return pl.pallas_call(
        functools.partial(_moe_up_kernel, layer=layer),
        grid_spec=pltpu.PrefetchScalarGridSpec(
            num_scalar_prefetch=len(plan),
            grid=(De // tj, nb),
            in_specs=[pl.BlockSpec((MOE_ROWS, D), lambda j, b, *_: (b, 0)), any_spec, any_spec],
            out_specs=pl.BlockSpec((MOE_ROWS, tj), lambda j, b, *_: (b, j)),
            scratch_shapes=[pltpu.VMEM((2, D, tj), F32), pltpu.VMEM((2, D, tj), F32),
                            pltpu.VMEM((D, tj), BF16), pltpu.VMEM((D, tj), BF16),
                            pltpu.SemaphoreType.DMA((2, 2))],
        ),
        out_shape=jax.ShapeDtypeStruct((P, De), BF16),
        compiler_params=_cparams(("arbitrary", "arbitrary")),
        name="moe_up",
    )(*plan, xg, w_gate, w_up)


def _moe_down_kernel(*refs, layer):
    plan, (h_ref, wd_hbm, y_ref, wd_f32, wd_bf, sem) = refs[:5], refs[5:]
    blk_e_ref, nblk_ref = plan[0], plan[1]
    b = pl.program_id(1)

    @pl.when(b < nblk_ref[0])
    def _():
        @pl.when(_new_expert(blk_e_ref, b))
        def _():
            _next_weights(plan, (wd_hbm,), (wd_f32,), (wd_bf,), sem, layer)

        y = _dot(h_ref[...], wd_bf[...])
        for cc in range(y_ref.shape[1]):
            y_ref[:, cc, :] = y[:, cc * LANES:(cc + 1) * LANES]

    @pl.when(b >= nblk_ref[0])
    def _():
        y_ref[...] = jnp.zeros(y_ref.shape, F32)


def _moe_down(h, w_down, layer, plan):
    P, De = h.shape
    D = w_down.shape[3]
    tn = _pick(D, (4096, 2048, 1024, 512, 256, 128))
    nb = P // MOE_ROWS
    return pl.pallas_call(
        functools.partial(_moe_down_kernel, layer=layer),
        grid_spec=pltpu.PrefetchScalarGridSpec(
            num_scalar_prefetch=len(plan),
            grid=(D // tn, nb),
            in_specs=[pl.BlockSpec((MOE_ROWS, De), lambda n, b, *_: (b, 0)),
                      pl.BlockSpec(memory_space=pl.ANY)],
            out_specs=pl.BlockSpec((MOE_ROWS, tn // LANES, LANES), lambda n, b, *_: (b, n, 0)),
            scratch_shapes=[pltpu.VMEM((2, De, tn), F32), pltpu.VMEM((De, tn), BF16),
                            pltpu.SemaphoreType.DMA((1, 2))],
        ),
        out_shape=jax.ShapeDtypeStruct((P, D // LANES, LANES), F32),
        compiler_params=_cparams(("arbitrary", "arbitrary")),
        name="moe_down",
    )(*plan, h, w_down)


def _dispatch_plan(eid, rank, counts):
    R = eid.shape[0]
    S = R * TOP_K
    n_experts = counts.shape[0]
    e_flat = eid.reshape(-1)
    padded = (counts + MOE_ROWS - 1) // MOE_ROWS * MOE_ROWS
    pad_end = jnp.cumsum(padded)
    dest = ((pad_end - padded)[e_flat] + rank.reshape(-1)).astype(jnp.int32)
    nb = -(-S // MOE_ROWS) + n_experts
    buf_tok = jnp.full((nb * MOE_ROWS,), -1, jnp.int32).at[dest].set(jnp.arange(S, dtype=jnp.int32) // TOP_K)
    blk_e = jnp.minimum(jnp.searchsorted(pad_end, jnp.arange(nb, dtype=jnp.int32) * MOE_ROWS, side='right'),
                        n_experts - 1).astype(jnp.int32)
    nblk = (pad_end[-1:] // MOE_ROWS).astype(jnp.int32)
    ids = jnp.arange(n_experts, dtype=jnp.int32)
    live = counts > 0
    later = jnp.where(live[None, :] & (ids[None, :] > ids[:, None]), ids[None, :], n_experts)
    nxt = jnp.min(later, axis=1)
    next_e = jnp.where(nxt == n_experts, jnp.argmax(live), nxt).astype(jnp.int32)
    run_idx = (jnp.cumsum(live) - 1).astype(jnp.int32)
    n_runs = jnp.sum(live).astype(jnp.int32)[None]
    return dest, buf_tok, (blk_e, nblk, run_idx, next_e, n_runs)


def _ple_kernel(slot_ref, x_ref, y_hbm, wts_ref, p_ref, g_ref, gd_ref, gu_ref, win_ref, gf_ref, *rest,
                final, head_tiles, n_tok):
    n_out = 2 if final else 1
    o_refs = rest[:n_out]
    buf0, buf1, ysel, sem = rest[n_out:]
    bufs = (buf0, buf1)
    m = pl.program_id(0)
    tm = x_ref.shape[0]
    nc = y_hbm.shape[1]

    def fetch(tile, slot):
        _issue_rows(y_hbm, lambda r: slot_ref[(r // tm) * n_tok + tile * tm + r % tm], TOP_K * tm,
                    bufs[slot], sem.at[slot])

    @pl.when(m == 0)
    def _():
        fetch(0, 0)

    for slot in range(2):
        @pl.when(m % 2 == slot)
        def _(slot=slot):
            @pl.when(m + 1 < pl.num_programs(0))
            def _():
                fetch(m + 1, 1 - slot)

            _wait_rows(TOP_K * tm, nc, bufs[slot], sem.at[slot])
            for k in range(TOP_K):
                for cc in range(nc):
                    ysel[k, :, cc * LANES:(cc + 1) * LANES] = (
                        bufs[slot][pl.ds(k * tm * _pitch(nc) + cc, tm, stride=_pitch(nc)), :])

    wts = wts_ref[...]
    x = x_ref[...]
    for k in range(TOP_K):
        x = x + ysel[k] * wts[:, k:k + 1]
    hn = _rms(x, g_ref[...]).astype(BF16)
    t = _dot(hn, gd_ref[...]).astype(BF16)
    gate = jax.nn.sigmoid(_dot(t, gu_ref[...]))
    pe = _dot(p_ref[...], win_ref[...])
    x = x + pe * gate
    if not final:
        o_refs[0][...] = x
        return
    x = _rms(x, gf_ref[...])
    m = pl.program_id(0)

    @pl.when(m < head_tiles)
    def _():
        o_refs[0][...] = x

    @pl.when(m >= head_tiles)
    def _():
        o_refs[1][...] = x


def _ple(x, y3, slot_rows, wts, p, g, gd, gu, win, gf, final, head_rows):
    R, D = x.shape
    Pd = p.shape[1]
    L = wts.shape[1]
    nc = y3.shape[1]
    tm = _pick(R, (128, 64, 32, 16, 8))
    nt = R // tm
    row = pl.BlockSpec((tm, D), lambda m, sl: (m, 0))
    fixed = lambda shape: pl.BlockSpec(shape, lambda m, sl: (0, 0))
    assert head_rows % tm == 0
    ht = head_rows // tm
    if final:
        out_specs = [pl.BlockSpec((tm, D), lambda m, sl: (jnp.minimum(m, ht - 1), 0)),
                     pl.BlockSpec((tm, D), lambda m, sl: (jnp.maximum(m - ht, 0), 0))]
        out_shape = [jax.ShapeDtypeStruct((head_rows, D), F32), jax.ShapeDtypeStruct((R - head_rows, D), F32)]
    else:
        out_specs = row
        out_shape = jax.ShapeDtypeStruct((R, D), F32)
    stage = pltpu.VMEM((TOP_K * tm * _pitch(nc), LANES), F32)
    return pl.pallas_call(
        functools.partial(_ple_kernel, final=final, head_tiles=ht, n_tok=R),
        grid_spec=pltpu.PrefetchScalarGridSpec(
            num_scalar_prefetch=1,
            grid=(nt,),
            in_specs=[
                row,
                pl.BlockSpec(memory_space=pl.ANY),
                pl.BlockSpec((tm, L), lambda m, sl: (m, 0)),
                pl.BlockSpec((tm, Pd), lambda m, sl: (m, 0)),
                fixed((1, D)), fixed((D, Pd)), fixed((Pd, D)), fixed((Pd, D)), fixed((1, D)),
            ],
            out_specs=out_specs,
            scratch_shapes=[stage, stage, pltpu.VMEM((TOP_K, tm, D), F32), pltpu.SemaphoreType.DMA((2,))],
        ),
        out_shape=out_shape,
        compiler_params=_cparams(("arbitrary",)),
        name="moe_combine_ple",
    )(slot_rows, x, y3, wts, p, g, gd, gu, win, gf)


def _moe_ple(x, p, layer, prm, final, head_rows):
    R, D = x.shape
    assert (R * TOP_K) % MOE_ROWS == 0 and D % LANES == 0
    n_grp = prm["moe_w_grp"].shape[2]
    n_exp = prm["moe_w_exp"].shape[2]
    L = LANES
    w_cat = jnp.concatenate([prm["moe_w_grp"][layer], prm["moe_w_exp"][layer],
                             jnp.zeros((D, L - n_grp - n_exp), F32)], axis=1)
    b_cat = jnp.concatenate([prm["moe_b_grp"][layer], prm["moe_b_exp"][layer],
                             jnp.zeros((L - n_grp - n_exp,), F32)])[None, :]
    hn, wts, route, cnt = _router(x, prm["norm_ffn"][layer][None, :], w_cat, b_cat, n_grp, n_exp // n_grp)
    dest, buf_tok, plan = _dispatch_plan(
        route[:, :TOP_K], route[:, TOP_K:2 * TOP_K], cnt[0, :n_exp].astype(jnp.int32))
    xg = _gather_rows(hn, buf_tok, plan[1])
    hmid = _moe_up(xg, prm["moe_w_gate"], prm["moe_w_up"], layer, plan)
    y3 = _moe_down(hmid, prm["moe_w_down"], layer, plan)
    slot_rows = dest.reshape(R, TOP_K).T.reshape(-1)
    return _ple(x, y3, slot_rows, wts, p, prm["norm_ple"][layer][None, :],
                prm["ple_gate_down"][layer].astype(BF16), prm["ple_gate_up"][layer].astype(BF16),
                prm["ple_w_in"][layer].astype(BF16), prm["norm_final"][None, :], final, head_rows)


def kernel(x_prompt, x_sample, state_rwkv_shift, state_rwkv_wkv, state_pool, p_prompt, p_sample,
           rwkv_mu, rwkv_w_rkv, rwkv_w_o, rwkv_w0, rwkv_w1, rwkv_w2, rwkv_a0, rwkv_a1, rwkv_a2,
           rwkv_g1, rwkv_g2, rwkv_k_k, rwkv_k_a, rwkv_r_k, rwkv_lnx_w, rwkv_lnx_b,
           pool_w, pool_scale, norm_mix, norm_ffn, norm_ple, norm_final,
           moe_w_grp, moe_b_grp, moe_w_exp, moe_b_exp, moe_w_gate, moe_w_up, moe_w_down,
           ple_w_in, ple_gate_down, ple_gate_up):
    Bp, Tp, D = x_prompt.shape
    Bs, Ts, _ = x_sample.shape
    depth = norm_mix.shape[0]
    H, hs = rwkv_r_k.shape[1], rwkv_r_k.shape[2]
    W = QUAD * hs
    assert D % W == 0 and Ts <= CHUNK and Tp % CHUNK == 0
    lay = _SeqLayout(Bp, Tp, Bs, Ts)
    Rp = Bp * Tp
    prm = dict(moe_w_grp=moe_w_grp, moe_b_grp=moe_b_grp, moe_w_exp=moe_w_exp, moe_b_exp=moe_b_exp,
               moe_w_gate=moe_w_gate, moe_w_up=moe_w_up, moe_w_down=moe_w_down, norm_ffn=norm_ffn,
               norm_ple=norm_ple, norm_final=norm_final, ple_w_in=ple_w_in, ple_gate_down=ple_gate_down,
               ple_gate_up=ple_gate_up)

    x = jnp.concatenate([x_prompt.reshape(Rp, D), x_sample.reshape(Bs * Ts, D)], axis=0)
    Pd = p_prompt.shape[-1]
    p_all = jnp.concatenate([p_prompt.reshape(depth, Rp, Pd), p_sample.reshape(depth, Bs * Ts, Pd)],
                            axis=1).astype(BF16)

    def slabs(vec):
        return vec.reshape(D // W, 1, W)

    shift_p, wkv_p, pool_p, shift_s, wkv_s, pool_s = [], [], [], [], [], []
    for i in range(depth):
        j = i // 2
        g_mix = norm_mix[i][None, :]
        if i % 2 == 0:
            shift_all = jnp.concatenate([jnp.zeros((Bp, D), F32), state_rwkv_shift[j]], axis=0)[:, None, :]
            mixed, h_last = _norm_mix(x, shift_all, g_mix, rwkv_mu[j], lay)
            xr, xw, xk, xv, xa, xg = mixed
            w_rkv = rwkv_w_rkv[j].astype(BF16)
            r = _mm(xr, w_rkv, w_group=0, slab_out=W, name="proj_r")
            k = _mm(xk, w_rkv, w_group=1, slab_out=W, name="proj_k")
            v = _mm(xv, w_rkv, w_group=2, slab_out=W, name="proj_v")
            lw = _lora(xw, rwkv_w1[j].astype(BF16), rwkv_w2[j].astype(BF16), rwkv_w0[j][None, :], "decay", W)
            a = _lora(xa, rwkv_a1[j].astype(BF16), rwkv_a2[j].astype(BF16), rwkv_a0[j][None, :], "aaa", W)
            gl = rwkv_g1.shape[2]
            glp = -(-gl // LANES) * LANES
            g1 = jnp.pad(rwkv_g1[j], ((0, 0), (0, glp - gl))).astype(BF16)
            g2 = jnp.pad(rwkv_g2[j], ((0, glp - gl), (0, 0))).astype(BF16)
            g = _lora(xg, g1, g2, jnp.zeros((1, D), F32), "gate", W)
            sprm = (slabs(rwkv_k_k[j]), slabs(rwkv_k_a[j]), slabs(rwkv_r_k[j].reshape(D)),
                    slabs(rwkv_lnx_w[j]), slabs(rwkv_lnx_b[j]))
            o_p, sp = _scan(r, k, v, lw, a, g, sprm, None, row0=0, n_seq=Bp, t_len=Tp, hs=hs)
            o_s, ss = _scan(r, k, v, lw, a, g, sprm, state_rwkv_wkv[j], row0=Rp, n_seq=Bs, t_len=Ts, hs=hs)
            o = jnp.concatenate([o_p, o_s], axis=1)
            x = _mm(o, rwkv_w_o[j].astype(BF16)[None], res=x, slab_in=True, name="proj_o")
            shift_p.append(h_last[:Bp, -1])
            shift_s.append(h_last[Bp:, -1])
            wkv_p.append(sp)
            wkv_s.append(ss)
        else:
            hist = jnp.concatenate([jnp.zeros((Bp, 16, D), F32),
                                    jnp.pad(state_pool[j], ((0, 0), (1, 0), (0, 0)))], axis=0)
            d, h_last = _pool(x, hist, g_mix, lay)
            x = _mm(d, pool_w[j].astype(BF16), scale=pool_scale[j][None, :], res=x, name="pool_proj")
            nh = state_pool.shape[2]
            pool_p.append(h_last[:Bp, SEQ_TILE - nh:])
            pool_s.append(h_last[Bp:, SEQ_TILE - nh:])
        x = _moe_ple(x, p_all[i], i, prm, final=(i == depth - 1), head_rows=Rp)

    y_prompt = x[0].reshape(Bp, Tp, D)
    y_sample = x[1].reshape(Bs, Ts, D)
    return (y_prompt, y_sample, jnp.stack(shift_p), jnp.stack(wkv_p), jnp.stack(pool_p),
            jnp.stack(shift_s), jnp.stack(wkv_s), jnp.stack(pool_s))
```

```python
import functools

import jax
import jax.numpy as jnp
from jax import lax
from jax.experimental import pallas as pl
from jax.experimental.pallas import tpu as pltpu

F32 = jnp.float32
BF16 = jnp.bfloat16

NORM_EPS = 1e-6
GN_EPS = 64e-5
PAST_LEN = 4096
POOL_WINDOWS = (2, 4, 8, 16)
TOP_K = 2
LANES = 128
SEQ_TILE = 32
CHUNK = 64
QUAD = 4
SCAN_GROUP = 8
MOE_ROWS = 256
ISSUE_UNROLL = 8
H_RING = 3
VMEM_LIMIT = 56 * 1024 * 1024


def _cparams(sem):
    return pltpu.CompilerParams(dimension_semantics=sem, vmem_limit_bytes=VMEM_LIMIT)


def _pick(n, cands):
    for c in cands:
        if n % c == 0:
            return c
    return n


def _rms(x, g):
    return x * lax.rsqrt(jnp.mean(x * x, axis=-1, keepdims=True) + NORM_EPS) * g


_NN = (((1,), (0,)), ((), ()))
_NT = (((1,), (1,)), ((), ()))
_TN = (((0,), (0,)), ((), ()))


def _dot(a, b, dims=_NN, prec=None):
    return lax.dot_general(a, b, dims, precision=prec, preferred_element_type=F32)


def _mm_kernel(*refs, has_res, has_scale, a_slabs, o_slabs):
    a_ref, w_ref = refs[0], refs[1]
    k = 2
    if a_slabs:
        a = jnp.concatenate([a_ref[q] for q in range(a_slabs)], axis=1)
    else:
        a = a_ref[...]
    acc = _dot(a, w_ref[...])
    if has_scale:
        acc = acc * refs[k][...]
        k += 1
    if has_res:
        acc = refs[k][...] + acc
        k += 1
    o_ref = refs[k]
    if o_slabs:
        sw = o_ref.shape[2]
        for q in range(o_slabs):
            o_ref[q] = acc[:, q * sw:(q + 1) * sw].astype(o_ref.dtype)
    else:
        o_ref[...] = acc.astype(o_ref.dtype)


def _mm(a, w, *, res=None, scale=None, out_dtype=F32, slab_in=False, slab_out=0, w_group=None, name="mm"):
    M = a.shape[1] if slab_in else a.shape[0]
    G, Kg, Ng = w.shape
    w0 = 0
    if w_group is not None:
        G, w0 = 1, w_group
    tm = _pick(M, (512, 256, 128, 64, 32, 16, 8))
    tn = _pick(Ng, (1024, 512, 256, 128))
    nn = Ng // tn
    if slab_in:
        assert G == 1
        a_spec = pl.BlockSpec((a.shape[0], tm, a.shape[2]), lambda g, n, m: (0, m, 0))
    else:
        a_spec = pl.BlockSpec((tm, Kg), lambda g, n, m: (m, g))
    in_specs = [a_spec, pl.BlockSpec((None, Kg, tn), lambda g, n, m: (w0 + g, 0, n))]
    args = [a, w]
    if scale is not None:
        in_specs.append(pl.BlockSpec((1, tn), lambda g, n, m: (0, g * nn + n)))
        args.append(scale)
    if res is not None:
        in_specs.append(pl.BlockSpec((tm, tn), lambda g, n, m: (m, g * nn + n)))
        args.append(res)
    if slab_out:
        assert tn % slab_out == 0
        per = tn // slab_out
        out_spec = pl.BlockSpec((per, tm, slab_out), lambda g, n, m: (g * nn + n, m, 0))
        out_shape = jax.ShapeDtypeStruct((G * Ng // slab_out, M, slab_out), out_dtype)
    else:
        per = 0
        out_spec = pl.BlockSpec((tm, tn), lambda g, n, m: (m, g * nn + n))
        out_shape = jax.ShapeDtypeStruct((M, G * Ng), out_dtype)
    return pl.pallas_call(
        functools.partial(_mm_kernel, has_res=res is not None, has_scale=scale is not None,
                          a_slabs=a.shape[0] if slab_in else 0, o_slabs=per),
        grid=(G, nn, M // tm),
        in_specs=in_specs,
        out_specs=out_spec,
        out_shape=out_shape,
        compiler_params=_cparams(("arbitrary", "arbitrary", "arbitrary")),
        name=name,
    )(*args)


def _lora_kernel(a_ref, w1_ref, w2_ref, b_ref, o_ref, *, mode):
    t = _dot(a_ref[...], w1_ref[...])
    if mode == "decay":
        t = jnp.tanh(t)
    elif mode == "gate":
        t = jax.nn.sigmoid(t)
    z = _dot(t.astype(BF16), w2_ref[...]) + b_ref[...]
    if mode == "decay":
        z = -jnp.exp(-jax.nn.softplus(-z) - 0.5)
    elif mode == "aaa":
        z = jax.nn.sigmoid(z)
    sw = o_ref.shape[2]
    for q in range(o_ref.shape[0]):
        o_ref[q] = z[:, q * sw:(q + 1) * sw]


def _lora(a, w1, w2, bias, mode, sw):
    M, K = a.shape
    L = w1.shape[1]
    D = w2.shape[1]
    tm = _pick(M, (256, 128, 64, 32, 16, 8))
    return pl.pallas_call(
        functools.partial(_lora_kernel, mode=mode),
        grid=(M // tm,),
        in_specs=[
            pl.BlockSpec((tm, K), lambda m: (m, 0)),
            pl.BlockSpec((K, L), lambda m: (0, 0)),
            pl.BlockSpec((L, D), lambda m: (0, 0)),
            pl.BlockSpec((1, D), lambda m: (0, 0)),
        ],
        out_specs=pl.BlockSpec((D // sw, tm, sw), lambda m: (0, m, 0)),
        out_shape=jax.ShapeDtypeStruct((D // sw, M, sw), F32),
        compiler_params=_cparams(("arbitrary",)),
        name="lora_" + mode,
    )(a, w1, w2, bias)


class _SeqLayout:
    def __init__(self, bp, tp, bs, ts):
        assert tp % SEQ_TILE == 0 and ts % SEQ_TILE == 0
        self.bp, self.tp, self.bs, self.ts = bp, tp, bs, ts
        self.tiles_p = tp // SEQ_TILE
        self.tiles_s = ts // SEQ_TILE
        self.n_tiles_p = bp * self.tiles_p
        self.n_tiles = self.n_tiles_p + bs * self.tiles_s
        self.n_seq = bp + bs
        self.rows = bp * tp + bs * ts

    def seq_of_tile(self, i):
        return jnp.where(i < self.n_tiles_p, i // self.tiles_p,
                         self.bp + (i - self.n_tiles_p) // self.tiles_s)

    def tile_in_seq(self, i):
        return jnp.where(i < self.n_tiles_p, i % self.tiles_p, (i - self.n_tiles_p) % self.tiles_s)


def _norm_mix_kernel(x_ref, xp_ref, sh_ref, g_ref, mu_ref, *outs, lay):
    i = pl.program_id(0)
    g = g_ref[...]
    h = _rms(x_ref[...], g)
    h_before = _rms(xp_ref[7:8, :], g)
    first = lay.tile_in_seq(i) == 0
    prev_row = jnp.where(first, sh_ref[...], h_before)
    rows = lax.broadcasted_iota(jnp.int32, h.shape, 0)
    h_prev = jnp.where(rows == 0, prev_row, pltpu.roll(h, 1, axis=0))
    xx = h_prev - h
    for n in range(6):
        outs[n][...] = (h + xx * mu_ref[n:n + 1, :]).astype(BF16)
    outs[6][...] = h


def _norm_mix(x, shift_all, g, mu, lay):
    R, D = x.shape
    tpb = SEQ_TILE // 8
    row_spec = pl.BlockSpec((SEQ_TILE, D), lambda i: (i, 0))
    outs = pl.pallas_call(
        functools.partial(_norm_mix_kernel, lay=lay),
        grid=(lay.n_tiles,),
        in_specs=[
            row_spec,
            pl.BlockSpec((8, D), lambda i: (jnp.maximum(i * tpb - 1, 0), 0)),
            pl.BlockSpec((None, 1, D), lambda i: (lay.seq_of_tile(i), 0, 0)),
            pl.BlockSpec((1, D), lambda i: (0, 0)),
            pl.BlockSpec((6, D), lambda i: (0, 0)),
        ],
        out_specs=[row_spec] * 6 + [pl.BlockSpec((None, SEQ_TILE, D), lambda i: (lay.seq_of_tile(i), 0, 0))],
        out_shape=[jax.ShapeDtypeStruct((R, D), BF16)] * 6
        + [jax.ShapeDtypeStruct((lay.n_seq, SEQ_TILE, D), F32)],
        compiler_params=_cparams(("arbitrary",)),
        name="norm_mix",
    )(x, x, shift_all, g, mu)
    return outs[:6], outs[6]


def _scan_kernel(r_ref, k_ref, v_ref, lw_ref, a_ref, g_ref, kk_ref, ka_ref, rk_ref, lnw_ref, lnb_ref,
                 *rest, t_blk, hs, group, n_chunks):
    s0_ref = rest[0] if len(rest) == 4 else None
    o_ref, sout_ref, s_ref = rest[-3:]
    C = t_blk
    nq, _, W = r_ref.shape
    c = pl.program_id(1)

    def head_block(q, j):
        return (q, slice(j * hs, (j + 1) * hs), slice(j * hs, (j + 1) * hs))

    @pl.when(c == 0)
    def _():
        s_ref[...] = jnp.zeros(s_ref.shape, F32)
        if s0_ref is not None:
            for q in range(nq):
                for j in range(QUAD):
                    s_ref[head_block(q, j)] = s0_ref[q * QUAD + j]

    ri = lax.broadcasted_iota(jnp.int32, (W, W), 0)
    ci = lax.broadcasted_iota(jnp.int32, (W, W), 1)
    head_blk = (ri // hs) == (ci // hs)
    ones_blk = jnp.where(head_blk, 1.0, 0.0).astype(F32)
    ti = lax.broadcasted_iota(jnp.int32, (C, C), 0)
    si = lax.broadcasted_iota(jnp.int32, (C, C), 1)
    tri_incl = jnp.where(ti >= si, 1.0, 0.0).astype(F32)
    lane_head = lax.broadcasted_iota(jnp.int32, (C, W), 1) // hs
    rj = lax.broadcasted_iota(jnp.int32, (QUAD * C, QUAD * C), 0)
    cj = lax.broadcasted_iota(jnp.int32, (QUAD * C, QUAD * C), 1)
    t_in = rj % C
    s_in = cj % C
    strict = jnp.where(t_in > s_in, 1.0, 0.0).astype(F32)
    incl = jnp.where(t_in >= s_in, 1.0, 0.0).astype(F32)
    eye = jnp.where(rj == cj, 1.0, 0.0).astype(F32)

    ones_bf = ones_blk.astype(BF16)
    head_masks = [jnp.where(lane_head == h, 1.0, 0.0).astype(BF16) for h in range(QUAD)]

    def fold(x):
        y = x[0:C]
        for h in range(1, QUAD):
            y = y + x[h * C:(h + 1) * C]
        return y

    def group_body(it, carry, *, passes, group):
        qs = [it * group + j for j in range(group)]
        each = lambda f, *xs: [f(*t) for t in zip(*xs)]

        def parts(x):
            hi = x.astype(BF16)
            if passes == 1:
                return (hi,)
            return (hi, (x - hi.astype(F32)).astype(BF16))

        def bdot(x, y, dims=_NN):
            out = _dot(x[0], y[0], dims)
            if passes > 1:
                out = out + (_dot(x[0], y[1], dims) + _dot(x[1], y[0], dims))
            return out

        def head_sum(x):
            out = _dot(parts(x)[0], ones_bf)
            if passes > 1:
                out = out + _dot(parts(x)[1], ones_bf)
            return out

        def stack(x):
            return tuple(jnp.concatenate([xb * m for m in head_masks], axis=0) for xb in parts(x))

        def load(ref):
            return [ref[q] for q in qs]

        r, k, v, lw, a = load(r_ref), load(k_ref), load(v_ref), load(lw_ref), load(a_ref)
        kk = each(lambda k_, q: k_ * kk_ref[q], k, qs)
        kk = each(lambda x: x * lax.rsqrt(jnp.maximum(head_sum(x * x), 1e-24)), kk)
        kf = each(lambda k_, a_, q: k_ * (1.0 + (a_ - 1.0) * ka_ref[q]), k, a, qs)
        bv = each(lambda x, a_: x * a_, kk, a)

        cum = each(lambda x: _dot(tri_incl, x, _NN, lax.Precision.HIGHEST), lw)
        e_neg = each(lambda x: jnp.exp(-x), cum)
        rt = each(lambda r_, x: r_ * jnp.exp(x), r, cum)
        at = each(lambda x, c_, l_: -x * jnp.exp(c_ - l_), kk, cum, lw)
        kt = each(lambda x, e: x * e, kf, e_neg)
        bt = each(lambda x, e: x * e, bv, e_neg)
        p_c = each(lambda x: jnp.exp(x[C - 1:C, :]), cum)

        a_st, r_st, b_st, k_st, v_st = (each(stack, x) for x in (at, rt, bt, kt, v))
        a_ab = each(lambda x, y: bdot(x, y, _NT) * strict, a_st, b_st)
        a_ak = each(lambda x, y: parts(bdot(x, y, _NT) * strict), a_st, k_st)
        a_rb = each(lambda x, y: parts(bdot(x, y, _NT) * incl), r_st, b_st)
        a_rk = each(lambda x, y: parts(bdot(x, y, _NT) * incl), r_st, k_st)

        t_inv = each(lambda x: eye + x, a_ab)
        n_pow = each(parts, a_ab)
        p = 1
        while 2 * p < C:
            n_pow = each(lambda x: parts(bdot(x, x)), n_pow)
            t_inv = each(lambda t, n: t + bdot(parts(t), n), t_inv, n_pow)
            p *= 2
        t_bf = each(parts, t_inv)

        s = [s_ref[q] for q in qs]
        s_bf = each(parts, s)
        u_in = each(lambda a_, s_, ak, v_: parts(bdot(a_, s_, _NT) + bdot(ak, v_)), a_st, s_bf, a_ak, v_st)
        u_st = each(bdot, t_bf, u_in)
        o_st = each(lambda r_, s_, rb, u_, rk, v_: bdot(r_, s_, _NT) + bdot(rb, parts(u_)) + bdot(rk, v_),
                    r_st, s_bf, a_rb, u_st, a_rk, v_st)
        o = each(fold, o_st)
        u = each(fold, u_st)
        upd = each(lambda u_, v_, b_, k_, pc: bdot(parts(jnp.concatenate([u_, v_], axis=0)),
                                                   parts(jnp.concatenate([b_ * pc, k_ * pc], axis=0)), _TN),
                   u, v, bt, kt, p_c)
        for q, s_, pc, up in zip(qs, s, p_c, upd):
            s_ref[q] = s_ * pc + jnp.where(head_blk, up, 0.0)

        inv_n = 1.0 / hs
        o_c = each(lambda x: x - head_sum(x) * inv_n, o)
        o_n = each(lambda x: x * lax.rsqrt(head_sum(x * x) * inv_n + GN_EPS), o_c)
        bonus = each(lambda r_, k_, q: head_sum(r_ * k_ * rk_ref[q]), r, kf, qs)
        for q, x, bo, v_ in zip(qs, o_n, bonus, v):
            y = x * lnw_ref[q] + lnb_ref[q] + bo * v_
            o_ref[q] = (y * g_ref[q]).astype(o_ref.dtype)
        return carry

    def run(passes, grp):
        lax.fori_loop(0, nq // grp, functools.partial(group_body, passes=passes, group=grp), 0)

    if n_chunks > 1:
        last = n_chunks - 1

        @pl.when(c < last)
        def _():
            run(1, group)

        @pl.when(c == last)
        def _():
            run(3, _pick(nq, (2, 1)))
    else:
        run(1, group)

    @pl.when(c == n_chunks - 1)
    def _():
        for q in range(nq):
            for j in range(QUAD):
                sout_ref[q * QUAD + j] = s_ref[head_block(q, j)]


def _scan(r, k, v, lw, a, g, prm, s0, *, row0, n_seq, t_len, hs):
    nq, _, W = r.shape
    H = nq * QUAD
    t_blk = min(CHUNK, t_len)
    n_chunks = t_len // t_blk
    blk0 = row0 // t_blk
    row_spec = pl.BlockSpec((nq, t_blk, W), lambda b, c: (0, blk0 + b * n_chunks + c, 0))
    prm_spec = pl.BlockSpec((nq, 1, W), lambda b, c: (0, 0, 0))
    st_spec = pl.BlockSpec((None, H, hs, hs), lambda b, c: (b, 0, 0, 0))
    return pl.pallas_call(
        functools.partial(_scan_kernel, t_blk=t_blk, hs=hs, group=_pick(nq, (SCAN_GROUP, 2, 1)),
                          n_chunks=n_chunks),
        grid=(n_seq, n_chunks),
        in_specs=[row_spec] * 6 + [prm_spec] * 5 + ([] if s0 is None else [st_spec]),
        out_specs=[pl.BlockSpec((nq, t_blk, W), lambda b, c: (0, b * n_chunks + c, 0)), st_spec],
        out_shape=[jax.ShapeDtypeStruct((nq, n_seq * t_len, W), BF16),
                   jax.ShapeDtypeStruct((n_seq, H, hs, hs), F32)],
        scratch_shapes=[pltpu.VMEM((nq, W, W), F32)],
        compiler_params=_cparams(("arbitrary", "arbitrary")),
        name="rwkv_scan",
    )(r, k, v, lw, a, g, *prm, *(() if s0 is None else (s0,)))


def _pool_kernel(x_ref, xp_ref, hist_ref, g_ref, d_ref, hl_ref, *, lay):
    i = pl.program_id(0)
    g = g_ref[...]
    h = _rms(x_ref[...], g)
    tis = lay.tile_in_seq(i)
    prev = jnp.where(tis == 0, hist_ref[...], _rms(xp_ref[...], g))
    ext = jnp.concatenate([prev, h], axis=0)
    D = h.shape[1]
    gw = D // len(POOL_WINDOWS)
    pos0 = jnp.where(i < lay.n_tiles_p, 0, PAST_LEN) + tis * SEQ_TILE
    pos = pos0 + lax.broadcasted_iota(jnp.int32, (SEQ_TILE, 1), 0)
    for gi, w in enumerate(POOL_WINDOWS):
        e = ext[:, gi * gw:(gi + 1) * gw]
        s = e
        step = 1
        while step < w:
            s = s + pltpu.roll(s, step, axis=0)
            step *= 2
        cnt = jnp.minimum(pos + 1, w).astype(F32)
        mean = s[16:] / cnt
        d_ref[:, gi * gw:(gi + 1) * gw] = (mean - h[:, gi * gw:(gi + 1) * gw]).astype(BF16)
    hl_ref[...] = h


def _pool(x, hist16, g, lay):
    R, D = x.shape
    tp16 = SEQ_TILE // 16
    return pl.pallas_call(
        functools.partial(_pool_kernel, lay=lay),
        grid=(lay.n_tiles,),
        in_specs=[
            pl.BlockSpec((SEQ_TILE, D), lambda i: (i, 0)),
            pl.BlockSpec((16, D), lambda i: (jnp.maximum(i * tp16 - 1, 0), 0)),
            pl.BlockSpec((None, 16, D), lambda i: (lay.seq_of_tile(i), 0, 0)),
            pl.BlockSpec((1, D), lambda i: (0, 0)),
        ],
        out_specs=[pl.BlockSpec((SEQ_TILE, D), lambda i: (i, 0)),
                   pl.BlockSpec((None, SEQ_TILE, D), lambda i: (lay.seq_of_tile(i), 0, 0))],
        out_shape=[jax.ShapeDtypeStruct((R, D), BF16),
                   jax.ShapeDtypeStruct((lay.n_seq, SEQ_TILE, D), F32)],
        compiler_params=_cparams(("arbitrary",)),
        name="pool_mix",
    )(x, x, hist16, g)


def _router_kernel(x_ref, g_ref, w_ref, b_ref, hn_ref, wts_ref, eid_ref, cnt_ref, *, n_grp, per_grp):
    hn = _rms(x_ref[...], g_ref[...])
    hb = hn.astype(BF16)
    bits = lax.bitcast_convert_type(hb.astype(F32), jnp.uint32)
    for c in range(hn_ref.shape[1]):
        lo = bits[:, (2 * c) * LANES:(2 * c + 1) * LANES] >> 16
        hi = bits[:, (2 * c + 1) * LANES:(2 * c + 2) * LANES] & jnp.uint32(0xFFFF0000)
        hn_ref[:, c, :] = lo | hi
    logits = _dot(hn, w_ref[...], prec=lax.Precision.HIGHEST) + b_ref[...]
    lane = lax.broadcasted_iota(jnp.int32, logits.shape, 1).astype(F32)
    neg = -jnp.inf
    big = float(logits.shape[1])
    gl = jnp.where(lane < n_grp, logits, neg)
    gmax = jnp.max(gl, axis=-1, keepdims=True)
    gsum = jnp.sum(jnp.exp(gl - gmax), axis=-1, keepdims=True)
    g_idx = jnp.min(jnp.where(gl == gmax, lane, big), axis=-1, keepdims=True)
    g_gate = 1.0 / gsum
    lo = n_grp + g_idx * per_grp
    el = jnp.where((lane >= lo) & (lane < lo + per_grp), logits, neg)
    e1 = jnp.max(el, axis=-1, keepdims=True)
    i1 = jnp.min(jnp.where(el == e1, lane, big), axis=-1, keepdims=True)
    el2 = jnp.where(lane == i1, neg, el)
    e2 = jnp.max(el2, axis=-1, keepdims=True)
    i2 = jnp.min(jnp.where(el2 == e2, lane, big), axis=-1, keepdims=True)
    esum = jnp.sum(jnp.exp(el - e1), axis=-1, keepdims=True)
    p1 = 1.0 / esum
    p2 = jnp.exp(e2 - e1) / esum
    w1 = g_gate * p1 / (p1 + p2)
    w2 = g_gate * p2 / (p1 + p2)
    wts_ref[...] = jnp.where(lane == 0, w1, jnp.where(lane == 1, w2, 0.0))

    @pl.when(pl.program_id(0) == 0)
    def _():
        cnt_ref[...] = jnp.zeros(cnt_ref.shape, F32)

    e1 = i1 - n_grp
    e2 = i2 - n_grp
    oh1 = jnp.where(lane == e1, 1.0, 0.0)
    oh2 = jnp.where(lane == e2, 1.0, 0.0)
    both = oh1 + oh2
    tm = both.shape[0]
    before = (lax.broadcasted_iota(jnp.int32, (tm, tm), 1) < lax.broadcasted_iota(jnp.int32, (tm, tm), 0))
    seen = _dot(jnp.where(before, 1.0, 0.0).astype(BF16), both.astype(BF16)) + cnt_ref[...]
    rank1 = jnp.sum(oh1 * seen, axis=-1, keepdims=True)
    rank2 = jnp.sum(oh2 * seen, axis=-1, keepdims=True)
    cnt_ref[...] += jnp.sum(both, axis=0, keepdims=True)
    eid_ref[...] = jnp.where(lane == 0, e1, jnp.where(lane == 1, e2, jnp.where(
        lane == 2, rank1, jnp.where(lane == 3, rank2, 0.0)))).astype(jnp.int32)


def _router(x, g, w_cat, b_cat, n_grp, per_grp):
    R, D = x.shape
    tm = _pick(R, (256, 128, 64, 32, 16, 8))
    L = w_cat.shape[1]
    return pl.pallas_call(
        functools.partial(_router_kernel, n_grp=n_grp, per_grp=per_grp),
        grid=(R // tm,),
        in_specs=[
            pl.BlockSpec((tm, D), lambda m: (m, 0)),
            pl.BlockSpec((1, D), lambda m: (0, 0)),
            pl.BlockSpec((D, L), lambda m: (0, 0)),
            pl.BlockSpec((1, L), lambda m: (0, 0)),
        ],
        out_specs=[pl.BlockSpec((tm, D // (2 * LANES), LANES), lambda m: (m, 0, 0)),
                   pl.BlockSpec((tm, L), lambda m: (m, 0)),
                   pl.BlockSpec((tm, L), lambda m: (m, 0)),
                   pl.BlockSpec((1, L), lambda m: (0, 0))],
        out_shape=[jax.ShapeDtypeStruct((R, D // (2 * LANES), LANES), jnp.uint32),
                   jax.ShapeDtypeStruct((R, L), F32),
                   jax.ShapeDtypeStruct((R, L), jnp.int32),
                   jax.ShapeDtypeStruct((1, L), F32)],
        compiler_params=_cparams(("arbitrary",)),
        name="moe_router",
    )(x, g, w_cat, b_cat)


def _pitch(nc):
    return nc + 4


def _issue_rows(src_ref, row_of, n_rows, buf, sem, holes=False):
    nc = src_ref.shape[1]

    def issue(j, carry):
        for u in range(ISSUE_UNROLL):
            r = ISSUE_UNROLL * j + u
            row = row_of(r)

            def start(r=r, row=row, u=u):
                pltpu.make_async_copy(src_ref.at[row], buf.at[pl.ds(r * _pitch(nc), nc)], sem).start(priority=u % 2)

            if holes:
                pl.when(row >= 0)(start)
            else:
                start()
        return carry

    assert n_rows % ISSUE_UNROLL == 0
    lax.fori_loop(0, n_rows // ISSUE_UNROLL, issue, 0)


def _wait_rows(n_rows, nc, buf, sem):
    pltpu.make_async_copy(buf.at[pl.ds(0, n_rows * nc)], buf.at[pl.ds(0, n_rows * nc)], sem).wait()


def _gather_kernel(idx_ref, nblk_ref, nvalid_ref, src_ref, o_ref, buf0, buf1, sem):
    G = o_ref.shape[0]
    nc = src_ref.shape[1]
    b = pl.program_id(0)
    nblk = nblk_ref[0]
    bufs = (buf0, buf1)

    def rows_of(blk):
        return lambda r: idx_ref[blk * G + r]

    def fetch(blk, slot):
        _issue_rows(src_ref, rows_of(blk), G, bufs[slot], sem.at[slot], holes=True)

    @pl.when(b == 0)
    def _():
        for buf in bufs:
            buf[...] = jnp.zeros(buf.shape, buf.dtype)

    @pl.when((b == 0) & (nblk > 0))
    def _():
        fetch(0, 0)

    for slot in range(2):
        @pl.when((b < nblk) & (b % 2 == slot))
        def _(slot=slot):
            @pl.when(b + 1 < nblk)
            def _():
                fetch(b + 1, 1 - slot)

            _wait_rows(nvalid_ref[b], nc, bufs[slot], sem.at[slot])
            for cc in range(nc):
                w = bufs[slot][pl.ds(cc, G, stride=_pitch(nc)), :]
                lo = lax.bitcast_convert_type(w << 16, F32)
                hi = lax.bitcast_convert_type(w & jnp.uint32(0xFFFF0000), F32)
                o_ref[:, (2 * cc) * LANES:(2 * cc + 1) * LANES] = lo.astype(BF16)
                o_ref[:, (2 * cc + 1) * LANES:(2 * cc + 2) * LANES] = hi.astype(BF16)

    @pl.when(b >= nblk_ref[0])
    def _():
        o_ref[...] = jnp.zeros(o_ref.shape, o_ref.dtype)


def _gather_rows(src3, idx, nblk):
    n = idx.shape[0]
    _, nc, lanes = src3.shape
    G = MOE_ROWS
    assert n % G == 0 and lanes == LANES and src3.dtype == jnp.uint32
    nvalid = jnp.sum((idx >= 0).reshape(n // G, G), axis=1).astype(jnp.int32)
    return pl.pallas_call(
        _gather_kernel,
        grid_spec=pltpu.PrefetchScalarGridSpec(
            num_scalar_prefetch=3,
            grid=(n // G,),
            in_specs=[pl.BlockSpec(memory_space=pl.ANY)],
            out_specs=pl.BlockSpec((G, 2 * nc * LANES), lambda b, *_: (b, 0)),
            scratch_shapes=[pltpu.VMEM((G * _pitch(nc), LANES), jnp.uint32),
                            pltpu.VMEM((G * _pitch(nc), LANES), jnp.uint32),
                            pltpu.SemaphoreType.DMA((2,))],
        ),
        out_shape=jax.ShapeDtypeStruct((n, 2 * nc * LANES), BF16),
        compiler_params=_cparams(("arbitrary",)),
        name="row_gather",
    )(idx, nblk, nvalid, src3)


def _new_expert(blk_e_ref, b):
    prev = blk_e_ref[jnp.maximum(b - 1, 0)]
    return (b == 0) | (blk_e_ref[b] != prev)


def _next_weights(plan, w_hbm, w_f32, w_bf, sem, layer):
    blk_e_ref, _, run_ref, next_ref, nruns_ref = plan
    s, b = pl.program_id(0), pl.program_id(1)
    e = blk_e_ref[b]
    ts = w_bf[0].shape[1]

    def copies(ee, ss, slot):
        return [pltpu.make_async_copy(src.at[layer, ee, :, pl.ds(ss * ts, ts)], dst.at[slot], sem.at[i, slot])
                for i, (src, dst) in enumerate(zip(w_hbm, w_f32))]

    slot = (run_ref[e] + s * nruns_ref[0]) % 2

    @pl.when((s == 0) & (b == 0))
    def _():
        for cp in copies(e, s, slot):
            cp.start()

    e_next = next_ref[e]
    s_next = s + (e_next <= e).astype(jnp.int32)

    @pl.when(s_next < pl.num_programs(0))
    def _():
        for cp in copies(e_next, s_next, 1 - slot):
            cp.start()

    for cp in copies(e, s, slot):
        cp.wait()
    for dst, src in zip(w_bf, w_f32):
        dst[...] = src[slot].astype(BF16)


def _moe_up_kernel(*refs, layer):
    plan, (x_ref, wg_hbm, wu_hbm, h_ref, wg_f32, wu_f32, wg_bf, wu_bf, sem) = refs[:5], refs[5:]
    blk_e_ref, nblk_ref = plan[0], plan[1]
    b = pl.program_id(1)

    @pl.when(b < nblk_ref[0])
    def _():
        @pl.when(_new_expert(blk_e_ref, b))
        def _():
            _next_weights(plan, (wg_hbm, wu_hbm), (wg_f32, wu_f32), (wg_bf, wu_bf), sem, layer)

        x = x_ref[...]
        gate = _dot(x, wg_bf[...])
        up = _dot(x, wu_bf[...])
        h_ref[...] = (jax.nn.silu(gate) * up).astype(BF16)

    @pl.when(b >= nblk_ref[0])
    def _():
        h_ref[...] = jnp.zeros(h_ref.shape, BF16)


def _moe_up(xg, w_gate, w_up, layer, plan):
    P, D = xg.shape
    De = w_gate.shape[3]
    tj = _pick(De, (512, 256, 128))
    nb = P // MOE_ROWS
    any_spec = pl.BlockSpec(memory_space=pl.ANY)
    return pl.pallas_call(
        functools.partial(_moe_up_kernel, layer=layer),
        grid_spec=pltpu.PrefetchScalarGridSpec(
            num_scalar_prefetch=len(plan),
            grid=(De // tj, nb),
            in_specs=[pl.BlockSpec((MOE_ROWS, D), lambda j, b, *_: (b, 0)), any_spec, any_spec],
            out_specs=pl.BlockSpec((MOE_ROWS, tj), lambda j, b, *_: (b, j)),
            scratch_shapes=[pltpu.VMEM((2, D, tj), F32), pltpu.VMEM((2, D, tj), F32),
                            pltpu.VMEM((D, tj), BF16), pltpu.VMEM((D, tj), BF16),
                            pltpu.SemaphoreType.DMA((2, 2))],
        ),
        out_shape=jax.ShapeDtypeStruct((P, De), BF16),
        compiler_params=_cparams(("arbitrary", "arbitrary")),
        name="moe_up",
    )(*plan, xg, w_gate, w_up)


def _moe_down_kernel(*refs, layer):
    plan, (h_hbm, wd_hbm, y_ref, wd_f32, wd_bf, sem, hbuf, hsem) = refs[:5], refs[5:]
    blk_e_ref, nblk_ref = plan[0], plan[1]
    b = pl.program_id(1)
    nblk = nblk_ref[0]
    rows = hbuf.shape[1]

    def h_copy(blk):
        slot = blk % H_RING
        return pltpu.make_async_copy(h_hbm.at[pl.ds(blk * rows, rows)], hbuf.at[slot], hsem.at[slot])

    @pl.when(b == 0)
    def _():
        for ahead in range(H_RING - 1):
            @pl.when(ahead < nblk)
            def _(ahead=ahead):
                h_copy(ahead).start()

    @pl.when(b < nblk)
    def _():
        @pl.when(b + H_RING - 1 < nblk)
        def _():
            h_copy(b + H_RING - 1).start()

        @pl.when(_new_expert(blk_e_ref, b))
        def _():
            _next_weights(plan, (wd_hbm,), (wd_f32,), (wd_bf,), sem, layer)

        h_copy(b).wait()
        y = _dot(hbuf[b % H_RING], wd_bf[...])
        for cc in range(y_ref.shape[1]):
            y_ref[:, cc, :] = y[:, cc * LANES:(cc + 1) * LANES]

    @pl.when(b >= nblk_ref[0])
    def _():
        y_ref[...] = jnp.zeros(y_ref.shape, F32)


def _moe_down(h, w_down, layer, plan):
    P, De = h.shape
    D = w_down.shape[3]
    tn = _pick(D, (4096, 2048, 1024, 512, 256, 128))
    nb = P // MOE_ROWS
    return pl.pallas_call(
        functools.partial(_moe_down_kernel, layer=layer),
        grid_spec=pltpu.PrefetchScalarGridSpec(
            num_scalar_prefetch=len(plan),
            grid=(D // tn, nb),
            in_specs=[pl.BlockSpec(memory_space=pl.ANY), pl.BlockSpec(memory_space=pl.ANY)],
            out_specs=pl.BlockSpec((MOE_ROWS, tn // LANES, LANES), lambda n, b, *_: (b, n, 0)),
            scratch_shapes=[pltpu.VMEM((2, De, tn), F32), pltpu.VMEM((De, tn), BF16),
                            pltpu.SemaphoreType.DMA((1, 2)),
                            pltpu.VMEM((H_RING, MOE_ROWS, De), BF16), pltpu.SemaphoreType.DMA((H_RING,))],
        ),
        out_shape=jax.ShapeDtypeStruct((P, D // LANES, LANES), F32),
        compiler_params=_cparams(("arbitrary", "arbitrary")),
        name="moe_down",
    )(*plan, h, w_down)


def _dispatch_plan(eid, rank, counts):
    R = eid.shape[0]
    S = R * TOP_K
    n_experts = counts.shape[0]
    e_flat = eid.reshape(-1)
    padded = (counts + MOE_ROWS - 1) // MOE_ROWS * MOE_ROWS
    pad_end = jnp.cumsum(padded)
    dest = ((pad_end - padded)[e_flat] + rank.reshape(-1)).astype(jnp.int32)
    nb = -(-S // MOE_ROWS) + n_experts
    buf_tok = jnp.full((nb * MOE_ROWS,), -1, jnp.int32).at[dest].set(jnp.arange(S, dtype=jnp.int32) // TOP_K)
    blk_e = jnp.minimum(jnp.searchsorted(pad_end, jnp.arange(nb, dtype=jnp.int32) * MOE_ROWS, side='right'),
                        n_experts - 1).astype(jnp.int32)
    nblk = (pad_end[-1:] // MOE_ROWS).astype(jnp.int32)
    ids = jnp.arange(n_experts, dtype=jnp.int32)
    live = counts > 0
    later = jnp.where(live[None, :] & (ids[None, :] > ids[:, None]), ids[None, :], n_experts)
    nxt = jnp.min(later, axis=1)
    next_e = jnp.where(nxt == n_experts, jnp.argmax(live), nxt).astype(jnp.int32)
    run_idx = (jnp.cumsum(live) - 1).astype(jnp.int32)
    n_runs = jnp.sum(live).astype(jnp.int32)[None]
    return dest, buf_tok, (blk_e, nblk, run_idx, next_e, n_runs)


def _ple_kernel(slot_ref, x_ref, y_hbm, wts_ref, p_ref, g_ref, gd_ref, gu_ref, win_ref, gf_ref, *rest,
                final, head_tiles, n_tok):
    n_out = 2 if final else 1
    o_refs = rest[:n_out]
    buf0, buf1, ysel, sem = rest[n_out:]
    bufs = (buf0, buf1)
    m = pl.program_id(0)
    tm = x_ref.shape[0]
    nc = y_hbm.shape[1]

    def fetch(tile, slot):
        _issue_rows(y_hbm, lambda r: slot_ref[(r // tm) * n_tok + tile * tm + r % tm], TOP_K * tm,
                    bufs[slot], sem.at[slot])

    @pl.when(m == 0)
    def _():
        fetch(0, 0)

    for slot in range(2):
        @pl.when(m % 2 == slot)
        def _(slot=slot):
            @pl.when(m + 1 < pl.num_programs(0))
            def _():
                fetch(m + 1, 1 - slot)

            _wait_rows(TOP_K * tm, nc, bufs[slot], sem.at[slot])
            for k in range(TOP_K):
                for cc in range(nc):
                    ysel[k, :, cc * LANES:(cc + 1) * LANES] = (
                        bufs[slot][pl.ds(k * tm * _pitch(nc) + cc, tm, stride=_pitch(nc)), :])

    wts = wts_ref[...]
    x = x_ref[...]
    for k in range(TOP_K):
        x = x + ysel[k] * wts[:, k:k + 1]
    hn = _rms(x, g_ref[...]).astype(BF16)
    t = _dot(hn, gd_ref[...]).astype(BF16)
    gate = jax.nn.sigmoid(_dot(t, gu_ref[...]))
    pe = _dot(p_ref[...], win_ref[...])
    x = x + pe * gate
    if not final:
        o_refs[0][...] = x
        return
    x = _rms(x, gf_ref[...])
    m = pl.program_id(0)

    @pl.when(m < head_tiles)
    def _():
        o_refs[0][...] = x

    @pl.when(m >= head_tiles)
    def _():
        o_refs[1][...] = x


def _ple(x, y3, slot_rows, wts, p, g, gd, gu, win, gf, final, head_rows):
    R, D = x.shape
    Pd = p.shape[1]
    L = wts.shape[1]
    nc = y3.shape[1]
    tm = _pick(R, (128, 64, 32, 16, 8))
    nt = R // tm
    row = pl.BlockSpec((tm, D), lambda m, sl: (m, 0))
    fixed = lambda shape: pl.BlockSpec(shape, lambda m, sl: (0, 0))
    assert head_rows % tm == 0
    ht = head_rows // tm
    if final:
        out_specs = [pl.BlockSpec((tm, D), lambda m, sl: (jnp.minimum(m, ht - 1), 0)),
                     pl.BlockSpec((tm, D), lambda m, sl: (jnp.maximum(m - ht, 0), 0))]
        out_shape = [jax.ShapeDtypeStruct((head_rows, D), F32), jax.ShapeDtypeStruct((R - head_rows, D), F32)]
    else:
        out_specs = row
        out_shape = jax.ShapeDtypeStruct((R, D), F32)
    stage = pltpu.VMEM((TOP_K * tm * _pitch(nc), LANES), F32)
    return pl.pallas_call(
        functools.partial(_ple_kernel, final=final, head_tiles=ht, n_tok=R),
        grid_spec=pltpu.PrefetchScalarGridSpec(
            num_scalar_prefetch=1,
            grid=(nt,),
            in_specs=[
                row,
                pl.BlockSpec(memory_space=pl.ANY),
                pl.BlockSpec((tm, L), lambda m, sl: (m, 0)),
                pl.BlockSpec((tm, Pd), lambda m, sl: (m, 0)),
                fixed((1, D)), fixed((D, Pd)), fixed((Pd, D)), fixed((Pd, D)), fixed((1, D)),
            ],
            out_specs=out_specs,
            scratch_shapes=[stage, stage, pltpu.VMEM((TOP_K, tm, D), F32), pltpu.SemaphoreType.DMA((2,))],
        ),
        out_shape=out_shape,
        compiler_params=_cparams(("arbitrary",)),
        name="moe_combine_ple",
    )(slot_rows, x, y3, wts, p, g, gd, gu, win, gf)


def _moe_ple(x, p, layer, prm, final, head_rows):
    R, D = x.shape
    assert (R * TOP_K) % MOE_ROWS == 0 and D % LANES == 0
    n_grp = prm["moe_w_grp"].shape[2]
    n_exp = prm["moe_w_exp"].shape[2]
    L = LANES
    w_cat = jnp.concatenate([prm["moe_w_grp"][layer], prm["moe_w_exp"][layer],
                             jnp.zeros((D, L - n_grp - n_exp), F32)], axis=1)
    b_cat = jnp.concatenate([prm["moe_b_grp"][layer], prm["moe_b_exp"][layer],
                             jnp.zeros((L - n_grp - n_exp,), F32)])[None, :]
    hn, wts, route, cnt = _router(x, prm["norm_ffn"][layer][None, :], w_cat, b_cat, n_grp, n_exp // n_grp)
    dest, buf_tok, plan = _dispatch_plan(
        route[:, :TOP_K], route[:, TOP_K:2 * TOP_K], cnt[0, :n_exp].astype(jnp.int32))
    xg = _gather_rows(hn, buf_tok, plan[1])
    hmid = _moe_up(xg, prm["moe_w_gate"], prm["moe_w_up"], layer, plan)
    y3 = _moe_down(hmid, prm["moe_w_down"], layer, plan)
    slot_rows = dest.reshape(R, TOP_K).T.reshape(-1)
    return _ple(x, y3, slot_rows, wts, p, prm["norm_ple"][layer][None, :],
                prm["ple_gate_down"][layer].astype(BF16), prm["ple_gate_up"][layer].astype(BF16),
                prm["ple_w_in"][layer].astype(BF16), prm["norm_final"][None, :], final, head_rows)


def kernel(x_prompt, x_sample, state_rwkv_shift, state_rwkv_wkv, state_pool, p_prompt, p_sample,
           rwkv_mu, rwkv_w_rkv, rwkv_w_o, rwkv_w0, rwkv_w1, rwkv_w2, rwkv_a0, rwkv_a1, rwkv_a2,
           rwkv_g1, rwkv_g2, rwkv_k_k, rwkv_k_a, rwkv_r_k, rwkv_lnx_w, rwkv_lnx_b,
           pool_w, pool_scale, norm_mix, norm_ffn, norm_ple, norm_final,
           moe_w_grp, moe_b_grp, moe_w_exp, moe_b_exp, moe_w_gate, moe_w_up, moe_w_down,
           ple_w_in, ple_gate_down, ple_gate_up):
    Bp, Tp, D = x_prompt.shape
    Bs, Ts, _ = x_sample.shape
    depth = norm_mix.shape[0]
    H, hs = rwkv_r_k.shape[1], rwkv_r_k.shape[2]
    W = QUAD * hs
    assert D % W == 0 and Ts <= CHUNK and Tp % CHUNK == 0
    lay = _SeqLayout(Bp, Tp, Bs, Ts)
    Rp = Bp * Tp
    prm = dict(moe_w_grp=moe_w_grp, moe_b_grp=moe_b_grp, moe_w_exp=moe_w_exp, moe_b_exp=moe_b_exp,
               moe_w_gate=moe_w_gate, moe_w_up=moe_w_up, moe_w_down=moe_w_down, norm_ffn=norm_ffn,
               norm_ple=norm_ple, norm_final=norm_final, ple_w_in=ple_w_in, ple_gate_down=ple_gate_down,
               ple_gate_up=ple_gate_up)

    x = jnp.concatenate([x_prompt.reshape(Rp, D), x_sample.reshape(Bs * Ts, D)], axis=0)
    Pd = p_prompt.shape[-1]
    p_all = jnp.concatenate([p_prompt.reshape(depth, Rp, Pd), p_sample.reshape(depth, Bs * Ts, Pd)],
                            axis=1).astype(BF16)

    def slabs(vec):
        return vec.reshape(D // W, 1, W)

    shift_p, wkv_p, pool_p, shift_s, wkv_s, pool_s = [], [], [], [], [], []
    for i in range(depth):
        j = i // 2
        g_mix = norm_mix[i][None, :]
        if i % 2 == 0:
            shift_all = jnp.concatenate([jnp.zeros((Bp, D), F32), state_rwkv_shift[j]], axis=0)[:, None, :]
            mixed, h_last = _norm_mix(x, shift_all, g_mix, rwkv_mu[j], lay)
            xr, xw, xk, xv, xa, xg = mixed
            w_rkv = rwkv_w_rkv[j].astype(BF16)
            r = _mm(xr, w_rkv, w_group=0, slab_out=W, name="proj_r")
            k = _mm(xk, w_rkv, w_group=1, slab_out=W, name="proj_k")
            v = _mm(xv, w_rkv, w_group=2, slab_out=W, name="proj_v")
            lw = _lora(xw, rwkv_w1[j].astype(BF16), rwkv_w2[j].astype(BF16), rwkv_w0[j][None, :], "decay", W)
            a = _lora(xa, rwkv_a1[j].astype(BF16), rwkv_a2[j].astype(BF16), rwkv_a0[j][None, :], "aaa", W)
            gl = rwkv_g1.shape[2]
            glp = -(-gl // LANES) * LANES
            g1 = jnp.pad(rwkv_g1[j], ((0, 0), (0, glp - gl))).astype(BF16)
            g2 = jnp.pad(rwkv_g2[j], ((0, glp - gl), (0, 0))).astype(BF16)
            g = _lora(xg, g1, g2, jnp.zeros((1, D), F32), "gate", W)
            sprm = (slabs(rwkv_k_k[j]), slabs(rwkv_k_a[j]), slabs(rwkv_r_k[j].reshape(D)),
                    slabs(rwkv_lnx_w[j]), slabs(rwkv_lnx_b[j]))
            o_p, sp = _scan(r, k, v, lw, a, g, sprm, None, row0=0, n_seq=Bp, t_len=Tp, hs=hs)
            o_s, ss = _scan(r, k, v, lw, a, g, sprm, state_rwkv_wkv[j], row0=Rp, n_seq=Bs, t_len=Ts, hs=hs)
            o = jnp.concatenate([o_p, o_s], axis=1)
            x = _mm(o, rwkv_w_o[j].astype(BF16)[None], res=x, slab_in=True, name="proj_o")
            shift_p.append(h_last[:Bp, -1])
            shift_s.append(h_last[Bp:, -1])
            wkv_p.append(sp)
            wkv_s.append(ss)
        else:
            hist = jnp.concatenate([jnp.zeros((Bp, 16, D), F32),
                                    jnp.pad(state_pool[j], ((0, 0), (1, 0), (0, 0)))], axis=0)
            d, h_last = _pool(x, hist, g_mix, lay)
            x = _mm(d, pool_w[j].astype(BF16), scale=pool_scale[j][None, :], res=x, name="pool_proj")
            nh = state_pool.shape[2]
            pool_p.append(h_last[:Bp, SEQ_TILE - nh:])
            pool_s.append(h_last[Bp:, SEQ_TILE - nh:])
        x = _moe_ple(x, p_all[i], i, prm, final=(i == depth - 1), head_rows=Rp)

    y_prompt = x[0].reshape(Bp, Tp, D)
    y_sample = x[1].reshape(Bs, Ts, D)
    return (y_prompt, y_sample, jnp.stack(shift_p), jnp.stack(wkv_p), jnp.stack(pool_p),
            jnp.stack(shift_s), jnp.stack(wkv_s), jnp.stack(pool_s))
```

```python
import functools

import jax
import jax.numpy as jnp
from jax import lax
from jax.experimental import pallas as pl
from jax.experimental.pallas import tpu as pltpu

F32 = jnp.float32
BF16 = jnp.bfloat16

NORM_EPS = 1e-6
GN_EPS = 64e-5
PAST_LEN = 4096
POOL_WINDOWS = (2, 4, 8, 16)
TOP_K = 2
LANES = 128
SEQ_TILE = 32
CHUNK = 64
QUAD = 4
SCAN_GROUP = 8
MOE_ROWS = 256
ISSUE_UNROLL = 8
VMEM_LIMIT = 56 * 1024 * 1024


def _cparams(sem, fuse_inputs=None):
    return pltpu.CompilerParams(dimension_semantics=sem, vmem_limit_bytes=VMEM_LIMIT,
                                allow_input_fusion=fuse_inputs)


def _pick(n, cands):
    for c in cands:
        if n % c == 0:
            return c
    return n


def _rms(x, g):
    return x * lax.rsqrt(jnp.mean(x * x, axis=-1, keepdims=True) + NORM_EPS) * g


_NN = (((1,), (0,)), ((), ()))
_NT = (((1,), (1,)), ((), ()))
_TN = (((0,), (0,)), ((), ()))


def _dot(a, b, dims=_NN, prec=None):
    return lax.dot_general(a, b, dims, precision=prec, preferred_element_type=F32)


def _mm_kernel(*refs, has_res, has_scale, a_slabs, o_slabs):
    a_ref, w_ref = refs[0], refs[1]
    k = 2
    if a_slabs:
        a = jnp.concatenate([a_ref[q] for q in range(a_slabs)], axis=1)
    else:
        a = a_ref[...]
    acc = _dot(a, w_ref[...])
    if has_scale:
        acc = acc * refs[k][...]
        k += 1
    if has_res:
        acc = refs[k][...] + acc
        k += 1
    o_ref = refs[k]
    if o_slabs:
        sw = o_ref.shape[2]
        for q in range(o_slabs):
            o_ref[q] = acc[:, q * sw:(q + 1) * sw].astype(o_ref.dtype)
    else:
        o_ref[...] = acc.astype(o_ref.dtype)


def _mm(a, w, *, res=None, scale=None, out_dtype=F32, slab_in=False, slab_out=0, w_group=None, name="mm"):
    M = a.shape[1] if slab_in else a.shape[0]
    G, Kg, Ng = w.shape
    w0 = 0
    if w_group is not None:
        G, w0 = 1, w_group
    tm = _pick(M, (512, 256, 128, 64, 32, 16, 8))
    tn = _pick(Ng, (1024, 512, 256, 128))
    nn = Ng // tn
    if slab_in:
        assert G == 1
        a_spec = pl.BlockSpec((a.shape[0], tm, a.shape[2]), lambda g, n, m: (0, m, 0))
    else:
        a_spec = pl.BlockSpec((tm, Kg), lambda g, n, m: (m, g))
    in_specs = [a_spec, pl.BlockSpec((None, Kg, tn), lambda g, n, m: (w0 + g, 0, n))]
    args = [a, w]
    if scale is not None:
        in_specs.append(pl.BlockSpec((1, tn), lambda g, n, m: (0, g * nn + n)))
        args.append(scale)
    if res is not None:
        in_specs.append(pl.BlockSpec((tm, tn), lambda g, n, m: (m, g * nn + n)))
        args.append(res)
    if slab_out:
        assert tn % slab_out == 0
        per = tn // slab_out
        out_spec = pl.BlockSpec((per, tm, slab_out), lambda g, n, m: (g * nn + n, m, 0))
        out_shape = jax.ShapeDtypeStruct((G * Ng // slab_out, M, slab_out), out_dtype)
    else:
        per = 0
        out_spec = pl.BlockSpec((tm, tn), lambda g, n, m: (m, g * nn + n))
        out_shape = jax.ShapeDtypeStruct((M, G * Ng), out_dtype)
    return pl.pallas_call(
        functools.partial(_mm_kernel, has_res=res is not None, has_scale=scale is not None,
                          a_slabs=a.shape[0] if slab_in else 0, o_slabs=per),
        grid=(G, nn, M // tm),
        in_specs=in_specs,
        out_specs=out_spec,
        out_shape=out_shape,
        compiler_params=_cparams(("arbitrary", "arbitrary", "arbitrary"),
                                 fuse_inputs=[i == 1 for i in range(len(args))]),
        name=name,
    )(*args)


def _lora_kernel(a_ref, w1_ref, w2_ref, b_ref, o_ref, *, mode):
    t = _dot(a_ref[...], w1_ref[...])
    if mode == "decay":
        t = jnp.tanh(t)
    elif mode == "gate":
        t = jax.nn.sigmoid(t)
    z = _dot(t.astype(BF16), w2_ref[...]) + b_ref[...]
    if mode == "decay":
        z = -jnp.exp(-jax.nn.softplus(-z) - 0.5)
    elif mode == "aaa":
        z = jax.nn.sigmoid(z)
    sw = o_ref.shape[2]
    for q in range(o_ref.shape[0]):
        o_ref[q] = z[:, q * sw:(q + 1) * sw]


def _lora(a, w1, w2, bias, mode, sw):
    M, K = a.shape
    L = w1.shape[1]
    D = w2.shape[1]
    tm = _pick(M, (256, 128, 64, 32, 16, 8))
    return pl.pallas_call(
        functools.partial(_lora_kernel, mode=mode),
        grid=(M // tm,),
        in_specs=[
            pl.BlockSpec((tm, K), lambda m: (m, 0)),
            pl.BlockSpec((K, L), lambda m: (0, 0)),
            pl.BlockSpec((L, D), lambda m: (0, 0)),
            pl.BlockSpec((1, D), lambda m: (0, 0)),
        ],
        out_specs=pl.BlockSpec((D // sw, tm, sw), lambda m: (0, m, 0)),
        out_shape=jax.ShapeDtypeStruct((D // sw, M, sw), F32),
        compiler_params=_cparams(("arbitrary",)),
        name="lora_" + mode,
    )(a, w1, w2, bias)


class _SeqLayout:
    def __init__(self, bp, tp, bs, ts):
        assert tp % SEQ_TILE == 0 and ts % SEQ_TILE == 0
        self.bp, self.tp, self.bs, self.ts = bp, tp, bs, ts
        self.tiles_p = tp // SEQ_TILE
        self.tiles_s = ts // SEQ_TILE
        self.n_tiles_p = bp * self.tiles_p
        self.n_tiles = self.n_tiles_p + bs * self.tiles_s
        self.n_seq = bp + bs
        self.rows = bp * tp + bs * ts

    def seq_of_tile(self, i):
        return jnp.where(i < self.n_tiles_p, i // self.tiles_p,
                         self.bp + (i - self.n_tiles_p) // self.tiles_s)

    def tile_in_seq(self, i):
        return jnp.where(i < self.n_tiles_p, i % self.tiles_p, (i - self.n_tiles_p) % self.tiles_s)


def _norm_mix_kernel(x_ref, xp_ref, sh_ref, g_ref, mu_ref, *outs, lay):
    i = pl.program_id(0)
    g = g_ref[...]
    h = _rms(x_ref[...], g)
    h_before = _rms(xp_ref[7:8, :], g)
    first = lay.tile_in_seq(i) == 0
    prev_row = jnp.where(first, sh_ref[...], h_before)
    rows = lax.broadcasted_iota(jnp.int32, h.shape, 0)
    h_prev = jnp.where(rows == 0, prev_row, pltpu.roll(h, 1, axis=0))
    xx = h_prev - h
    for n in range(6):
        outs[n][...] = (h + xx * mu_ref[n:n + 1, :]).astype(BF16)
    outs[6][...] = h


def _norm_mix(x, shift_all, g, mu, lay):
    R, D = x.shape
    tpb = SEQ_TILE // 8
    row_spec = pl.BlockSpec((SEQ_TILE, D), lambda i: (i, 0))
    outs = pl.pallas_call(
        functools.partial(_norm_mix_kernel, lay=lay),
        grid=(lay.n_tiles,),
        in_specs=[
            row_spec,
            pl.BlockSpec((8, D), lambda i: (jnp.maximum(i * tpb - 1, 0), 0)),
            pl.BlockSpec((None, 1, D), lambda i: (lay.seq_of_tile(i), 0, 0)),
            pl.BlockSpec((1, D), lambda i: (0, 0)),
            pl.BlockSpec((6, D), lambda i: (0, 0)),
        ],
        out_specs=[row_spec] * 6 + [pl.BlockSpec((None, SEQ_TILE, D), lambda i: (lay.seq_of_tile(i), 0, 0))],
        out_shape=[jax.ShapeDtypeStruct((R, D), BF16)] * 6
        + [jax.ShapeDtypeStruct((lay.n_seq, SEQ_TILE, D), F32)],
        compiler_params=_cparams(("arbitrary",)),
        name="norm_mix",
    )(x, x, shift_all, g, mu)
    return outs[:6], outs[6]


def _scan_kernel(r_ref, k_ref, v_ref, lw_ref, a_ref, g_ref, kk_ref, ka_ref, rk_ref, lnw_ref, lnb_ref,
                 *rest, t_blk, hs, group, n_chunks):
    s0_ref = rest[0] if len(rest) == 4 else None
    o_ref, sout_ref, s_ref = rest[-3:]
    C = t_blk
    nq, _, W = r_ref.shape
    c = pl.program_id(1)

    def head_block(q, j):
        return (q, slice(j * hs, (j + 1) * hs), slice(j * hs, (j + 1) * hs))

    @pl.when(c == 0)
    def _():
        s_ref[...] = jnp.zeros(s_ref.shape, F32)
        if s0_ref is not None:
            for q in range(nq):
                for j in range(QUAD):
                    s_ref[head_block(q, j)] = s0_ref[q * QUAD + j]

    ri = lax.broadcasted_iota(jnp.int32, (W, W), 0)
    ci = lax.broadcasted_iota(jnp.int32, (W, W), 1)
    head_blk = (ri // hs) == (ci // hs)
    ones_blk = jnp.where(head_blk, 1.0, 0.0).astype(F32)
    ti = lax.broadcasted_iota(jnp.int32, (C, C), 0)
    si = lax.broadcasted_iota(jnp.int32, (C, C), 1)
    tri_incl = jnp.where(ti >= si, 1.0, 0.0).astype(F32)
    lane_head = lax.broadcasted_iota(jnp.int32, (C, W), 1) // hs
    rj = lax.broadcasted_iota(jnp.int32, (QUAD * C, QUAD * C), 0)
    cj = lax.broadcasted_iota(jnp.int32, (QUAD * C, QUAD * C), 1)
    t_in = rj % C
    s_in = cj % C
    strict = jnp.where(t_in > s_in, 1.0, 0.0).astype(F32)
    incl = jnp.where(t_in >= s_in, 1.0, 0.0).astype(F32)
    eye = jnp.where(rj == cj, 1.0, 0.0).astype(F32)

    ones_bf = ones_blk.astype(BF16)
    head_masks = [jnp.where(lane_head == h, 1.0, 0.0).astype(BF16) for h in range(QUAD)]

    def fold(x):
        y = x[0:C]
        for h in range(1, QUAD):
            y = y + x[h * C:(h + 1) * C]
        return y

    def group_body(it, carry, *, passes, group):
        qs = [it * group + j for j in range(group)]
        each = lambda f, *xs: [f(*t) for t in zip(*xs)]

        def parts(x):
            hi = x.astype(BF16)
            if passes == 1:
                return (hi,)
            return (hi, (x - hi.astype(F32)).astype(BF16))

        def bdot(x, y, dims=_NN):
            out = _dot(x[0], y[0], dims)
            if passes > 1:
                out = out + (_dot(x[0], y[1], dims) + _dot(x[1], y[0], dims))
            return out

        def head_sum(x):
            out = _dot(parts(x)[0], ones_bf)
            if passes > 1:
                out = out + _dot(parts(x)[1], ones_bf)
            return out

        def stack(x):
            return tuple(jnp.concatenate([xb * m for m in head_masks], axis=0) for xb in parts(x))

        def load(ref):
            return [ref[q] for q in qs]

        r, k, v, lw, a = load(r_ref), load(k_ref), load(v_ref), load(lw_ref), load(a_ref)
        kk = each(lambda k_, q: k_ * kk_ref[q], k, qs)
        kk = each(lambda x: x * lax.rsqrt(jnp.maximum(head_sum(x * x), 1e-24)), kk)
        kf = each(lambda k_, a_, q: k_ * (1.0 + (a_ - 1.0) * ka_ref[q]), k, a, qs)
        bv = each(lambda x, a_: x * a_, kk, a)

        cum = each(lambda x: _dot(tri_incl, x, _NN, lax.Precision.HIGHEST), lw)
        e_neg = each(lambda x: jnp.exp(-x), cum)
        rt = each(lambda r_, x: r_ * jnp.exp(x), r, cum)
        at = each(lambda x, c_, l_: -x * jnp.exp(c_ - l_), kk, cum, lw)
        kt = each(lambda x, e: x * e, kf, e_neg)
        bt = each(lambda x, e: x * e, bv, e_neg)
        p_c = each(lambda x: jnp.exp(x[C - 1:C, :]), cum)

        a_st, r_st, b_st, k_st, v_st = (each(stack, x) for x in (at, rt, bt, kt, v))
        a_ab = each(lambda x, y: bdot(x, y, _NT) * strict, a_st, b_st)
        a_ak = each(lambda x, y: parts(bdot(x, y, _NT) * strict), a_st, k_st)
        a_rb = each(lambda x, y: parts(bdot(x, y, _NT) * incl), r_st, b_st)
        a_rk = each(lambda x, y: parts(bdot(x, y, _NT) * incl), r_st, k_st)

        t_inv = each(lambda x: eye + x, a_ab)
        n_pow = each(parts, a_ab)
        p = 1
        while 2 * p < C:
            n_pow = each(lambda x: parts(bdot(x, x)), n_pow)
            t_inv = each(lambda t, n: t + bdot(parts(t), n), t_inv, n_pow)
            p *= 2
        t_bf = each(parts, t_inv)

        s = [s_ref[q] for q in qs]
        s_bf = each(parts, s)
        u_in = each(lambda a_, s_, ak, v_: parts(bdot(a_, s_, _NT) + bdot(ak, v_)), a_st, s_bf, a_ak, v_st)
        u_st = each(bdot, t_bf, u_in)
        o_st = each(lambda r_, s_, rb, u_, rk, v_: bdot(r_, s_, _NT) + bdot(rb, parts(u_)) + bdot(rk, v_),
                    r_st, s_bf, a_rb, u_st, a_rk, v_st)
        o = each(fold, o_st)
        u = each(fold, u_st)
        upd = each(lambda u_, v_, b_, k_, pc: bdot(parts(jnp.concatenate([u_, v_], axis=0)),
                                                   parts(jnp.concatenate([b_ * pc, k_ * pc], axis=0)), _TN),
                   u, v, bt, kt, p_c)
        for q, s_, pc, up in zip(qs, s, p_c, upd):
            s_ref[q] = s_ * pc + jnp.where(head_blk, up, 0.0)

        inv_n = 1.0 / hs
        o_c = each(lambda x: x - head_sum(x) * inv_n, o)
        o_n = each(lambda x: x * lax.rsqrt(head_sum(x * x) * inv_n + GN_EPS), o_c)
        bonus = each(lambda r_, k_, q: head_sum(r_ * k_ * rk_ref[q]), r, kf, qs)
        for q, x, bo, v_ in zip(qs, o_n, bonus, v):
            y = x * lnw_ref[q] + lnb_ref[q] + bo * v_
            o_ref[q] = (y * g_ref[q]).astype(o_ref.dtype)
        return carry

    def run(passes, grp):
        lax.fori_loop(0, nq // grp, functools.partial(group_body, passes=passes, group=grp), 0)

    if n_chunks > 1:
        last = n_chunks - 1

        @pl.when(c < last)
        def _():
            run(1, group)

        @pl.when(c == last)
        def _():
            run(3, _pick(nq, (2, 1)))
    else:
        run(1, group)

    @pl.when(c == n_chunks - 1)
    def _():
        for q in range(nq):
            for j in range(QUAD):
                sout_ref[q * QUAD + j] = s_ref[head_block(q, j)]


def _scan(r, k, v, lw, a, g, prm, s0, *, row0, n_seq, t_len, hs):
    nq, _, W = r.shape
    H = nq * QUAD
    t_blk = min(CHUNK, t_len)
    n_chunks = t_len // t_blk
    blk0 = row0 // t_blk
    row_spec = pl.BlockSpec((nq, t_blk, W), lambda b, c: (0, blk0 + b * n_chunks + c, 0))
    prm_spec = pl.BlockSpec((nq, 1, W), lambda b, c: (0, 0, 0))
    st_spec = pl.BlockSpec((None, H, hs, hs), lambda b, c: (b, 0, 0, 0))
    return pl.pallas_call(
        functools.partial(_scan_kernel, t_blk=t_blk, hs=hs, group=_pick(nq, (SCAN_GROUP, 2, 1)),
                          n_chunks=n_chunks),
        grid=(n_seq, n_chunks),
        in_specs=[row_spec] * 6 + [prm_spec] * 5 + ([] if s0 is None else [st_spec]),
        out_specs=[pl.BlockSpec((nq, t_blk, W), lambda b, c: (0, b * n_chunks + c, 0)), st_spec],
        out_shape=[jax.ShapeDtypeStruct((nq, n_seq * t_len, W), BF16),
                   jax.ShapeDtypeStruct((n_seq, H, hs, hs), F32)],
        scratch_shapes=[pltpu.VMEM((nq, W, W), F32)],
        compiler_params=_cparams(("arbitrary", "arbitrary")),
        name="rwkv_scan",
    )(r, k, v, lw, a, g, *prm, *(() if s0 is None else (s0,)))


def _pool_kernel(x_ref, xp_ref, hist_ref, g_ref, d_ref, hl_ref, *, lay):
    i = pl.program_id(0)
    g = g_ref[...]
    h = _rms(x_ref[...], g)
    tis = lay.tile_in_seq(i)
    prev = jnp.where(tis == 0, hist_ref[...], _rms(xp_ref[...], g))
    ext = jnp.concatenate([prev, h], axis=0)
    D = h.shape[1]
    gw = D // len(POOL_WINDOWS)
    pos0 = jnp.where(i < lay.n_tiles_p, 0, PAST_LEN) + tis * SEQ_TILE
    pos = pos0 + lax.broadcasted_iota(jnp.int32, (SEQ_TILE, 1), 0)
    for gi, w in enumerate(POOL_WINDOWS):
        e = ext[:, gi * gw:(gi + 1) * gw]
        s = e
        step = 1
        while step < w:
            s = s + pltpu.roll(s, step, axis=0)
            step *= 2
        cnt = jnp.minimum(pos + 1, w).astype(F32)
        mean = s[16:] / cnt
        d_ref[:, gi * gw:(gi + 1) * gw] = (mean - h[:, gi * gw:(gi + 1) * gw]).astype(BF16)
    hl_ref[...] = h


def _pool(x, hist16, g, lay):
    R, D = x.shape
    tp16 = SEQ_TILE // 16
    return pl.pallas_call(
        functools.partial(_pool_kernel, lay=lay),
        grid=(lay.n_tiles,),
        in_specs=[
            pl.BlockSpec((SEQ_TILE, D), lambda i: (i, 0)),
            pl.BlockSpec((16, D), lambda i: (jnp.maximum(i * tp16 - 1, 0), 0)),
            pl.BlockSpec((None, 16, D), lambda i: (lay.seq_of_tile(i), 0, 0)),
            pl.BlockSpec((1, D), lambda i: (0, 0)),
        ],
        out_specs=[pl.BlockSpec((SEQ_TILE, D), lambda i: (i, 0)),
                   pl.BlockSpec((None, SEQ_TILE, D), lambda i: (lay.seq_of_tile(i), 0, 0))],
        out_shape=[jax.ShapeDtypeStruct((R, D), BF16),
                   jax.ShapeDtypeStruct((lay.n_seq, SEQ_TILE, D), F32)],
        compiler_params=_cparams(("arbitrary",)),
        name="pool_mix",
    )(x, x, hist16, g)


def _router_kernel(x_ref, g_ref, w_ref, b_ref, hn_ref, wts_ref, eid_ref, cnt_ref, *, n_grp, per_grp):
    hn = _rms(x_ref[...], g_ref[...])
    hb = hn.astype(BF16)
    bits = lax.bitcast_convert_type(hb.astype(F32), jnp.uint32)
    for c in range(hn_ref.shape[1]):
        lo = bits[:, (2 * c) * LANES:(2 * c + 1) * LANES] >> 16
        hi = bits[:, (2 * c + 1) * LANES:(2 * c + 2) * LANES] & jnp.uint32(0xFFFF0000)
        hn_ref[:, c, :] = lo | hi
    logits = _dot(hn, w_ref[...], prec=lax.Precision.HIGHEST) + b_ref[...]
    lane = lax.broadcasted_iota(jnp.int32, logits.shape, 1).astype(F32)
    neg = -jnp.inf
    big = float(logits.shape[1])
    gl = jnp.where(lane < n_grp, logits, neg)
    gmax = jnp.max(gl, axis=-1, keepdims=True)
    gsum = jnp.sum(jnp.exp(gl - gmax), axis=-1, keepdims=True)
    g_idx = jnp.min(jnp.where(gl == gmax, lane, big), axis=-1, keepdims=True)
    g_gate = 1.0 / gsum
    lo = n_grp + g_idx * per_grp
    el = jnp.where((lane >= lo) & (lane < lo + per_grp), logits, neg)
    e1 = jnp.max(el, axis=-1, keepdims=True)
    i1 = jnp.min(jnp.where(el == e1, lane, big), axis=-1, keepdims=True)
    el2 = jnp.where(lane == i1, neg, el)
    e2 = jnp.max(el2, axis=-1, keepdims=True)
    i2 = jnp.min(jnp.where(el2 == e2, lane, big), axis=-1, keepdims=True)
    esum = jnp.sum(jnp.exp(el - e1), axis=-1, keepdims=True)
    p1 = 1.0 / esum
    p2 = jnp.exp(e2 - e1) / esum
    w1 = g_gate * p1 / (p1 + p2)
    w2 = g_gate * p2 / (p1 + p2)
    wts_ref[...] = jnp.where(lane == 0, w1, jnp.where(lane == 1, w2, 0.0))

    @pl.when(pl.program_id(0) == 0)
    def _():
        cnt_ref[...] = jnp.zeros(cnt_ref.shape, F32)

    e1 = i1 - n_grp
    e2 = i2 - n_grp
    oh1 = jnp.where(lane == e1, 1.0, 0.0)
    oh2 = jnp.where(lane == e2, 1.0, 0.0)
    both = oh1 + oh2
    tm = both.shape[0]
    before = (lax.broadcasted_iota(jnp.int32, (tm, tm), 1) < lax.broadcasted_iota(jnp.int32, (tm, tm), 0))
    seen = _dot(jnp.where(before, 1.0, 0.0).astype(BF16), both.astype(BF16)) + cnt_ref[...]
    rank1 = jnp.sum(oh1 * seen, axis=-1, keepdims=True)
    rank2 = jnp.sum(oh2 * seen, axis=-1, keepdims=True)
    cnt_ref[...] += jnp.sum(both, axis=0, keepdims=True)
    eid_ref[...] = jnp.where(lane == 0, e1, jnp.where(lane == 1, e2, jnp.where(
        lane == 2, rank1, jnp.where(lane == 3, rank2, 0.0)))).astype(jnp.int32)


def _router(x, g, w_cat, b_cat, n_grp, per_grp):
    R, D = x.shape
    tm = _pick(R, (256, 128, 64, 32, 16, 8))
    L = w_cat.shape[1]
    return pl.pallas_call(
        functools.partial(_router_kernel, n_grp=n_grp, per_grp=per_grp),
        grid=(R // tm,),
        in_specs=[
            pl.BlockSpec((tm, D), lambda m: (m, 0)),
            pl.BlockSpec((1, D), lambda m: (0, 0)),
            pl.BlockSpec((D, L), lambda m: (0, 0)),
            pl.BlockSpec((1, L), lambda m: (0, 0)),
        ],
        out_specs=[pl.BlockSpec((tm, D // (2 * LANES), LANES), lambda m: (m, 0, 0)),
                   pl.BlockSpec((tm, L), lambda m: (m, 0)),
                   pl.BlockSpec((tm, L), lambda m: (m, 0)),
                   pl.BlockSpec((1, L), lambda m: (0, 0))],
        out_shape=[jax.ShapeDtypeStruct((R, D // (2 * LANES), LANES), jnp.uint32),
                   jax.ShapeDtypeStruct((R, L), F32),
                   jax.ShapeDtypeStruct((R, L), jnp.int32),
                   jax.ShapeDtypeStruct((1, L), F32)],
        compiler_params=_cparams(("arbitrary",)),
        name="moe_router",
    )(x, g, w_cat, b_cat)


def _pitch(nc):
    return nc + 4


def _issue_rows(src_ref, row_of, n_rows, buf, sem, holes=False):
    nc = src_ref.shape[1]

    def issue(j, carry):
        for u in range(ISSUE_UNROLL):
            r = ISSUE_UNROLL * j + u
            row = row_of(r)

            def start(r=r, row=row, u=u):
                pltpu.make_async_copy(src_ref.at[row], buf.at[pl.ds(r * _pitch(nc), nc)], sem).start(priority=u % 2)

            if holes:
                pl.when(row >= 0)(start)
            else:
                start()
        return carry

    assert n_rows % ISSUE_UNROLL == 0
    lax.fori_loop(0, n_rows // ISSUE_UNROLL, issue, 0)


def _wait_rows(n_rows, nc, buf, sem):
    pltpu.make_async_copy(buf.at[pl.ds(0, n_rows * nc)], buf.at[pl.ds(0, n_rows * nc)], sem).wait()


def _gather_kernel(idx_ref, nblk_ref, nvalid_ref, src_ref, o_ref, buf0, buf1, sem):
    G = o_ref.shape[0]
    nc = src_ref.shape[1]
    b = pl.program_id(0)
    nblk = nblk_ref[0]
    bufs = (buf0, buf1)

    def rows_of(blk):
        return lambda r: idx_ref[blk * G + r]

    def fetch(blk, slot):
        _issue_rows(src_ref, rows_of(blk), G, bufs[slot], sem.at[slot], holes=True)

    @pl.when(b == 0)
    def _():
        for buf in bufs:
            buf[...] = jnp.zeros(buf.shape, buf.dtype)

    @pl.when((b == 0) & (nblk > 0))
    def _():
        fetch(0, 0)

    for slot in range(2):
        @pl.when((b < nblk) & (b % 2 == slot))
        def _(slot=slot):
            @pl.when(b + 1 < nblk)
            def _():
                fetch(b + 1, 1 - slot)

            _wait_rows(nvalid_ref[b], nc, bufs[slot], sem.at[slot])
            for cc in range(nc):
                w = bufs[slot][pl.ds(cc, G, stride=_pitch(nc)), :]
                lo = lax.bitcast_convert_type(w << 16, F32)
                hi = lax.bitcast_convert_type(w & jnp.uint32(0xFFFF0000), F32)
                o_ref[:, (2 * cc) * LANES:(2 * cc + 1) * LANES] = lo.astype(BF16)
                o_ref[:, (2 * cc + 1) * LANES:(2 * cc + 2) * LANES] = hi.astype(BF16)

    @pl.when(b >= nblk_ref[0])
    def _():
        o_ref[...] = jnp.zeros(o_ref.shape, o_ref.dtype)


def _gather_rows(src3, idx, nblk):
    n = idx.shape[0]
    _, nc, lanes = src3.shape
    G = MOE_ROWS
    assert n % G == 0 and lanes == LANES and src3.dtype == jnp.uint32
    nvalid = jnp.sum((idx >= 0).reshape(n // G, G), axis=1).astype(jnp.int32)
    return pl.pallas_call(
        _gather_kernel,
        grid_spec=pltpu.PrefetchScalarGridSpec(
            num_scalar_prefetch=3,
            grid=(n // G,),
            in_specs=[pl.BlockSpec(memory_space=pl.ANY)],
            out_specs=pl.BlockSpec((G, 2 * nc * LANES), lambda b, *_: (b, 0)),
            scratch_shapes=[pltpu.VMEM((G * _pitch(nc), LANES), jnp.uint32),
                            pltpu.VMEM((G * _pitch(nc), LANES), jnp.uint32),
                            pltpu.SemaphoreType.DMA((2,))],
        ),
        out_shape=jax.ShapeDtypeStruct((n, 2 * nc * LANES), BF16),
        compiler_params=_cparams(("arbitrary",)),
        name="row_gather",
    )(idx, nblk, nvalid, src3)


def _new_expert(blk_e_ref, b):
    prev = blk_e_ref[jnp.maximum(b - 1, 0)]
    return (b == 0) | (blk_e_ref[b] != prev)


def _next_weights(plan, w_hbm, w_f32, w_bf, sem, layer):
    blk_e_ref, _, run_ref, next_ref, nruns_ref = plan
    s, b = pl.program_id(0), pl.program_id(1)
    e = blk_e_ref[b]
    ts = w_bf[0].shape[1]

    def copies(ee, ss, slot):
        return [pltpu.make_async_copy(src.at[layer, ee, :, pl.ds(ss * ts, ts)], dst.at[slot], sem.at[i, slot])
                for i, (src, dst) in enumerate(zip(w_hbm, w_f32))]

    slot = (run_ref[e] + s * nruns_ref[0]) % 2

    @pl.when((s == 0) & (b == 0))
    def _():
        for cp in copies(e, s, slot):
            cp.start()

    e_next = next_ref[e]
    s_next = s + (e_next <= e).astype(jnp.int32)

    @pl.when(s_next < pl.num_programs(0))
    def _():
        for cp in copies(e_next, s_next, 1 - slot):
            cp.start()

    for cp in copies(e, s, slot):
        cp.wait()
    for dst, src in zip(w_bf, w_f32):
        dst[...] = src[slot].astype(BF16)


def _moe_up_kernel(*refs, layer):
    plan, (x_ref, wg_hbm, wu_hbm, h_ref, wg_f32, wu_f32, wg_bf, wu_bf, sem) = refs[:5], refs[5:]
    blk_e_ref, nblk_ref = plan[0], plan[1]
    b = pl.program_id(1)

    @pl.when(b < nblk_ref[0])
    def _():
        @pl.when(_new_expert(blk_e_ref, b))
        def _():
            _next_weights(plan, (wg_hbm, wu_hbm), (wg_f32, wu_f32), (wg_bf, wu_bf), sem, layer)

        x = x_ref[...]
        gate = _dot(x, wg_bf[...])
        up = _dot(x, wu_bf[...])
        h_ref[...] = (jax.nn.silu(gate) * up).astype(BF16)

    @pl.when(b >= nblk_ref[0])
    def _():
        h_ref[...] = jnp.zeros(h_ref.shape, BF16)


def _moe_up(xg, w_gate, w_up, layer, plan):
    P, D = xg.shape
    De = w_gate.shape[3]
    tj = _pick(De, (512, 256, 128))
    nb = P // MOE_ROWS
    any_spec = pl.BlockSpec(memory_space=pl.ANY)
    return pl.pallas_call(
        functools.partial(_moe_up_kernel, layer=layer),
        grid_spec=pltpu.PrefetchScalarGridSpec(
            num_scalar_prefetch=len(plan),
            grid=(De // tj, nb),
            in_specs=[pl.BlockSpec((MOE_ROWS, D), lambda j, b, *_: (b, 0)), any_spec, any_spec],
            out_specs=pl.BlockSpec((MOE_ROWS, tj), lambda j, b, *_: (b, j)),
            scratch_shapes=[pltpu.VMEM((2, D, tj), F32), pltpu.VMEM((2, D, tj), F32),
                            pltpu.VMEM((D, tj), BF16), pltpu.VMEM((D, tj), BF16),
                            pltpu.SemaphoreType.DMA((2, 2))],
        ),
        out_shape=jax.ShapeDtypeStruct((P, De), BF16),
        compiler_params=_cparams(("arbitrary", "arbitrary")),
        name="moe_up",
    )(*plan, xg, w_gate, w_up)


def _moe_down_kernel(*refs, layer):
    plan, (h_ref, wd_hbm, y_ref, wd_f32, wd_bf, sem) = refs[:5], refs[5:]
    blk_e_ref, nblk_ref = plan[0], plan[1]
    b = pl.program_id(1)

    @pl.when(b < nblk_ref[0])
    def _():
        @pl.when(_new_expert(blk_e_ref, b))
        def _():
            _next_weights(plan, (wd_hbm,), (wd_f32,), (wd_bf,), sem, layer)

        y = _dot(h_ref[...], wd_bf[...])
        for cc in range(y_ref.shape[1]):
            y_ref[:, cc, :] = y[:, cc * LANES:(cc + 1) * LANES]

    @pl.when(b >= nblk_ref[0])
    def _():
        y_ref[...] = jnp.zeros(y_ref.shape, F32)


def _moe_down(h, w_down, layer, plan):
    P, De = h.shape
    D = w_down.shape[3]
    tn = _pick(D, (4096, 2048, 1024, 512, 256, 128))
    nb = P // MOE_ROWS
    return pl.pallas_call(
        functools.partial(_moe_down_kernel, layer=layer),
        grid_spec=pltpu.PrefetchScalarGridSpec(
            num_scalar_prefetch=len(plan),
            grid=(D // tn, nb),
            in_specs=[pl.BlockSpec((MOE_ROWS, De), lambda n, b, *_: (b, 0)),
                      pl.BlockSpec(memory_space=pl.ANY)],
            out_specs=pl.BlockSpec((MOE_ROWS, tn // LANES, LANES), lambda n, b, *_: (b, n, 0)),
            scratch_shapes=[pltpu.VMEM((2, De, tn), F32), pltpu.VMEM((De, tn), BF16),
                            pltpu.SemaphoreType.DMA((1, 2))],
        ),
        out_shape=jax.ShapeDtypeStruct((P, D // LANES, LANES), F32),
        compiler_params=_cparams(("arbitrary", "arbitrary")),
        name="moe_down",
    )(*plan, h, w_down)


def _dispatch_plan(eid, rank, counts):
    R = eid.shape[0]
    S = R * TOP_K
    n_experts = counts.shape[0]
    e_flat = eid.reshape(-1)
    padded = (counts + MOE_ROWS - 1) // MOE_ROWS * MOE_ROWS
    pad_end = jnp.cumsum(padded)
    dest = ((pad_end - padded)[e_flat] + rank.reshape(-1)).astype(jnp.int32)
    nb = -(-S // MOE_ROWS) + n_experts
    buf_tok = jnp.full((nb * MOE_ROWS,), -1, jnp.int32).at[dest].set(jnp.arange(S, dtype=jnp.int32) // TOP_K)
    blk_e = jnp.minimum(jnp.searchsorted(pad_end, jnp.arange(nb, dtype=jnp.int32) * MOE_ROWS, side='right'),
                        n_experts - 1).astype(jnp.int32)
    nblk = (pad_end[-1:] // MOE_ROWS).astype(jnp.int32)
    ids = jnp.arange(n_experts, dtype=jnp.int32)
    live = counts > 0
    later = jnp.where(live[None, :] & (ids[None, :] > ids[:, None]), ids[None, :], n_experts)
    nxt = jnp.min(later, axis=1)
    next_e = jnp.where(nxt == n_experts, jnp.argmax(live), nxt).astype(jnp.int32)
    run_idx = (jnp.cumsum(live) - 1).astype(jnp.int32)
    n_runs = jnp.sum(live).astype(jnp.int32)[None]
    return dest, buf_tok, (blk_e, nblk, run_idx, next_e, n_runs)


def _ple_kernel(slot_ref, x_ref, y_hbm, wts_ref, p_ref, g_ref, gd_ref, gu_ref, win_ref, gf_ref, *rest,
                final, head_tiles, n_tok):
    n_out = 2 if final else 1
    o_refs = rest[:n_out]
    buf0, buf1, ysel, sem = rest[n_out:]
    bufs = (buf0, buf1)
    m = pl.program_id(0)
    tm = x_ref.shape[0]
    nc = y_hbm.shape[1]

    def fetch(tile, slot):
        _issue_rows(y_hbm, lambda r: slot_ref[(r // tm) * n_tok + tile * tm + r % tm], TOP_K * tm,
                    bufs[slot], sem.at[slot])

    @pl.when(m == 0)
    def _():
        fetch(0, 0)

    for slot in range(2):
        @pl.when(m % 2 == slot)
        def _(slot=slot):
            @pl.when(m + 1 < pl.num_programs(0))
            def _():
                fetch(m + 1, 1 - slot)

            _wait_rows(TOP_K * tm, nc, bufs[slot], sem.at[slot])
            for k in range(TOP_K):
                for cc in range(nc):
                    ysel[k, :, cc * LANES:(cc + 1) * LANES] = (
                        bufs[slot][pl.ds(k * tm * _pitch(nc) + cc, tm, stride=_pitch(nc)), :])

    wts = wts_ref[...]
    x = x_ref[...]
    for k in range(TOP_K):
        x = x + ysel[k] * wts[:, k:k + 1]
    hn = _rms(x, g_ref[...]).astype(BF16)
    t = _dot(hn, gd_ref[...]).astype(BF16)
    gate = jax.nn.sigmoid(_dot(t, gu_ref[...]))
    pe = _dot(p_ref[...], win_ref[...])
    x = x + pe * gate
    if not final:
        o_refs[0][...] = x
        return
    x = _rms(x, gf_ref[...])
    m = pl.program_id(0)

    @pl.when(m < head_tiles)
    def _():
        o_refs[0][...] = x

    @pl.when(m >= head_tiles)
    def _():
        o_refs[1][...] = x


def _ple(x, y3, slot_rows, wts, p, g, gd, gu, win, gf, final, head_rows):
    R, D = x.shape
    Pd = p.shape[1]
    L = wts.shape[1]
    nc = y3.shape[1]
    tm = _pick(R, (128, 64, 32, 16, 8))
    nt = R // tm
    row = pl.BlockSpec((tm, D), lambda m, sl: (m, 0))
    fixed = lambda shape: pl.BlockSpec(shape, lambda m, sl: (0, 0))
    assert head_rows % tm == 0
    ht = head_rows // tm
    if final:
        out_specs = [pl.BlockSpec((tm, D), lambda m, sl: (jnp.minimum(m, ht - 1), 0)),
                     pl.BlockSpec((tm, D), lambda m, sl: (jnp.maximum(m - ht, 0), 0))]
        out_shape = [jax.ShapeDtypeStruct((head_rows, D), F32), jax.ShapeDtypeStruct((R - head_rows, D), F32)]
    else:
        out_specs = row
        out_shape = jax.ShapeDtypeStruct((R, D), F32)
    stage = pltpu.VMEM((TOP_K * tm * _pitch(nc), LANES), F32)
    return pl.pallas_call(
        functools.partial(_ple_kernel, final=final, head_tiles=ht, n_tok=R),
        grid_spec=pltpu.PrefetchScalarGridSpec(
            num_scalar_prefetch=1,
            grid=(nt,),
            in_specs=[
                row,
                pl.BlockSpec(memory_space=pl.ANY),
                pl.BlockSpec((tm, L), lambda m, sl: (m, 0)),
                pl.BlockSpec((tm, Pd), lambda m, sl: (m, 0)),
                fixed((1, D)), fixed((D, Pd)), fixed((Pd, D)), fixed((Pd, D)), fixed((1, D)),
            ],
            out_specs=out_specs,
            scratch_shapes=[stage, stage, pltpu.VMEM((TOP_K, tm, D), F32), pltpu.SemaphoreType.DMA((2,))],
        ),
        out_shape=out_shape,
        compiler_params=_cparams(("arbitrary",)),
        name="moe_combine_ple",
    )(slot_rows, x, y3, wts, p, g, gd, gu, win, gf)


def _moe_ple(x, p, layer, prm, final, head_rows):
    R, D = x.shape
    assert (R * TOP_K) % MOE_ROWS == 0 and D % LANES == 0
    n_grp = prm["moe_w_grp"].shape[2]
    n_exp = prm["moe_w_exp"].shape[2]
    L = LANES
    w_cat = jnp.concatenate([prm["moe_w_grp"][layer], prm["moe_w_exp"][layer],
                             jnp.zeros((D, L - n_grp - n_exp), F32)], axis=1)
    b_cat = jnp.concatenate([prm["moe_b_grp"][layer], prm["moe_b_exp"][layer],
                             jnp.zeros((L - n_grp - n_exp,), F32)])[None, :]
    hn, wts, route, cnt = _router(x, prm["norm_ffn"][layer][None, :], w_cat, b_cat, n_grp, n_exp // n_grp)
    dest, buf_tok, plan = _dispatch_plan(
        route[:, :TOP_K], route[:, TOP_K:2 * TOP_K], cnt[0, :n_exp].astype(jnp.int32))
    xg = _gather_rows(hn, buf_tok, plan[1])
    hmid = _moe_up(xg, prm["moe_w_gate"], prm["moe_w_up"], layer, plan)
    y3 = _moe_down(hmid, prm["moe_w_down"], layer, plan)
    slot_rows = dest.reshape(R, TOP_K).T.reshape(-1)
    return _ple(x, y3, slot_rows, wts, p, prm["norm_ple"][layer][None, :],
                prm["ple_gate_down"][layer].astype(BF16), prm["ple_gate_up"][layer].astype(BF16),
                prm["ple_w_in"][layer].astype(BF16), prm["norm_final"][None, :], final, head_rows)


def kernel(x_prompt, x_sample, state_rwkv_shift, state_rwkv_wkv, state_pool, p_prompt, p_sample,
           rwkv_mu, rwkv_w_rkv, rwkv_w_o, rwkv_w0, rwkv_w1, rwkv_w2, rwkv_a0, rwkv_a1, rwkv_a2,
           rwkv_g1, rwkv_g2, rwkv_k_k, rwkv_k_a, rwkv_r_k, rwkv_lnx_w, rwkv_lnx_b,
           pool_w, pool_scale, norm_mix, norm_ffn, norm_ple, norm_final,
           moe_w_grp, moe_b_grp, moe_w_exp, moe_b_exp, moe_w_gate, moe_w_up, moe_w_down,
           ple_w_in, ple_gate_down, ple_gate_up):
    Bp, Tp, D = x_prompt.shape
    Bs, Ts, _ = x_sample.shape
    depth = norm_mix.shape[0]
    H, hs = rwkv_r_k.shape[1], rwkv_r_k.shape[2]
    W = QUAD * hs
    assert D % W == 0 and Ts <= CHUNK and Tp % CHUNK == 0
    lay = _SeqLayout(Bp, Tp, Bs, Ts)
    Rp = Bp * Tp
    prm = dict(moe_w_grp=moe_w_grp, moe_b_grp=moe_b_grp, moe_w_exp=moe_w_exp, moe_b_exp=moe_b_exp,
               moe_w_gate=moe_w_gate, moe_w_up=moe_w_up, moe_w_down=moe_w_down, norm_ffn=norm_ffn,
               norm_ple=norm_ple, norm_final=norm_final, ple_w_in=ple_w_in, ple_gate_down=ple_gate_down,
               ple_gate_up=ple_gate_up)

    x = jnp.concatenate([x_prompt.reshape(Rp, D), x_sample.reshape(Bs * Ts, D)], axis=0)
    Pd = p_prompt.shape[-1]
    p_all = jnp.concatenate([p_prompt.reshape(depth, Rp, Pd), p_sample.reshape(depth, Bs * Ts, Pd)],
                            axis=1).astype(BF16)

    def slabs(vec):
        return vec.reshape(D // W, 1, W)

    shift_p, wkv_p, pool_p, shift_s, wkv_s, pool_s = [], [], [], [], [], []
    for i in range(depth):
        j = i // 2
        g_mix = norm_mix[i][None, :]
        if i % 2 == 0:
            shift_all = jnp.concatenate([jnp.zeros((Bp, D), F32), state_rwkv_shift[j]], axis=0)[:, None, :]
            mixed, h_last = _norm_mix(x, shift_all, g_mix, rwkv_mu[j], lay)
            xr, xw, xk, xv, xa, xg = mixed
            w_rkv = rwkv_w_rkv[j].astype(BF16)
            r = _mm(xr, w_rkv, w_group=0, slab_out=W, name="proj_r")
            k = _mm(xk, w_rkv, w_group=1, slab_out=W, name="proj_k")
            v = _mm(xv, w_rkv, w_group=2, slab_out=W, name="proj_v")
            lw = _lora(xw, rwkv_w1[j].astype(BF16), rwkv_w2[j].astype(BF16), rwkv_w0[j][None, :], "decay", W)
            a = _lora(xa, rwkv_a1[j].astype(BF16), rwkv_a2[j].astype(BF16), rwkv_a0[j][None, :], "aaa", W)
            gl = rwkv_g1.shape[2]
            glp = -(-gl // LANES) * LANES
            g1 = jnp.pad(rwkv_g1[j], ((0, 0), (0, glp - gl))).astype(BF16)
            g2 = jnp.pad(rwkv_g2[j], ((0, glp - gl), (0, 0))).astype(BF16)
            g = _lora(xg, g1, g2, jnp.zeros((1, D), F32), "gate", W)
            sprm = (slabs(rwkv_k_k[j]), slabs(rwkv_k_a[j]), slabs(rwkv_r_k[j].reshape(D)),
                    slabs(rwkv_lnx_w[j]), slabs(rwkv_lnx_b[j]))
            o_p, sp = _scan(r, k, v, lw, a, g, sprm, None, row0=0, n_seq=Bp, t_len=Tp, hs=hs)
            o_s, ss = _scan(r, k, v, lw, a, g, sprm, state_rwkv_wkv[j], row0=Rp, n_seq=Bs, t_len=Ts, hs=hs)
            o = jnp.concatenate([o_p, o_s], axis=1)
            x = _mm(o, rwkv_w_o[j].astype(BF16)[None], res=x, slab_in=True, name="proj_o")
            shift_p.append(h_last[:Bp, -1])
            shift_s.append(h_last[Bp:, -1])
            wkv_p.append(sp)
            wkv_s.append(ss)
        else:
            hist = jnp.concatenate([jnp.zeros((Bp, 16, D), F32),
                                    jnp.pad(state_pool[j], ((0, 0), (1, 0), (0, 0)))], axis=0)
            d, h_last = _pool(x, hist, g_mix, lay)
            x = _mm(d, pool_w[j].astype(BF16), scale=pool_scale[j][None, :], res=x, name="pool_proj")
            nh = state_pool.shape[2]
            pool_p.append(h_last[:Bp, SEQ_TILE - nh:])
            pool_s.append(h_last[Bp:, SEQ_TILE - nh:])
        x = _moe_ple(x, p_all[i], i, prm, final=(i == depth - 1), head_rows=Rp)

    y_prompt = x[0].reshape(Bp, Tp, D)
    y_sample = x[1].reshape(Bs, Ts, D)
    return (y_prompt, y_sample, jnp.stack(shift_p), jnp.stack(wkv_p), jnp.stack(pool_p),
            jnp.stack(shift_s), jnp.stack(wkv_s), jnp.stack(pool_s))
```
